```python
import math
import jax, jax.numpy as jnp
from jax import lax
import numpy as np

D_MODEL = 4096
BATCH = 4
SEQ = 4096
DEPTH = 4

CTX_LEN = 256
GRID_W = 64

MOD_RANK = 256
MIX_WIDTH = D_MODEL
H_A = 16
Q_LORA = 1024
KV_LORA = 512
NOPE_A = 128
ROPE_A = 64
V_A = 128
H_B = 16
DK_B = 128
DV_B = 128
SHORT_CONV = 3
H_C = 4
DK_C = 256
DV_C = 512
GATE_CAP = 15.0
H_D = 16
KV_D = 4
HD_D = 128
D_FF = 11008
FFN_CONV = 3

CHUNK = 64
Q_BLOCK = 128
ROPE_THETA = 10000.0
EPS = 1e-6
N_EVEN = (DEPTH + 1) // 2
N_ODD = DEPTH // 2

EVEN_COLS = (Q_LORA, KV_LORA, ROPE_A, 2 * H_B * DK_B + H_B * DV_B, H_B * DV_B, 4 * H_B)
ODD_COLS = (H_C * DK_C, H_C * DK_C, H_C * DV_C, H_C * DV_C, 4 * H_C, H_D * HD_D, KV_D * HD_D, KV_D * HD_D)
P_EVEN = sum(EVEN_COLS)
P_ODD = sum(ODD_COLS)

kernel_name = 'hybrid_mla_gdn_mlstm_gqa_dit_block'


def split_cols(p, widths):
    idx = np.cumsum(widths)[:-1].tolist()
    return jnp.split(p, idx, axis=-1)


def rms_norm(x, w):
    xf = x.astype(jnp.float32)
    y = xf * lax.rsqrt(jnp.mean(xf * xf, axis=-1, keepdims=True) + EPS)
    return y.astype(x.dtype) * w


def l2_norm(x):
    xf = x.astype(jnp.float32)
    return (xf * lax.rsqrt(jnp.sum(xf * xf, axis=-1, keepdims=True) + EPS)).astype(x.dtype)


def dwconv_centred(u, w):
    k = w.shape[0]
    return lax.conv_general_dilated(u, w[:, None, :].astype(u.dtype), window_strides=(1,),
                                    padding=[(k // 2, k // 2)],
                                    dimension_numbers=('NWC', 'WIO', 'NWC'),
                                    feature_group_count=u.shape[-1])


def axial_rope(n_rows, rot_dim):
    r, col = jnp.meshgrid(jnp.arange(n_rows), jnp.arange(GRID_W), indexing='ij')
    r = r.reshape(-1).astype(jnp.float32)
    col = col.reshape(-1).astype(jnp.float32)
    n_freq = rot_dim // 4
    inv = ROPE_THETA ** (-jnp.arange(n_freq, dtype=jnp.float32) / n_freq)
    ang = jnp.concatenate([r[:, None] * inv, col[:, None] * inv], axis=-1)
    return jnp.cos(ang), jnp.sin(ang)


def apply_rope(x, cos, sin):
    xr = x.astype(jnp.float32).reshape(*x.shape[:-1], -1, 2)
    x0, x1 = xr[..., 0], xr[..., 1]
    cc, ss = cos[:, None, :], sin[:, None, :]
    out = jnp.stack([x0 * cc - x1 * ss, x0 * ss + x1 * cc], axis=-1)
    return out.reshape(x.shape).astype(x.dtype)


def attend(q, k, v):
    B, T, HQ, DH = q.shape
    HK = k.shape[2]
    G = HQ // HK
    DV = v.shape[-1]
    nb = T // Q_BLOCK
    qb = q.reshape(B, nb, Q_BLOCK, HK, G, DH).transpose(1, 0, 3, 4, 2, 5)
    kt = k.transpose(0, 2, 1, 3)
    vt = v.transpose(0, 2, 1, 3)
    scale = DH ** -0.5

    def block(qblk):
        s = jnp.einsum('bhgqd,bhsd->bhgqs', qblk, kt).astype(jnp.float32) * scale
        p = jax.nn.softmax(s, axis=-1).astype(vt.dtype)
        return jnp.einsum('bhgqs,bhsv->bhgqv', p, vt)

    o = lax.map(block, qb)
    return o.transpose(1, 0, 4, 2, 3, 5).reshape(B, T, HQ * DV)


def gated_delta_chunked(q, k, v, g, beta, s0):
    dt = v.dtype
    B, H, T, DK = q.shape
    DV = v.shape[-1]
    n = T // CHUNK
    f = lambda a: a.astype(jnp.float32).reshape(B, H, n, CHUNK, *a.shape[3:])
    q, k, v, g, beta = f(q), f(k), f(v), f(g), f(beta)
    gc = jnp.cumsum(g, axis=-1)
    incl = jnp.tril(jnp.ones((CHUNK, CHUNK), bool))
    eye = jnp.eye(CHUNK, dtype=jnp.float32)
    decay = jnp.exp(jnp.where(incl, gc[..., :, None] - gc[..., None, :], -jnp.inf))
    kb = k * beta[..., None]
    m = jnp.einsum('bhnid,bhnjd->bhnij', kb, k) * decay * (1.0 - eye)
    t_inv = lax.linalg.triangular_solve(m + eye, jnp.broadcast_to(eye, m.shape), left_side=True,
                                        lower=True, unit_diagonal=True)
    u = jnp.einsum('bhnij,bhnje->bhnie', t_inv, v * beta[..., None])
    w = jnp.einsum('bhnij,bhnjd->bhnid', t_inv, kb * jnp.exp(gc)[..., None])
    qk = jnp.einsum('bhnid,bhnjd->bhnij', q, k) * decay
    qg = q * jnp.exp(gc)[..., None]
    kg = k * jnp.exp(gc[..., -1:] - gc)[..., None]
    g_last = jnp.exp(gc[..., -1])
    xs = tuple(jnp.moveaxis(a, 2, 0) for a in (u, w, qk, qg, kg, g_last))

    def step(s, xc):
        u_c, w_c, qk_c, qg_c, kg_c, gl_c = xc
        v_new = u_c - jnp.einsum('bhld,bhde->bhle', w_c, s)
        o = jnp.einsum('bhld,bhde->bhle', qg_c, s) + jnp.einsum('bhls,bhse->bhle', qk_c, v_new)
        s = s * gl_c[..., None, None] + jnp.einsum('bhld,bhle->bhde', kg_c, v_new)
        return s, o

    s, o = lax.scan(step, s0.astype(jnp.float32), xs)
    return jnp.moveaxis(o, 0, 2).reshape(B, H, T, DV).astype(dt), s


def mlstm_chunked(q, k, v, i_pre, f_log, state):
    dt = v.dtype
    B, H, T, DK = q.shape
    DV = v.shape[-1]
    n_chunks = T // CHUNK
    f = lambda a: jnp.moveaxis(a.astype(jnp.float32).reshape(B, H, n_chunks, CHUNK, *a.shape[3:]), 2, 0)
    xs = (f(q), f(k), f(v), f(i_pre), f(f_log))
    incl = jnp.tril(jnp.ones((CHUNK, CHUNK), bool))

    def step(carry, xc):
        c_mat, n_vec, m = carry
        q_c, k_c, v_c, i_c, f_c = xc
        b = jnp.cumsum(f_c, axis=-1)
        d = jnp.where(incl, b[..., :, None] - b[..., None, :] + i_c[..., None, :], -jnp.inf)
        inter = b + m[..., None]
        m_t = jnp.maximum(inter, jnp.max(d, axis=-1))
        a_inter = jnp.exp(inter - m_t)
        p = jnp.exp(d - m_t[..., None]) * jnp.einsum('bhld,bhsd->bhls', q_c, k_c)
        num = a_inter[..., None] * jnp.einsum('bhld,bhde->bhle', q_c, c_mat) + jnp.einsum('bhls,bhse->bhle', p, v_c)
        den = a_inter * jnp.einsum('bhld,bhd->bhl', q_c, n_vec) + jnp.sum(p, axis=-1)
        h = num / jnp.maximum(jnp.abs(den), jnp.exp(-m_t))[..., None]
        w_end = b[..., -1:] - b + i_c
        m_new = jnp.maximum(b[..., -1] + m, jnp.max(w_end, axis=-1))
        g_old = jnp.exp(b[..., -1] + m - m_new)
        g_s = jnp.exp(w_end - m_new[..., None])
        c_mat = g_old[..., None, None] * c_mat + jnp.einsum('bhl,bhld,bhle->bhde', g_s, k_c, v_c)
        n_vec = g_old[..., None] * n_vec + jnp.einsum('bhl,bhld->bhd', g_s, k_c)
        return (c_mat, n_vec, m_new), h

    state, h = lax.scan(step, tuple(s.astype(jnp.float32) for s in state), xs)
    return jnp.moveaxis(h, 0, 2).reshape(B, H, T, DV).astype(dt), state


def bidir_two_stream(scan_fn, init_state, ctx_seq, ctx_gates, lat_seq, lat_gates):
    outs_ctx, outs_lat = [], []
    for direction in range(2):
        rev = (lambda a: a) if direction == 0 else (lambda a: jnp.flip(a, axis=2))
        o_ctx, state = scan_fn(*[rev(a) for a in ctx_seq + ctx_gates[direction]], init_state)
        o_lat, _ = scan_fn(*[rev(a) for a in lat_seq + lat_gates[direction]], state)
        outs_ctx.append(rev(o_ctx))
        outs_lat.append(rev(o_lat))
    return outs_ctx[0] + outs_ctx[1], outs_lat[0] + outs_lat[1]


def mla_qkv(cq, ckv, kpe, w_qb, q_lora_norm, w_kvb, kv_lora_norm, q_head_norm, k_head_norm, rope):
    B, T, _ = cq.shape
    q = (rms_norm(cq, q_lora_norm) @ w_qb).reshape(B, T, H_A, NOPE_A + ROPE_A)
    kv = (rms_norm(ckv, kv_lora_norm) @ w_kvb).reshape(B, T, H_A, NOPE_A + V_A)
    q_nope = rms_norm(q[..., :NOPE_A], q_head_norm[:NOPE_A])
    q_pe = rms_norm(q[..., NOPE_A:], q_head_norm[NOPE_A:])
    k_nope = rms_norm(kv[..., :NOPE_A], k_head_norm[:NOPE_A])
    k_pe = rms_norm(kpe, k_head_norm[NOPE_A:])[:, :, None, :]
    if rope is not None:
        q_pe = apply_rope(q_pe, *rope)
        k_pe = apply_rope(k_pe, *rope)
    q = jnp.concatenate([q_nope, q_pe], axis=-1)
    k = jnp.concatenate([k_nope, jnp.broadcast_to(k_pe, (B, T, H_A, ROPE_A))], axis=-1)
    return q, k, kv[..., NOPE_A:]


def gdn_inputs(qkv, ab, conv_w, a_log, dt_bias):
    B, T, _ = qkv.shape
    qkv = jax.nn.silu(dwconv_centred(qkv, conv_w))
    q, k, v = split_cols(qkv, (H_B * DK_B, H_B * DK_B, H_B * DV_B))
    heads = lambda a, d: a.reshape(B, T, H_B, d).transpose(0, 2, 1, 3)
    q = l2_norm(heads(q, DK_B)) * DK_B ** -0.5
    k = l2_norm(heads(k, DK_B))
    v = heads(v, DV_B)
    ab = ab.reshape(B, T, 4, H_B).transpose(2, 0, 3, 1).astype(jnp.float32)
    g = -jnp.exp(a_log.astype(jnp.float32))[:, None, :, None] * jax.nn.softplus(ab[:2] + dt_bias[:, None, :, None])
    beta = jax.nn.sigmoid(ab[2:])
    return (q, k, v), ((g[0], beta[0]), (g[1], beta[1]))


def gdn_output(o, z, norm_w):
    B, H, T, DV = o.shape
    o = rms_norm(o.transpose(0, 2, 1, 3), norm_w)
    return (o * jax.nn.silu(z.reshape(B, T, H, DV))).reshape(B, T, H * DV)


def mlstm_inputs(q, k, v, gates, gate_bias):
    B, T, _ = q.shape
    heads = lambda a, d: a.reshape(B, T, H_C, d).transpose(0, 2, 1, 3)
    q = heads(q, DK_C) * DK_C ** -0.5
    k = heads(k, DK_C)
    v = heads(v, DV_C)
    g = gates.reshape(B, T, 4, H_C) + gate_bias
    g = (GATE_CAP * jnp.tanh(g / GATE_CAP)).transpose(2, 0, 3, 1)
    i_f, i_b = g[0], g[1]
    f_f, f_b = jax.nn.log_sigmoid(g[2]), jax.nn.log_sigmoid(g[3])
    return (q, k, v), ((i_f, f_f), (i_b, f_b))


def mlstm_output(h, o_pre, norm_w):
    B, H, T, DV = h.shape
    h = rms_norm(h.transpose(0, 2, 1, 3), norm_w).reshape(B, T, H * DV)
    return h * jax.nn.sigmoid(o_pre)


def gqa_qkv(q, k, v, q_norm, k_norm, rope):
    B, T, _ = q.shape
    q = rms_norm(q.reshape(B, T, H_D, HD_D), q_norm)
    k = rms_norm(k.reshape(B, T, KV_D, HD_D), k_norm)
    v = v.reshape(B, T, KV_D, HD_D)
    if rope is not None:
        q = apply_rope(q, *rope)
        k = apply_rope(k, *rope)
    return q, k, v


def even_mixers(p_lat, p_ctx, rope_a, w_qb, q_lora_norm, w_kvb, kv_lora_norm, q_head_norm, k_head_norm,
                gdn_conv, gdn_a_log, gdn_dt_bias, gdn_out_norm, need_ctx):
    cq_l, ckv_l, kpe_l, qkv_l, z_l, ab_l = split_cols(p_lat, EVEN_COLS)
    cq_c, ckv_c, kpe_c, qkv_c, z_c, ab_c = split_cols(p_ctx, EVEN_COLS)
    q_l, k_l, v_l = mla_qkv(cq_l, ckv_l, kpe_l, w_qb, q_lora_norm, w_kvb, kv_lora_norm, q_head_norm, k_head_norm, rope_a)
    q_c, k_c, v_c = mla_qkv(cq_c, ckv_c, kpe_c, w_qb, q_lora_norm, w_kvb, kv_lora_norm, q_head_norm, k_head_norm, None)
    a_lat = attend(q_l, jnp.concatenate([k_c, k_l], axis=1), jnp.concatenate([v_c, v_l], axis=1))
    seq_l, gates_l = gdn_inputs(qkv_l, ab_l, gdn_conv, gdn_a_log, gdn_dt_bias)
    seq_c, gates_c = gdn_inputs(qkv_c, ab_c, gdn_conv, gdn_a_log, gdn_dt_bias)
    s0 = jnp.zeros((p_lat.shape[0], H_B, DK_B, DV_B), jnp.float32)
    o_c, o_l = bidir_two_stream(gated_delta_chunked, s0, seq_c, gates_c, seq_l, gates_l)
    y_lat = jnp.concatenate([a_lat, gdn_output(o_l, z_l, gdn_out_norm)], axis=-1)
    if not need_ctx:
        return y_lat, None
    a_ctx = attend(q_c, k_c, v_c)
    y_ctx = jnp.concatenate([a_ctx, gdn_output(o_c, z_c, gdn_out_norm)], axis=-1)
    return y_lat, y_ctx


def odd_mixers(p_lat, p_ctx, rope_d, mlstm_gate_bias, mlstm_out_norm, gqa_q_norm, gqa_k_norm, need_ctx):
    mq_l, mk_l, mv_l, mo_l, mg_l, gq_l, gk_l, gv_l = split_cols(p_lat, ODD_COLS)
    mq_c, mk_c, mv_c, mo_c, mg_c, gq_c, gk_c, gv_c = split_cols(p_ctx, ODD_COLS)
    seq_l, gates_l = mlstm_inputs(mq_l, mk_l, mv_l, mg_l, mlstm_gate_bias)
    seq_c, gates_c = mlstm_inputs(mq_c, mk_c, mv_c, mg_c, mlstm_gate_bias)
    B = p_lat.shape[0]
    st0 = (jnp.zeros((B, H_C, DK_C, DV_C), jnp.float32), jnp.zeros((B, H_C, DK_C), jnp.float32),
           jnp.zeros((B, H_C), jnp.float32))
    h_c, h_l = bidir_two_stream(mlstm_chunked, st0, seq_c, gates_c, seq_l, gates_l)
    q_l, k_l, v_l = gqa_qkv(gq_l, gk_l, gv_l, gqa_q_norm, gqa_k_norm, rope_d)
    q_c, k_c, v_c = gqa_qkv(gq_c, gk_c, gv_c, gqa_q_norm, gqa_k_norm, None)
    d_lat = attend(q_l, jnp.concatenate([k_c, k_l], axis=1), jnp.concatenate([v_c, v_l], axis=1))
    y_lat = jnp.concatenate([mlstm_output(h_l, mo_l, mlstm_out_norm), d_lat], axis=-1)
    if not need_ctx:
        return y_lat, None
    d_ctx = attend(q_c, k_c, v_c)
    y_ctx = jnp.concatenate([mlstm_output(h_c, mo_c, mlstm_out_norm), d_ctx], axis=-1)
    return y_lat, y_ctx


def adaln(cond, w_down, w_up, b):
    return (jax.nn.silu(cond) @ w_down) @ w_up + b


def conv_ffn(u, w_up, conv_w, w_down):
    gate, up = jnp.split(dwconv_centred(u @ w_up, conv_w), 2, axis=-1)
    return (jax.nn.silu(gate) * up) @ w_down


def _normal(k, shape, scale):
    return jax.random.normal(k, shape, jnp.float32) * scale


def _gain(k, shape):
    return 1.0 + 0.05 * jax.random.normal(k, shape, jnp.float32)


def setup_inputs(seed: int = 0) -> dict:
    key = jax.random.key(seed)
    ks = jax.random.split(key, 32)
    D = D_MODEL
    dt = jnp.exp(jax.random.uniform(ks[22], (N_EVEN, 2, H_B), jnp.float32, math.log(1e-3), math.log(1e-1)))
    return {
        'x': _normal(ks[0], (BATCH, SEQ, D), 1.0),
        'c': _normal(ks[1], (BATCH, D), 1.0),
        'ctx': _normal(ks[2], (BATCH, CTX_LEN, D), 1.0),
        'c_ctx': _normal(ks[3], (D,), 1.0),
        'mod_down': _normal(ks[4], (DEPTH, D, MOD_RANK), D ** -0.5),
        'mod_up': _normal(ks[5], (DEPTH, MOD_RANK, 6 * D), 0.5 * MOD_RANK ** -0.5),
        'mod_bias': _normal(ks[6], (DEPTH, 6 * D), 0.02),
        'norm_mix': _gain(ks[7], (DEPTH, D)),
        'norm_ffn': _gain(ks[8], (DEPTH, D)),
        'ffn_up': _normal(ks[9], (DEPTH, D, 2 * D_FF), D ** -0.5),
        'ffn_conv': _normal(ks[10], (DEPTH, FFN_CONV, 2 * D_FF), FFN_CONV ** -0.5),
        'ffn_down': _normal(ks[11], (DEPTH, D_FF, D), D_FF ** -0.5),
        'ev_w_in': _normal(ks[12], (N_EVEN, D, P_EVEN), D ** -0.5),
        'ev_w_out': _normal(ks[13], (N_EVEN, MIX_WIDTH, D), MIX_WIDTH ** -0.5),
        'mla_w_qb': _normal(ks[14], (N_EVEN, Q_LORA, H_A * (NOPE_A + ROPE_A)), Q_LORA ** -0.5),
        'mla_q_lora_norm': _gain(ks[15], (N_EVEN, Q_LORA)),
        'mla_w_kvb': _normal(ks[16], (N_EVEN, KV_LORA, H_A * (NOPE_A + V_A)), KV_LORA ** -0.5),
        'mla_kv_lora_norm': _gain(ks[17], (N_EVEN, KV_LORA)),
        'mla_q_head_norm': _gain(ks[18], (N_EVEN, NOPE_A + ROPE_A)),
        'mla_k_head_norm': _gain(ks[19], (N_EVEN, NOPE_A + ROPE_A)),
        'gdn_conv': _normal(ks[20], (N_EVEN, SHORT_CONV, 2 * H_B * DK_B + H_B * DV_B), SHORT_CONV ** -0.5),
        'gdn_a_log': jnp.log(jax.random.uniform(ks[21], (N_EVEN, 2, H_B), jnp.float32, 1.0, 16.0)),
        'gdn_dt_bias': dt + jnp.log(-jnp.expm1(-dt)),
        'gdn_out_norm': _gain(ks[23], (N_EVEN, DV_B)),
        'od_w_in': _normal(ks[24], (N_ODD, D, P_ODD), D ** -0.5),
        'od_w_out': _normal(ks[25], (N_ODD, MIX_WIDTH, D), MIX_WIDTH ** -0.5),
        'mlstm_gate_bias': jnp.concatenate([_normal(ks[26], (N_ODD, 2, H_C), 0.1),
                                            3.0 + _normal(ks[27], (N_ODD, 2, H_C), 0.5)], axis=1),
        'mlstm_out_norm': _gain(ks[28], (N_ODD, H_C, DV_C)),
        'gqa_q_norm': _gain(ks[29], (N_ODD, HD_D)),
        'gqa_k_norm': _gain(ks[30], (N_ODD, HD_D)),
    }


def reference(x, c, ctx, c_ctx, mod_down, mod_up, mod_bias, norm_mix, norm_ffn, ffn_up, ffn_conv, ffn_down,
              ev_w_in, ev_w_out, mla_w_qb, mla_q_lora_norm, mla_w_kvb, mla_kv_lora_norm, mla_q_head_norm,
              mla_k_head_norm, gdn_conv, gdn_a_log, gdn_dt_bias, gdn_out_norm, od_w_in, od_w_out,
              mlstm_gate_bias, mlstm_out_norm, gqa_q_norm, gqa_k_norm):
    n_rows = x.shape[1] // GRID_W
    rope_a = axial_rope(n_rows, ROPE_A)
    rope_d = axial_rope(n_rows, HD_D)
    h_ctx = ctx
    for layer in range(DEPTH):
        need_ctx = layer < DEPTH - 1
        sh1, sc1, g1, sh2, sc2, g2 = jnp.split(
            adaln(c, mod_down[layer], mod_up[layer], mod_bias[layer])[:, None, :], 6, axis=-1)
        csh1, csc1, cg1, csh2, csc2, cg2 = jnp.split(
            adaln(c_ctx[None, :], mod_down[layer], mod_up[layer], mod_bias[layer])[:, None, :], 6, axis=-1)
        u_lat = rms_norm(x, norm_mix[layer]) * (1 + sc1) + sh1
        u_ctx = rms_norm(h_ctx, norm_mix[layer]) * (1 + csc1) + csh1
        j = layer // 2
        if layer % 2 == 0:
            w_in, w_out = ev_w_in[j], ev_w_out[j]
            y_lat, y_ctx = even_mixers(u_lat @ w_in, u_ctx @ w_in, rope_a, mla_w_qb[j], mla_q_lora_norm[j],
                                       mla_w_kvb[j], mla_kv_lora_norm[j], mla_q_head_norm[j], mla_k_head_norm[j],
                                       gdn_conv[j], gdn_a_log[j], gdn_dt_bias[j], gdn_out_norm[j], need_ctx)
        else:
            w_in, w_out = od_w_in[j], od_w_out[j]
            y_lat, y_ctx = odd_mixers(u_lat @ w_in, u_ctx @ w_in, rope_d, mlstm_gate_bias[j], mlstm_out_norm[j],
                                      gqa_q_norm[j], gqa_k_norm[j], need_ctx)
        x = x + g1 * (y_lat @ w_out)
        x = x + g2 * conv_ffn(rms_norm(x, norm_ffn[layer]) * (1 + sc2) + sh2,
                              ffn_up[layer], ffn_conv[layer], ffn_down[layer])
        if need_ctx:
            h_ctx = h_ctx + cg1 * (y_ctx @ w_out)
            h_ctx = h_ctx + cg2 * conv_ffn(rms_norm(h_ctx, norm_ffn[layer]) * (1 + csc2) + csh2,
                                           ffn_up[layer], ffn_conv[layer], ffn_down[layer])
    return x
```

```python
import functools
import math

import numpy as np
import jax
import jax.numpy as jnp
from jax import lax
from jax.experimental import pallas as pl
from jax.experimental.pallas import tpu as pltpu

F32 = jnp.float32
BF16 = jnp.bfloat16

D_MODEL = 4096
BATCH = 4
SEQ = 4096
DEPTH = 4
CTX_LEN = 256
GRID_W = 64
MOD_RANK = 256
H_A, Q_LORA, KV_LORA, NOPE_A, ROPE_A, V_A = 16, 1024, 512, 128, 64, 128
H_B, DK_B, DV_B, SHORT_CONV = 16, 128, 128, 3
H_C, DK_C, DV_C, GATE_CAP = 4, 256, 512, 15.0
H_D, KV_D, HD_D = 16, 4, 128
D_FF, FFN_CONV = 11008, 3
CHUNK = 64
ROPE_THETA = 10000.0
EPS = 1e-6

N_LAT = BATCH * SEQ
N_CTX = BATCH * CTX_LEN
MOD_ROWS = 8

VMEM_LIMIT_V7X = 56 * 1024 * 1024

TM = 1024
TM_NORM = 256
TM_RMS = 512
TQ = 256


def _cparams(sem):
    return pltpu.CompilerParams(dimension_semantics=sem, vmem_limit_bytes=VMEM_LIMIT_V7X)


def _mod_row(i, tm):
    return jnp.minimum((i * tm) // SEQ, BATCH)


def _adaln_kernel(cond_ref, down_ref, up_ref, bias_ref, o_ref):
    cond = cond_ref[...]
    a = cond * jax.nn.sigmoid(cond)
    t = jnp.dot(a, down_ref[0], preferred_element_type=F32, precision=lax.Precision.HIGHEST)
    o = jnp.dot(t, up_ref[0], preferred_element_type=F32, precision=lax.Precision.HIGHEST)
    o_ref[0] = o + bias_ref[0]


def adaln_all(cond, mod_down, mod_up, mod_bias):
    depth, d, rank = mod_down.shape
    n = mod_up.shape[-1]
    tn = d
    return pl.pallas_call(
        _adaln_kernel,
        grid=(depth, n // tn),
        in_specs=[
            pl.BlockSpec((MOD_ROWS, d), lambda l, j: (0, 0)),
            pl.BlockSpec((1, d, rank), lambda l, j: (l, 0, 0)),
            pl.BlockSpec((1, rank, tn), lambda l, j: (l, 0, j)),
            pl.BlockSpec((1, 1, tn), lambda l, j: (l, 0, j)),
        ],
        out_specs=pl.BlockSpec((1, MOD_ROWS, tn), lambda l, j: (l, 0, j)),
        out_shape=jax.ShapeDtypeStruct((depth, MOD_ROWS, n), F32),
        compiler_params=_cparams(("arbitrary", "arbitrary")),
        name="adaln",
    )(cond, mod_down, mod_up, mod_bias.reshape(depth, 1, n))


def _norm_mod_kernel(x_ref, w_ref, sh_ref, sc_ref, o_ref):
    x = x_ref[...]
    y = x * lax.rsqrt(jnp.mean(x * x, axis=-1, keepdims=True) + EPS)
    y = y * w_ref[...]
    o_ref[...] = (y * (1.0 + sc_ref[0]) + sh_ref[0]).astype(o_ref.dtype)


def norm_mod(xs, w, mod, shift_chunk):
    m, d = xs.shape
    tm = TM_NORM
    return pl.pallas_call(
        _norm_mod_kernel,
        grid=(m // tm,),
        in_specs=[
            pl.BlockSpec((tm, d), lambda i: (i, 0)),
            pl.BlockSpec((1, d), lambda i: (0, 0)),
            pl.BlockSpec((1, 1, d), lambda i: (_mod_row(i, tm), 0, shift_chunk)),
            pl.BlockSpec((1, 1, d), lambda i: (_mod_row(i, tm), 0, shift_chunk + 1)),
        ],
        out_specs=pl.BlockSpec((tm, d), lambda i: (i, 0)),
        out_shape=jax.ShapeDtypeStruct((m, d), BF16),
        compiler_params=_cparams(("parallel",)),
        name="norm_mod",
    )(xs, w.reshape(1, d), mod, mod)


def _rms_rows_kernel(x_ref, w_ref, o_ref):
    x = x_ref[...]
    y = x * lax.rsqrt(jnp.mean(x * x, axis=-1, keepdims=True) + EPS)
    o_ref[...] = (y * w_ref[...]).astype(o_ref.dtype)


def rms_rows(p, col_block, width, w):
    m = p.shape[0]
    tm = TM_RMS
    return pl.pallas_call(
        _rms_rows_kernel,
        grid=(m // tm,),
        in_specs=[
            pl.BlockSpec((tm, width), lambda i: (i, col_block)),
            pl.BlockSpec((1, width), lambda i: (0, 0)),
        ],
        out_specs=pl.BlockSpec((tm, width), lambda i: (i, 0)),
        out_shape=jax.ShapeDtypeStruct((m, width), BF16),
        compiler_params=_cparams(("parallel",)),
        name="rms_rows",
    )(p, w.reshape(1, width))


def _mm_kernel(*refs, nk, has_res):
    if has_res:
        a_ref, w_ref, res_ref, gate_ref, o_ref = refs[:5]
        rest = refs[5:]
    else:
        a_ref, w_ref, o_ref = refs[:3]
        rest = refs[3:]

    def finish(acc):
        if has_res:
            acc = res_ref[...] + gate_ref[0] * acc
        o_ref[...] = acc.astype(o_ref.dtype)

    part = jnp.dot(a_ref[...], w_ref[...], preferred_element_type=F32)
    if nk == 1:
        finish(part)
        return
    acc_ref = rest[0]
    k = pl.program_id(2)

    @pl.when(k == 0)
    def _():
        acc_ref[...] = part

    @pl.when(jnp.logical_and(k > 0, k < nk - 1))
    def _():
        acc_ref[...] += part

    @pl.when(k == nk - 1)
    def _():
        finish(acc_ref[...] + part)


def matmul(a, w, *, tn, tk, out_dtype, res=None, mod=None, gate_chunk=None):
    m, kdim = a.shape
    n = w.shape[1]
    tm, tn, tk = TM, min(tn, n), min(tk, kdim)
    assert m % tm == 0 and kdim % tk == 0
    nk = kdim // tk
    has_res = res is not None
    in_specs = [
        pl.BlockSpec((tm, tk), lambda i, j, k: (i, k)),
        pl.BlockSpec((tk, tn), lambda i, j, k: (k, j)),
    ]
    args = [a, w]
    if has_res:
        assert n % tn == 0 and mod.shape[-1] % tn == 0
        gate_off = gate_chunk * (n // tn)
        in_specs += [
            pl.BlockSpec((tm, tn), lambda i, j, k: (i, j)),
            pl.BlockSpec((1, 1, tn), lambda i, j, k: (_mod_row(i, tm), 0, gate_off + j)),
        ]
        args += [res, mod]
    scratch = [pltpu.VMEM((tm, tn), F32)] if nk > 1 else []
    return pl.pallas_call(
        functools.partial(_mm_kernel, nk=nk, has_res=has_res),
        grid=(m // tm, pl.cdiv(n, tn), nk),
        in_specs=in_specs,
        out_specs=pl.BlockSpec((tm, tn), lambda i, j, k: (i, j)),
        out_shape=jax.ShapeDtypeStruct((m, n), out_dtype),
        scratch_shapes=scratch,
        compiler_params=_cparams(("parallel", "parallel", "arbitrary")),
        name="matmul",
    )(*args)


HALO = 16


def _seq_pos(i, tm):
    r = i * tm + lax.broadcasted_iota(jnp.int32, (tm, 1), 0)
    pos = jnp.where(r < N_LAT, r & (SEQ - 1), (r - N_LAT) & (CTX_LEN - 1))
    last = jnp.where(r < N_LAT, SEQ - 1, CTX_LEN - 1)
    return pos == 0, pos == last


def _conv3_rows(h, h_prev_row, h_next_row, cw, first, last, tm):
    t = lax.broadcasted_iota(jnp.int32, (tm, 1), 0)
    dn = jnp.where(t == 0, h_prev_row, pltpu.roll(h, 1, 0))
    dn = jnp.where(first, 0.0, dn)
    up = jnp.where(t == tm - 1, h_next_row, pltpu.roll(h, tm - 1, 0))
    up = jnp.where(last, 0.0, up)
    return cw[0:1, :] * dn + cw[1:2, :] * h + cw[2:3, :] * up


def _ffn_up_kernel(a_ref, ap_ref, an_ref, wg_ref, wu_ref, cg_ref, cu_ref, o_ref, *, tm):
    i = pl.program_id(0)
    first, last = _seq_pos(i, tm)
    a = a_ref[...]
    ap = ap_ref[...]
    an = an_ref[...]

    def branch(w_ref, c_ref):
        w = w_ref[...]
        h = jnp.dot(a, w, preferred_element_type=F32)
        hp = jnp.dot(ap, w, preferred_element_type=F32)[HALO - 1:HALO, :]
        hn = jnp.dot(an, w, preferred_element_type=F32)[0:1, :]
        return _conv3_rows(h, hp, hn, c_ref[...], first, last, tm)

    g = branch(wg_ref, cg_ref)
    u = branch(wu_ref, cu_ref)
    o_ref[...] = (g * jax.nn.sigmoid(g) * u).astype(o_ref.dtype)


def ffn_up_proj(u2, w_up, conv_w, *, tn=256):
    m, d = u2.shape
    tm = TM
    dff = w_up.shape[1] // 2
    nj = dff // tn
    assert dff % tn == 0 and m % tm == 0 and tm % HALO == 0
    hb = tm // HALO
    last_halo = m // HALO - 1
    return pl.pallas_call(
        functools.partial(_ffn_up_kernel, tm=tm),
        grid=(m // tm, nj),
        in_specs=[
            pl.BlockSpec((tm, d), lambda i, j: (i, 0)),
            pl.BlockSpec((HALO, d), lambda i, j: (jnp.maximum(i * hb - 1, 0), 0)),
            pl.BlockSpec((HALO, d), lambda i, j: (jnp.minimum((i + 1) * hb, last_halo), 0)),
            pl.BlockSpec((d, tn), lambda i, j: (0, j)),
            pl.BlockSpec((d, tn), lambda i, j: (0, j + nj)),
            pl.BlockSpec((FFN_CONV, tn), lambda i, j: (0, j)),
            pl.BlockSpec((FFN_CONV, tn), lambda i, j: (0, j + nj)),
        ],
        out_specs=pl.BlockSpec((tm, tn), lambda i, j: (i, j)),
        out_shape=jax.ShapeDtypeStruct((m, dff), BF16),
        compiler_params=_cparams(("parallel", "arbitrary")),
        name="ffn_up",
    )(u2, u2, u2, w_up, w_up, conv_w, conv_w)


def _attn_kernel(q_ref, k_ref, v_ref, o_ref, *, scale):
    q = q_ref[0, 0]
    k = k_ref[0, 0]
    v = v_ref[0, 0]
    s = lax.dot_general(q, k, (((1,), (1,)), ((), ())), preferred_element_type=F32) * scale
    m = jnp.max(s, axis=-1, keepdims=True)
    p = jnp.exp(s - m)
    l = jnp.sum(p, axis=-1, keepdims=True)
    o = jnp.dot(p.astype(v.dtype), v, preferred_element_type=F32)
    o_ref[0] = (o / l).astype(o_ref.dtype)


def attention(q, k, v):
    b, h, t, dk = q.shape
    hk, s = k.shape[1], k.shape[2]
    dv = v.shape[-1]
    g = h // hk
    tq = min(TQ, t)
    return pl.pallas_call(
        functools.partial(_attn_kernel, scale=dk ** -0.5),
        grid=(b, h, t // tq),
        in_specs=[
            pl.BlockSpec((1, 1, tq, dk), lambda bi, hi, ti: (bi, hi, ti, 0)),
            pl.BlockSpec((1, 1, s, dk), lambda bi, hi, ti: (bi, hi // g, 0, 0)),
            pl.BlockSpec((1, 1, s, dv), lambda bi, hi, ti: (bi, hi // g, 0, 0)),
        ],
        out_specs=pl.BlockSpec((1, tq, dv), lambda bi, hi, ti: (bi, ti, hi)),
        out_shape=jax.ShapeDtypeStruct((b, t, h * dv), BF16),
        compiler_params=_cparams(("parallel", "parallel", "arbitrary")),
        name="attention",
    )(q, k, v)


def _col_from_row(row, eye):
    return jnp.sum(jnp.where(eye, row, 0.0), axis=1, keepdims=True)


def _tri_masks(n, rev):
    ii = lax.broadcasted_iota(jnp.int32, (n, n), 0)
    jj = lax.broadcasted_iota(jnp.int32, (n, n), 1)
    eye = ii == jj
    if rev:
        be, be_t = jj >= ii, ii >= jj
    else:
        be, be_t = jj <= ii, ii <= jj
    return ii, jj, eye, be, be_t


def _unit_tri_inverse(mm, ii, jj, eye, rev):
    n = mm.shape[0]
    lo, hi = (1, 0) if not rev else (0, 1)
    t_inv = None
    s = 1
    while s < n:
        lvl = jnp.logical_and(
            (ii // (2 * s)) == (jj // (2 * s)),
            jnp.logical_and(((ii // s) & 1) == lo, ((jj // s) & 1) == hi))
        mc = jnp.where(lvl, mm, 0.0)
        if t_inv is None:
            t_inv = jnp.where(eye, 1.0, 0.0) - mc
        else:
            tb = t_inv.astype(BF16)
            y = jnp.dot(tb, mc.astype(BF16), preferred_element_type=F32)
            t_inv = t_inv - jnp.dot(y.astype(BF16), tb, preferred_element_type=F32)
        s *= 2
    return t_inv


def _gdn_chunk(q, k, v, g_row, b_row, state, rev):
    n = q.shape[0]
    dv = v.shape[1]
    ii, jj, eye, be, be_t = _tri_masks(n, rev)
    g_col = _col_from_row(g_row, eye)
    b_col = _col_from_row(b_row, eye)
    gc_col = jnp.sum(jnp.where(be, g_row, 0.0), axis=1, keepdims=True)
    gc_row = jnp.sum(jnp.where(be_t, g_col, 0.0), axis=0, keepdims=True)
    tot = jnp.sum(g_row, axis=1, keepdims=True)
    decay = jnp.where(be, jnp.exp(jnp.where(be, gc_col - gc_row, 0.0)), 0.0)
    kb = k * b_col
    kq = lax.dot_general(jnp.concatenate([kb, q], axis=0).astype(BF16), k.astype(BF16),
                         (((1,), (1,)), ((), ())), preferred_element_type=F32)
    mm = jnp.where(jnp.logical_and(be, jnp.logical_not(eye)), kq[:n] * decay, 0.0)
    qk = kq[n:] * decay
    t_inv = _unit_tri_inverse(mm, ii, jj, eye, rev)
    egc = jnp.exp(gc_col)
    rhs = jnp.concatenate([v * b_col, kb * egc], axis=1).astype(BF16)
    uw = jnp.dot(t_inv.astype(BF16), rhs, preferred_element_type=F32)
    u, w = uw[:, :dv], uw[:, dv:]
    qg = q * egc
    kg = k * jnp.exp(tot - gc_col)
    ws = jnp.dot(jnp.concatenate([w, qg], axis=0).astype(BF16), state.astype(BF16), preferred_element_type=F32)
    v_new = u - ws[:n]
    o = ws[n:] + jnp.dot(qk.astype(BF16), v_new.astype(BF16), preferred_element_type=F32)
    upd = lax.dot_general(kg.astype(BF16), v_new.astype(BF16), (((0,), (0,)), ((), ())),
                          preferred_element_type=F32)
    return o, state * jnp.exp(tot) + upd


def _bidir_scan(chunk_fn, n_chunks, init_f, init_b, o_ref, unroll):
    half = n_chunks // 2
    assert n_chunks == 2 * half

    def rows(c):
        return pl.ds(pl.multiple_of(c * CHUNK, CHUNK), CHUNK)

    def step(t, carry, accumulate):
        cf, cb = carry
        c_f = t
        c_b = n_chunks - 1 - t
        o_f, cf = chunk_fn(c_f, cf, False)
        o_b, cb = chunk_fn(c_b, cb, True)
        if accumulate:
            o_ref[rows(c_f), :] += o_f
            o_ref[rows(c_b), :] += o_b
        else:
            o_ref[rows(c_f), :] = o_f
            o_ref[rows(c_b), :] = o_b
        return cf, cb

    carry = lax.fori_loop(0, half, lambda t, c: step(t, c, False), (init_f, init_b), unroll=unroll)
    return lax.fori_loop(half, n_chunks, lambda t, c: step(t, c, True), carry, unroll=unroll)


def _gdn_kernel(qc_ref, kc_ref, vc_ref, gc_ref, ql_ref, kl_ref, vl_ref, gl_ref, oc_ref, ol_ref, *, unroll):
    def make(q_ref, k_ref, v_ref, g_ref):
        def fn(c, state, rev):
            r = pl.ds(pl.multiple_of(c * CHUNK, CHUNK), CHUNK)
            gates = g_ref[0, 0, c]
            d = 1 if rev else 0
            return _gdn_chunk(q_ref[0, 0, r, :], k_ref[0, 0, r, :], v_ref[0, 0, r, :],
                              gates[d:d + 1, :], gates[2 + d:3 + d, :], state, rev)
        return fn

    zero = jnp.zeros((kc_ref.shape[-1], vc_ref.shape[-1]), F32)
    s_f, s_b = _bidir_scan(make(qc_ref, kc_ref, vc_ref, gc_ref), qc_ref.shape[2] // CHUNK, zero, zero,
                           oc_ref.at[0, 0], unroll)
    _bidir_scan(make(ql_ref, kl_ref, vl_ref, gl_ref), ql_ref.shape[2] // CHUNK, s_f, s_b, ol_ref.at[0, 0], unroll)


def gdn_scan(qkv_c, gates_c, qkv_l, gates_l, *, unroll=2):
    b, h = qkv_l[0].shape[:2]

    def specs(qkv, gates):
        out = [pl.BlockSpec((1, 1) + a.shape[2:], lambda bi, hi: (bi, hi, 0, 0)) for a in qkv]
        out.append(pl.BlockSpec((1, 1) + gates.shape[2:], lambda bi, hi: (bi, hi, 0, 0, 0)))
        return out

    o_shapes = [jax.ShapeDtypeStruct(qkv_c[2].shape, F32), jax.ShapeDtypeStruct(qkv_l[2].shape, F32)]
    return pl.pallas_call(
        functools.partial(_gdn_kernel, unroll=unroll),
        grid=(b, h),
        in_specs=specs(qkv_c, gates_c) + specs(qkv_l, gates_l),
        out_specs=[pl.BlockSpec((1, 1) + s.shape[2:], lambda bi, hi: (bi, hi, 0, 0)) for s in o_shapes],
        out_shape=o_shapes,
        compiler_params=_cparams(("parallel", "parallel")),
        name="gdn_scan",
    )(*qkv_c, gates_c, *qkv_l, gates_l)


def _mlstm_chunk(q, k, v, i_row, f_row, carry, c_ref, rev):
    n_vec, m = carry
    n = q.shape[0]
    ii, jj, eye, be, be_t = _tri_masks(n, rev)
    f_col = _col_from_row(f_row, eye)
    i_col = _col_from_row(i_row, eye)
    b_col = jnp.sum(jnp.where(be, f_row, 0.0), axis=1, keepdims=True)
    b_row = jnp.sum(jnp.where(be_t, f_col, 0.0), axis=0, keepdims=True)
    b_tot = jnp.sum(f_row, axis=1, keepdims=True)
    neg = -jnp.inf
    d = jnp.where(be, b_col - b_row + i_row, neg)
    inter = b_col + m
    m_t = jnp.maximum(inter, jnp.max(d, axis=1, keepdims=True))
    a_inter = jnp.exp(inter - m_t)
    qb = q.astype(BF16)
    kbf = k.astype(BF16)
    s = lax.dot_general(qb, kbf, (((1,), (1,)), ((), ())), preferred_element_type=F32)
    p = jnp.exp(d - m_t) * s
    c_mat = c_ref[...]
    num = a_inter * jnp.dot(qb, c_mat.astype(BF16), preferred_element_type=F32) + jnp.dot(
        p.astype(BF16), v.astype(BF16), preferred_element_type=F32)
    den = a_inter * jnp.sum(q * n_vec, axis=1, keepdims=True) + jnp.sum(p, axis=1, keepdims=True)
    h = num / jnp.maximum(jnp.abs(den), jnp.exp(-m_t))
    w_end_row = b_tot - b_row + i_row
    w_end_col = b_tot - b_col + i_col
    m_new = jnp.maximum(b_tot + m, jnp.max(w_end_row, axis=1, keepdims=True))
    g_old = jnp.exp(b_tot + m - m_new)
    gs_col = jnp.exp(w_end_col - m_new)
    upd = lax.dot_general(kbf, (gs_col * v).astype(BF16), (((0,), (0,)), ((), ())), preferred_element_type=F32)
    c_ref[...] = g_old * c_mat + upd
    n_new = g_old * n_vec + jnp.sum(gs_col * k, axis=0, keepdims=True)
    return h, (n_new, m_new)


def _mlstm_kernel(qc_ref, kc_ref, vc_ref, gc_ref, ql_ref, kl_ref, vl_ref, gl_ref, oc_ref, ol_ref,
                  cf_ref, cb_ref, *, unroll):
    def make(q_ref, k_ref, v_ref, g_ref):
        def fn(c, carry, rev):
            r = pl.ds(pl.multiple_of(c * CHUNK, CHUNK), CHUNK)
            gates = g_ref[0, 0, c]
            d = 1 if rev else 0
            return _mlstm_chunk(q_ref[0, 0, r, :], k_ref[0, 0, r, :], v_ref[0, 0, r, :],
                                gates[d:d + 1, :], gates[2 + d:3 + d, :], carry,
                                cb_ref if rev else cf_ref, rev)
        return fn

    cf_ref[...] = jnp.zeros(cf_ref.shape, F32)
    cb_ref[...] = jnp.zeros(cb_ref.shape, F32)
    init = (jnp.zeros((1, kc_ref.shape[-1]), F32), jnp.zeros((1, 1), F32))
    st_f, st_b = _bidir_scan(make(qc_ref, kc_ref, vc_ref, gc_ref), qc_ref.shape[2] // CHUNK, init, init,
                             oc_ref.at[0, 0], unroll)
    _bidir_scan(make(ql_ref, kl_ref, vl_ref, gl_ref), ql_ref.shape[2] // CHUNK, st_f, st_b, ol_ref.at[0, 0], unroll)


def mlstm_scan(qkv_c, gates_c, qkv_l, gates_l, *, unroll=2):
    b, h = qkv_l[0].shape[:2]
    dk, dv = qkv_l[0].shape[-1], qkv_l[2].shape[-1]

    def specs(qkv, gates):
        out = [pl.BlockSpec((1, 1) + a.shape[2:], lambda bi, hi: (bi, hi, 0, 0)) for a in qkv]
        out.append(pl.BlockSpec((1, 1) + gates.shape[2:], lambda bi, hi: (bi, hi, 0, 0, 0)))
        return out

    o_shapes = [jax.ShapeDtypeStruct(qkv_c[2].shape, F32), jax.ShapeDtypeStruct(qkv_l[2].shape, F32)]
    return pl.pallas_call(
        functools.partial(_mlstm_kernel, unroll=unroll),
        grid=(b, h),
        in_specs=specs(qkv_c, gates_c) + specs(qkv_l, gates_l),
        out_specs=[pl.BlockSpec((1, 1) + s.shape[2:], lambda bi, hi: (bi, hi, 0, 0)) for s in o_shapes],
        out_shape=o_shapes,
        scratch_shapes=[pltpu.VMEM((dk, dv), F32), pltpu.VMEM((dk, dv), F32)],
        compiler_params=_cparams(("parallel", "parallel")),
        name="mlstm_scan",
    )(*qkv_c, gates_c, *qkv_l, gates_l)


def _rms(x, w):
    return x * lax.rsqrt(jnp.mean(x * x, axis=-1, keepdims=True) + EPS) * w


def _l2(x):
    return x * lax.rsqrt(jnp.sum(x * x, axis=-1, keepdims=True) + EPS)


def _axial_rope(n_rows, rot_dim):
    r, col = jnp.meshgrid(jnp.arange(n_rows), jnp.arange(GRID_W), indexing="ij")
    r = r.reshape(-1).astype(F32)
    col = col.reshape(-1).astype(F32)
    n_freq = rot_dim // 4
    inv = ROPE_THETA ** (-jnp.arange(n_freq, dtype=F32) / n_freq)
    ang = jnp.concatenate([r[:, None] * inv, col[:, None] * inv], axis=-1)
    return jnp.cos(ang), jnp.sin(ang)


def _rope(x, cos, sin):
    xr = x.reshape(*x.shape[:-1], -1, 2)
    x0, x1 = xr[..., 0], xr[..., 1]
    cc, ss = cos[:, None, :], sin[:, None, :]
    return jnp.stack([x0 * cc - x1 * ss, x0 * ss + x1 * cc], axis=-1).reshape(x.shape)


def _split_streams(a, need_ctx_rows=True):
    c = a.shape[-1]
    return a[:N_LAT].reshape(BATCH, SEQ, c), a[N_LAT:].reshape(BATCH, CTX_LEN, c)


def _dwconv3(u, w):
    z = jnp.zeros_like(u[:, :1])
    dn = jnp.concatenate([z, u[:, :-1]], axis=1)
    up = jnp.concatenate([u[:, 1:], z], axis=1)
    return w[0] * dn + w[1] * u + w[2] * up


def _chunk_rows(g):
    b, h, t = g[0].shape
    rows = [a.reshape(b, h, t // CHUNK, 1, CHUNK) for a in g]
    rows += [jnp.zeros_like(rows[0])] * (8 - len(rows))
    return jnp.concatenate(rows, axis=3)


def _heads(a, h, d):
    b, t, _ = a.shape
    return a.reshape(b, t, h, d).transpose(0, 2, 1, 3)


def _unheads(a):
    b, h, t, d = a.shape
    return a.transpose(0, 2, 1, 3).reshape(b, t, h * d)


EV_CQ, EV_CKV, EV_QKV, EV_Z, EV_KPE, EV_AB = 0, 1024, 1536, 7680, 9728, 9792
P_EVEN = 9856
OD_MQ, OD_MK, OD_MV, OD_MO, OD_GQ, OD_GK, OD_GV, OD_MG = 0, 1024, 2048, 4096, 6144, 8192, 8704, 9216
P_ODD = 9232


def _reorder_even(w_in):
    cq, ckv, kpe, qkv, z, ab = jnp.split(w_in, np.cumsum([1024, 512, 64, 6144, 2048]).tolist(), axis=1)
    return jnp.concatenate([cq, ckv, qkv, z, kpe, ab], axis=1)


def _reorder_odd(w_in):
    mq, mk, mv, mo, mg, gq, gk, gv = jnp.split(
        w_in, np.cumsum([1024, 1024, 2048, 2048, 16, 2048, 512]).tolist(), axis=1)
    return jnp.concatenate([mq, mk, mv, mo, gq, gk, gv, mg], axis=1)


def _even_mixers(p, need_ctx, rope, w_qb, q_lora_norm, w_kvb, kv_lora_norm, q_head_norm, k_head_norm,
                 gdn_conv, a_log, dt_bias, out_norm):
    m = p.shape[0]
    qn = rms_rows(p, EV_CQ // Q_LORA, Q_LORA, q_lora_norm)
    kvn = rms_rows(p, EV_CKV // KV_LORA, KV_LORA, kv_lora_norm)
    q = matmul(qn, w_qb, tn=1024, tk=Q_LORA, out_dtype=F32)
    kv = matmul(kvn, w_kvb, tn=1024, tk=KV_LORA, out_dtype=F32)
    q = q.reshape(m, H_A, NOPE_A + ROPE_A)
    kv = kv.reshape(m, H_A, NOPE_A + V_A)
    q_nope = _rms(q[..., :NOPE_A], q_head_norm[:NOPE_A])
    q_pe = _rms(q[..., NOPE_A:], q_head_norm[NOPE_A:])
    k_nope = _rms(kv[..., :NOPE_A], k_head_norm[:NOPE_A])
    v = kv[..., NOPE_A:]
    k_pe = _rms(p[:, EV_KPE:EV_KPE + ROPE_A], k_head_norm[NOPE_A:])[:, None, :]
    cos, sin = rope

    def stream(a, lo, hi, t, roped):
        a = a[lo:hi].reshape(BATCH, t, a.shape[1], a.shape[2])
        return _rope(a, cos, sin) if roped else a

    def assemble(lo, hi, t, roped):
        qh = jnp.concatenate([stream(q_nope, lo, hi, t, False), stream(q_pe, lo, hi, t, roped)], axis=-1)
        kp = jnp.broadcast_to(stream(k_pe, lo, hi, t, roped), (BATCH, t, H_A, ROPE_A))
        kh = jnp.concatenate([stream(k_nope, lo, hi, t, False), kp], axis=-1)
        vh = stream(v, lo, hi, t, False)
        tr = lambda a: a.transpose(0, 2, 1, 3).astype(BF16)
        return tr(qh), tr(kh), tr(vh)

    q_l, k_l, v_l = assemble(0, N_LAT, SEQ, True)
    q_c, k_c, v_c = assemble(N_LAT, m, CTX_LEN, False)
    a_lat = attention(q_l, jnp.concatenate([k_c, k_l], axis=2), jnp.concatenate([v_c, v_l], axis=2))
    qkv_l, qkv_c = _split_streams(p[:, EV_QKV:EV_Z])
    ab_l, ab_c = _split_streams(p[:, EV_AB:EV_AB + 4 * H_B])

    def gdn_in(qkv, ab):
        b, t, _ = qkv.shape
        x = _dwconv3(qkv, gdn_conv)
        x = x * jax.nn.sigmoid(x)
        qg, kg, vg = jnp.split(x, 3, axis=-1)
        qg = _l2(_heads(qg, H_B, DK_B)) * DK_B ** -0.5
        kg = _l2(_heads(kg, H_B, DK_B))
        vg = _heads(vg, H_B, DV_B)
        ab = ab.reshape(b, t, 4, H_B).transpose(2, 0, 3, 1)
        g = -jnp.exp(a_log)[:, None, :, None] * jax.nn.softplus(ab[:2] + dt_bias[:, None, :, None])
        beta = jax.nn.sigmoid(ab[2:])
        return (qg, kg, vg), _chunk_rows([g[0], g[1], beta[0], beta[1]])

    in_l, gates_l = gdn_in(qkv_l, ab_l)
    in_c, gates_c = gdn_in(qkv_c, ab_c)
    o_c, o_l = gdn_scan(in_c, gates_c, in_l, gates_l)
    z_l, z_c = _split_streams(p[:, EV_Z:EV_KPE])

    def gdn_out(o, z):
        b, h, t, dv = o.shape
        o = _rms(o.transpose(0, 2, 1, 3), out_norm)
        z = z.reshape(b, t, h, dv)
        return (o * (z * jax.nn.sigmoid(z))).reshape(b, t, h * dv).astype(BF16)

    y_lat = jnp.concatenate([a_lat, gdn_out(o_l, z_l)], axis=-1).reshape(N_LAT, -1)
    if not need_ctx:
        return y_lat
    a_ctx = attention(q_c, k_c, v_c)
    y_ctx = jnp.concatenate([a_ctx, gdn_out(o_c, z_c)], axis=-1).reshape(N_CTX, -1)
    return jnp.concatenate([y_lat, y_ctx], axis=0)


def _odd_mixers(p, need_ctx, rope, gate_bias, out_norm, q_norm, k_norm):
    m = p.shape[0]
    def mlstm_in(lo, hi, t):
        sl = lambda off, width: p[lo:hi, off:off + width].reshape(BATCH, t, width)
        qm = _heads(sl(OD_MQ, H_C * DK_C), H_C, DK_C) * DK_C ** -0.5
        km = _heads(sl(OD_MK, H_C * DK_C), H_C, DK_C)
        vm = _heads(sl(OD_MV, H_C * DV_C), H_C, DV_C).astype(BF16)
        g = sl(OD_MG, 4 * H_C).reshape(BATCH, t, 4, H_C) + gate_bias
        g = (GATE_CAP * jnp.tanh(g / GATE_CAP)).transpose(2, 0, 3, 1)
        return (qm, km, vm), _chunk_rows([g[0], g[1], jax.nn.log_sigmoid(g[2]), jax.nn.log_sigmoid(g[3])])

    in_l, gates_l = mlstm_in(0, N_LAT, SEQ)
    in_c, gates_c = mlstm_in(N_LAT, m, CTX_LEN)
    h_c, h_l = mlstm_scan(in_c, gates_c, in_l, gates_l)

    def mlstm_out(h, lo, hi, t):
        o_pre = p[lo:hi, OD_MO:OD_MO + H_C * DV_C].reshape(BATCH, t, H_C * DV_C)
        h = _rms(h.transpose(0, 2, 1, 3), out_norm).reshape(BATCH, t, H_C * DV_C)
        return (h * jax.nn.sigmoid(o_pre)).astype(BF16)

    cos, sin = rope

    def gqa(lo, hi, t, roped):
        sl = lambda off, h: p[lo:hi, off:off + h * HD_D].reshape(BATCH, t, h, HD_D)
        qd = _rms(sl(OD_GQ, H_D), q_norm)
        kd = _rms(sl(OD_GK, KV_D), k_norm)
        vd = sl(OD_GV, KV_D)
        if roped:
            qd, kd = _rope(qd, cos, sin), _rope(kd, cos, sin)
        tr = lambda a: a.transpose(0, 2, 1, 3).astype(BF16)
        return tr(qd), tr(kd), tr(vd)

    q_l, k_l, v_l = gqa(0, N_LAT, SEQ, True)
    q_c, k_c, v_c = gqa(N_LAT, m, CTX_LEN, False)
    d_lat = attention(q_l, jnp.concatenate([k_c, k_l], axis=2), jnp.concatenate([v_c, v_l], axis=2))
    y_lat = jnp.concatenate([mlstm_out(h_l, 0, N_LAT, SEQ), d_lat], axis=-1).reshape(N_LAT, -1)
    if not need_ctx:
        return y_lat
    d_ctx = attention(q_c, k_c, v_c)
    y_ctx = jnp.concatenate([mlstm_out(h_c, N_LAT, m, CTX_LEN), d_ctx], axis=-1).reshape(N_CTX, -1)
    return jnp.concatenate([y_lat, y_ctx], axis=0)


def kernel(x, c, ctx, c_ctx, mod_down, mod_up, mod_bias, norm_mix, norm_ffn, ffn_up, ffn_conv, ffn_down,
           ev_w_in, ev_w_out, mla_w_qb, mla_q_lora_norm, mla_w_kvb, mla_kv_lora_norm, mla_q_head_norm,
           mla_k_head_norm, gdn_conv, gdn_a_log, gdn_dt_bias, gdn_out_norm, od_w_in, od_w_out,
           mlstm_gate_bias, mlstm_out_norm, gqa_q_norm, gqa_k_norm):
    n_rows = SEQ // GRID_W
    rope_a = _axial_rope(n_rows, ROPE_A)
    rope_d = _axial_rope(n_rows, HD_D)
    cond = jnp.concatenate([c, c_ctx[None, :], jnp.zeros((MOD_ROWS - BATCH - 1, D_MODEL), F32)], axis=0)
    mods = adaln_all(cond, mod_down, mod_up, mod_bias)
    xs = jnp.concatenate([x.reshape(N_LAT, D_MODEL), ctx.reshape(N_CTX, D_MODEL)], axis=0)
    for layer in range(DEPTH):
        need_ctx = layer < DEPTH - 1
        j = layer // 2
        mod = mods[layer].reshape(MOD_ROWS, 1, 6 * D_MODEL)
        u = norm_mod(xs, norm_mix[layer], mod, 0)
        if layer % 2 == 0:
            w_in = _reorder_even(ev_w_in[j]).astype(BF16)
            p = matmul(u, w_in, tn=1024, tk=D_MODEL, out_dtype=F32)
            y = _even_mixers(p, need_ctx, rope_a, mla_w_qb[j].astype(BF16), mla_q_lora_norm[j],
                             mla_w_kvb[j].astype(BF16), mla_kv_lora_norm[j], mla_q_head_norm[j],
                             mla_k_head_norm[j], gdn_conv[j], gdn_a_log[j], gdn_dt_bias[j], gdn_out_norm[j])
            w_out = ev_w_out[j]
        else:
            w_in = _reorder_odd(od_w_in[j]).astype(BF16)
            p = matmul(u, w_in, tn=1024, tk=D_MODEL, out_dtype=F32)
            y = _odd_mixers(p, need_ctx, rope_d, mlstm_gate_bias[j], mlstm_out_norm[j], gqa_q_norm[j],
                            gqa_k_norm[j])
            w_out = od_w_out[j]
        if not need_ctx:
            xs = xs[:N_LAT]
        xs = matmul(y, w_out.astype(BF16), tn=1024, tk=D_MODEL, out_dtype=F32,
                    res=xs, mod=mod, gate_chunk=2)
        u2 = norm_mod(xs, norm_ffn[layer], mod, 3)
        hmid = ffn_up_proj(u2, ffn_up[layer].astype(BF16), ffn_conv[layer])
        xs = matmul(hmid, ffn_down[layer].astype(BF16), tn=2048, tk=256, out_dtype=F32,
                    res=xs, mod=mod, gate_chunk=5)
    return xs.reshape(BATCH, SEQ, D_MODEL)
```

```python
import functools
import math

import numpy as np
import jax
import jax.numpy as jnp
from jax import lax
from jax.experimental import pallas as pl
from jax.experimental.pallas import tpu as pltpu

F32 = jnp.float32
BF16 = jnp.bfloat16

D_MODEL = 4096
BATCH = 4
SEQ = 4096
DEPTH = 4
CTX_LEN = 256
GRID_W = 64
MOD_RANK = 256
H_A, Q_LORA, KV_LORA, NOPE_A, ROPE_A, V_A = 16, 1024, 512, 128, 64, 128
H_B, DK_B, DV_B, SHORT_CONV = 16, 128, 128, 3
H_C, DK_C, DV_C, GATE_CAP = 4, 256, 512, 15.0
H_D, KV_D, HD_D = 16, 4, 128
D_FF, FFN_CONV = 11008, 3
CHUNK = 64
ROPE_THETA = 10000.0
EPS = 1e-6

N_LAT = BATCH * SEQ
N_CTX = BATCH * CTX_LEN
MOD_ROWS = 8
LANES = 128

VMEM_LIMIT_V7X = 56 * 1024 * 1024

TM = 1024
TM_NORM = 256
TM_RMS = 512
TQ = 256
GROUP = 256


def _cparams(sem):
    return pltpu.CompilerParams(dimension_semantics=sem, vmem_limit_bytes=VMEM_LIMIT_V7X)


def _mod_row(i, tm):
    return jnp.minimum((i * tm) // SEQ, BATCH)


def _adaln_kernel(cond_ref, down_ref, up_ref, bias_ref, o_ref):
    cond = cond_ref[...]
    a = cond * jax.nn.sigmoid(cond)
    t = jnp.dot(a, down_ref[0], preferred_element_type=F32, precision=lax.Precision.HIGHEST)
    o = jnp.dot(t, up_ref[0], preferred_element_type=F32, precision=lax.Precision.HIGHEST)
    o_ref[0] = o + bias_ref[0]


def adaln_all(cond, mod_down, mod_up, mod_bias):
    depth, d, rank = mod_down.shape
    n = mod_up.shape[-1]
    tn = d
    return pl.pallas_call(
        _adaln_kernel,
        grid=(depth, n // tn),
        in_specs=[
            pl.BlockSpec((MOD_ROWS, d), lambda l, j: (0, 0)),
            pl.BlockSpec((1, d, rank), lambda l, j: (l, 0, 0)),
            pl.BlockSpec((1, rank, tn), lambda l, j: (l, 0, j)),
            pl.BlockSpec((1, 1, tn), lambda l, j: (l, 0, j)),
        ],
        out_specs=pl.BlockSpec((1, MOD_ROWS, tn), lambda l, j: (l, 0, j)),
        out_shape=jax.ShapeDtypeStruct((depth, MOD_ROWS, n), F32),
        compiler_params=_cparams(("arbitrary", "arbitrary")),
        name="adaln",
    )(cond, mod_down, mod_up, mod_bias.reshape(depth, 1, n))


def _norm_mod_kernel(x_ref, w_ref, sh_ref, sc_ref, o_ref):
    x = x_ref[...]
    y = x * lax.rsqrt(jnp.mean(x * x, axis=-1, keepdims=True) + EPS)
    y = y * w_ref[...]
    o_ref[...] = (y * (1.0 + sc_ref[0]) + sh_ref[0]).astype(o_ref.dtype)


def norm_mod(xs, w, mod, shift_chunk):
    m, d = xs.shape
    tm = TM_NORM
    return pl.pallas_call(
        _norm_mod_kernel,
        grid=(m // tm,),
        in_specs=[
            pl.BlockSpec((tm, d), lambda i: (i, 0)),
            pl.BlockSpec((1, d), lambda i: (0, 0)),
            pl.BlockSpec((1, 1, d), lambda i: (_mod_row(i, tm), 0, shift_chunk)),
            pl.BlockSpec((1, 1, d), lambda i: (_mod_row(i, tm), 0, shift_chunk + 1)),
        ],
        out_specs=pl.BlockSpec((tm, d), lambda i: (i, 0)),
        out_shape=jax.ShapeDtypeStruct((m, d), BF16),
        compiler_params=_cparams(("parallel",)),
        name="norm_mod",
    )(xs, w.reshape(1, d), mod, mod)


def _rms_rows_kernel(x_ref, w_ref, o_ref):
    x = x_ref[...].astype(F32)
    y = x * lax.rsqrt(jnp.mean(x * x, axis=-1, keepdims=True) + EPS)
    o_ref[...] = (y * w_ref[...]).astype(o_ref.dtype)


def rms_rows(p, col_block, width, w):
    m = p.shape[0]
    tm = TM_RMS
    return pl.pallas_call(
        _rms_rows_kernel,
        grid=(m // tm,),
        in_specs=[
            pl.BlockSpec((tm, width), lambda i: (i, col_block)),
            pl.BlockSpec((1, width), lambda i: (0, 0)),
        ],
        out_specs=pl.BlockSpec((tm, width), lambda i: (i, 0)),
        out_shape=jax.ShapeDtypeStruct((m, width), BF16),
        compiler_params=_cparams(("parallel",)),
        name="rms_rows",
    )(p, w.reshape(1, width))


def _mm_kernel(*refs, n_parts, has_res):
    a_refs, w_refs, rest = refs[:n_parts], refs[n_parts:2 * n_parts], refs[2 * n_parts:]
    acc = jnp.dot(a_refs[0][...], w_refs[0][...], preferred_element_type=F32)
    for a_ref, w_ref in zip(a_refs[1:], w_refs[1:]):
        acc += jnp.dot(a_ref[...], w_ref[...], preferred_element_type=F32)
    if has_res:
        res_ref, gate_ref, o_ref = rest
        acc = res_ref[...] + gate_ref[0] * acc
    else:
        (o_ref,) = rest
    o_ref[...] = acc.astype(o_ref.dtype)


def matmul(a_parts, w, *, tn, out_dtype, res=None, mod=None, gate_chunk=None, single_buffer_a=False):
    if not isinstance(a_parts, (list, tuple)):
        a_parts = [a_parts]
    n_parts = len(a_parts)
    m, kp = a_parts[0].shape
    n = w.shape[1]
    tm, tn = TM, min(tn, n)
    assert m % tm == 0 and w.shape[0] == kp * n_parts and all(a.shape == (m, kp) for a in a_parts)
    has_res = res is not None
    a_mode = dict(pipeline_mode=pl.Buffered(1)) if single_buffer_a else {}
    in_specs = [pl.BlockSpec((tm, kp), lambda i, j: (i, 0), **a_mode) for _ in a_parts]
    in_specs += [pl.BlockSpec((kp, tn), functools.partial(lambda i, j, part: (part, j), part=part))
                 for part in range(n_parts)]
    args = list(a_parts) + [w] * n_parts
    if has_res:
        assert n % tn == 0 and mod.shape[-1] % tn == 0
        gate_off = gate_chunk * (n // tn)
        in_specs += [
            pl.BlockSpec((tm, tn), lambda i, j: (i, j)),
            pl.BlockSpec((1, 1, tn), lambda i, j: (_mod_row(i, tm), 0, gate_off + j)),
        ]
        args += [res, mod]
    return pl.pallas_call(
        functools.partial(_mm_kernel, n_parts=n_parts, has_res=has_res),
        grid=(m // tm, pl.cdiv(n, tn)),
        in_specs=in_specs,
        out_specs=pl.BlockSpec((tm, tn), lambda i, j: (i, j)),
        out_shape=jax.ShapeDtypeStruct((m, n), out_dtype),
        compiler_params=_cparams(("parallel", "arbitrary")),
        name="matmul",
    )(*args)


HALO = 16


def _seq_pos(i, tm):
    r = i * tm + lax.broadcasted_iota(jnp.int32, (tm, 1), 0)
    pos = jnp.where(r < N_LAT, r & (SEQ - 1), (r - N_LAT) & (CTX_LEN - 1))
    last = jnp.where(r < N_LAT, SEQ - 1, CTX_LEN - 1)
    return pos == 0, pos == last


def _conv3_rows(h, h_prev_row, h_next_row, cw, first, last):
    tm = h.shape[0]
    t = lax.broadcasted_iota(jnp.int32, (tm, 1), 0)
    dn = jnp.where(t == 0, h_prev_row, pltpu.roll(h, 1, 0))
    dn = jnp.where(first, 0.0, dn)
    up = jnp.where(t == tm - 1, h_next_row, pltpu.roll(h, tm - 1, 0))
    up = jnp.where(last, 0.0, up)
    return cw[0:1, :] * dn + cw[1:2, :] * h + cw[2:3, :] * up


def _ffn_up_kernel(a_ref, ap_ref, an_ref, wg_ref, wu_ref, cg_ref, cu_ref, o_ref, *, tm):
    i = pl.program_id(0)
    first, last = _seq_pos(i, tm)
    a = a_ref[...]
    ap = ap_ref[...]
    an = an_ref[...]

    def branch(w_ref, c_ref):
        w = w_ref[...]
        h = jnp.dot(a, w, preferred_element_type=F32)
        hp = jnp.dot(ap, w, preferred_element_type=F32)[HALO - 1:HALO, :]
        hn = jnp.dot(an, w, preferred_element_type=F32)[0:1, :]
        return _conv3_rows(h, hp, hn, c_ref[...], first, last)

    g = branch(wg_ref, cg_ref)
    u = branch(wu_ref, cu_ref)
    o_ref[...] = (g * jax.nn.sigmoid(g) * u).astype(o_ref.dtype)


def ffn_up_proj(u2, w_up, conv_w, *, tn=256):
    m, d = u2.shape
    tm = TM
    dff = w_up.shape[1] // 2
    nj = dff // tn
    assert dff % tn == 0 and m % tm == 0 and tm % HALO == 0
    hb = tm // HALO
    last_halo = m // HALO - 1
    return pl.pallas_call(
        functools.partial(_ffn_up_kernel, tm=tm),
        grid=(m // tm, nj),
        in_specs=[
            pl.BlockSpec((tm, d), lambda i, j: (i, 0)),
            pl.BlockSpec((HALO, d), lambda i, j: (jnp.maximum(i * hb - 1, 0), 0)),
            pl.BlockSpec((HALO, d), lambda i, j: (jnp.minimum((i + 1) * hb, last_halo), 0)),
            pl.BlockSpec((d, tn), lambda i, j: (0, j)),
            pl.BlockSpec((d, tn), lambda i, j: (0, j + nj)),
            pl.BlockSpec((FFN_CONV, tn), lambda i, j: (0, j)),
            pl.BlockSpec((FFN_CONV, tn), lambda i, j: (0, j + nj)),
        ],
        out_specs=pl.BlockSpec((tm, tn), lambda i, j: (i, j)),
        out_shape=jax.ShapeDtypeStruct((m, dff), BF16),
        compiler_params=_cparams(("parallel", "arbitrary")),
        name="ffn_up",
    )(u2, u2, u2, w_up, w_up, conv_w, conv_w)


def _attn_kernel(q_ref, k_ref, v_ref, o_ref, *, scale):
    q = q_ref[0, 0]
    k = k_ref[0, 0]
    v = v_ref[0, 0]
    s = lax.dot_general(q, k, (((1,), (1,)), ((), ())), preferred_element_type=F32) * scale
    m = jnp.max(s, axis=-1, keepdims=True)
    p = jnp.exp(s - m)
    l = jnp.sum(p, axis=-1, keepdims=True)
    o = jnp.dot(p.astype(v.dtype), v, preferred_element_type=F32)
    o_ref[0] = (o / l).astype(o_ref.dtype)


def attention(q, k, v):
    b, h, t, dk = q.shape
    hk, s = k.shape[1], k.shape[2]
    dv = v.shape[-1]
    g = h // hk
    tq = min(TQ, t)
    return pl.pallas_call(
        functools.partial(_attn_kernel, scale=dk ** -0.5),
        grid=(b, h, t // tq),
        in_specs=[
            pl.BlockSpec((1, 1, tq, dk), lambda bi, hi, ti: (bi, hi, ti, 0)),
            pl.BlockSpec((1, 1, s, dk), lambda bi, hi, ti: (bi, hi // g, 0, 0)),
            pl.BlockSpec((1, 1, s, dv), lambda bi, hi, ti: (bi, hi // g, 0, 0)),
        ],
        out_specs=pl.BlockSpec((1, tq, dv), lambda bi, hi, ti: (bi, ti, hi)),
        out_shape=jax.ShapeDtypeStruct((b, t, h * dv), BF16),
        compiler_params=_cparams(("parallel", "parallel", "arbitrary")),
        name="attention",
    )(q, k, v)


def _col_from_row(row, eye):
    return jnp.sum(jnp.where(eye, row, 0.0), axis=1, keepdims=True)


def _rows(idx, size):
    return pl.ds(pl.multiple_of(idx * size, size), size)


def _seq_conv_silu(x, cw):
    zero = jnp.zeros((1, x.shape[1]), F32)
    t = lax.broadcasted_iota(jnp.int32, (x.shape[0], 1), 0)
    c = _conv3_rows(x, zero, zero, cw, t == 0, t == x.shape[0] - 1)
    return c * jax.nn.sigmoid(c)


N_LEVELS = int(math.log2(CHUNK))
MASK_SAME, MASK_EYE, MASK_BE = 0, 1, 2
MASK_LEVEL = 4
N_MASKS = MASK_LEVEL + 2 * N_LEVELS


def _gdn_init_masks(mask_ref):
    n = GROUP
    ii = lax.broadcasted_iota(jnp.int32, (n, n), 0)
    jj = lax.broadcasted_iota(jnp.int32, (n, n), 1)
    same = (ii // CHUNK) == (jj // CHUNK)

    def put(idx, m):
        mask_ref[idx] = jnp.where(m, 1.0, 0.0)

    put(MASK_SAME, same)
    put(MASK_EYE, ii == jj)
    put(MASK_BE, jnp.logical_and(same, jj <= ii))
    put(MASK_BE + 1, jnp.logical_and(same, jj >= ii))
    for level in range(N_LEVELS):
        s = 1 << level
        blk = (ii // (2 * s)) == (jj // (2 * s))
        ih = (ii // s) & 1
        jh = (jj // s) & 1
        put(MASK_LEVEL + level, jnp.logical_and(blk, jnp.logical_and(ih == 1, jh == 0)))
        put(MASK_LEVEL + N_LEVELS + level, jnp.logical_and(blk, jnp.logical_and(ih == 0, jh == 1)))


def _gdn_prep_group(q, k, v, gates, mask_ref):
    def mask(idx):
        return mask_ref[idx] > 0.5

    eye = mask(MASK_EYE)
    same = mask(MASK_SAME)
    k16 = k.astype(BF16)
    dn_t = (((1,), (1,)), ((), ()))
    kk = lax.dot_general(k16, k16, dn_t, preferred_element_type=F32)
    qk0 = lax.dot_general(q.astype(BF16), k16, dn_t, preferred_element_type=F32)
    outs = []
    for d in range(2):
        be = mask(MASK_BE + d)
        be_t = mask(MASK_BE + 1 - d)
        g_row, b_row = gates[d:d + 1, :], gates[2 + d:3 + d, :]
        g_col = _col_from_row(g_row, eye)
        b_col = _col_from_row(b_row, eye)
        gc_col = jnp.sum(jnp.where(be, g_row, 0.0), axis=1, keepdims=True)
        gc_row = jnp.sum(jnp.where(be_t, g_col, 0.0), axis=0, keepdims=True)
        tot_col = jnp.sum(jnp.where(same, g_row, 0.0), axis=1, keepdims=True)
        decay = jnp.where(be, jnp.exp(jnp.where(be, gc_col - gc_row, 0.0)), 0.0)
        mm = jnp.where(jnp.logical_and(be, jnp.logical_not(eye)), b_col * kk * decay, 0.0)
        qk = qk0 * decay
        t_inv = jnp.where(eye, 1.0, 0.0) - jnp.where(mask(MASK_LEVEL + d * N_LEVELS), mm, 0.0)
        for level in range(1, N_LEVELS):
            mc = jnp.where(mask(MASK_LEVEL + d * N_LEVELS + level), mm, 0.0)
            tb = t_inv.astype(BF16)
            y = jnp.dot(tb, mc.astype(BF16), preferred_element_type=F32)
            t_inv = t_inv - jnp.dot(y.astype(BF16), tb, preferred_element_type=F32)
        egc = jnp.exp(gc_col)
        kb = k * b_col
        rhs = jnp.concatenate([v * b_col, kb * egc], axis=1).astype(BF16)
        uw = jnp.dot(t_inv.astype(BF16), rhs, preferred_element_type=F32)
        dv = v.shape[1]
        outs.append((uw[:, :dv], uw[:, dv:], q * egc, k * jnp.exp(tot_col - gc_col), qk, jnp.exp(tot_col)))
    return outs


def _gdn_stream(x_refs, cw_refs, gates_ref, z_ref, onorm_ref, y_ref, states, scr, unroll):
    q_ref, k_ref, v_ref = x_refs
    mask_ref, qs_ref, ks_ref, vs_ref, o_ref, u_ref, wq_ref, kg_ref, qk_ref, gl_ref = scr
    t_len = q_ref.shape[0]
    n_groups = t_len // GROUP
    cpg = GROUP // CHUNK
    dk = q_ref.shape[1]

    def l2(x):
        return x * lax.rsqrt(jnp.sum(x * x, axis=-1, keepdims=True) + EPS)

    rows_all = pl.ds(0, t_len)
    qs_ref[rows_all, :] = l2(_seq_conv_silu(q_ref[...].astype(F32), cw_refs[0][...])) * dk ** -0.5
    ks_ref[rows_all, :] = l2(_seq_conv_silu(k_ref[...].astype(F32), cw_refs[1][...]))
    vs_ref[rows_all, :] = _seq_conv_silu(v_ref[...].astype(F32), cw_refs[2][...])

    def prep(g, carry):
        r = _rows(g, GROUP)
        outs = _gdn_prep_group(qs_ref[r, :], ks_ref[r, :], vs_ref[r, :], gates_ref[g], mask_ref)
        for d, (u, w, qg, kg, qk, glast) in enumerate(outs):
            u_ref[d, g] = u
            kg_ref[d, g] = kg.astype(BF16)
            qk_ref[d, g] = qk.astype(BF16)
            for c in range(cpg):
                rc = slice(c * CHUNK, (c + 1) * CHUNK)
                wq_ref[d, g * cpg + c] = jnp.concatenate([w[rc], qg[rc]], axis=0).astype(BF16)
                gl_ref[d, g * cpg + c] = jnp.broadcast_to(glast[c * CHUNK:c * CHUNK + 8, :], (8, LANES))
        return carry

    if n_groups == 1:
        prep(0, 0)
    else:
        lax.fori_loop(0, n_groups, prep, 0, unroll=unroll)

    def scan_group(d, g, state):
        u = u_ref[d, g]
        order = range(cpg) if d == 0 else range(cpg - 1, -1, -1)
        v_parts, o_parts = [None] * cpg, [None] * cpg
        for c in order:
            rc = slice(c * CHUNK, (c + 1) * CHUNK)
            ws = jnp.dot(wq_ref[d, g * cpg + c], state.astype(BF16), preferred_element_type=F32)
            v_new = u[rc] - ws[:CHUNK]
            o_parts[c] = ws[CHUNK:]
            upd = lax.dot_general(kg_ref[d, g, rc, :], v_new.astype(BF16), (((0,), (0,)), ((), ())),
                                  preferred_element_type=F32)
            state = state * gl_ref[d, g * cpg + c][0:1, :] + upd
            v_parts[c] = v_new
        v_all = jnp.concatenate(v_parts, axis=0).astype(BF16)
        o = jnp.concatenate(o_parts, axis=0) + jnp.dot(qk_ref[d, g], v_all, preferred_element_type=F32)
        return o, state

    def step(t, carry, accumulate):
        s_f, s_b = carry
        g_b = n_groups - 1 - t
        o_f, s_f = scan_group(0, t, s_f)
        o_b, s_b = scan_group(1, g_b, s_b)
        if accumulate:
            o_ref[_rows(t, GROUP), :] += o_f
            o_ref[_rows(g_b, GROUP), :] += o_b
        else:
            o_ref[_rows(t, GROUP), :] = o_f
            o_ref[_rows(g_b, GROUP), :] = o_b
        return s_f, s_b

    if n_groups == 1:
        o_f, s_f = scan_group(0, 0, states[0])
        o_b, s_b = scan_group(1, 0, states[1])
        o_ref[pl.ds(0, GROUP), :] = o_f + o_b
        states = (s_f, s_b)
    else:
        half = n_groups // 2
        assert n_groups == 2 * half
        states = lax.fori_loop(0, half, lambda t, c: step(t, c, False), states)
        states = lax.fori_loop(half, n_groups, lambda t, c: step(t, c, True), states)

    o = o_ref[rows_all, :]
    o = o * lax.rsqrt(jnp.mean(o * o, axis=-1, keepdims=True) + EPS) * onorm_ref[...]
    z = z_ref[...].astype(F32)
    y_ref[...] = (o * (z * jax.nn.sigmoid(z))).astype(y_ref.dtype)
    return states


def _gdn_kernel(qc, kc, vc, zc, gc, ql, kl, vl, zl, gl, cwq, cwk, cwv, onorm, yc_ref, yl_ref, *scr, unroll):
    @pl.when(jnp.logical_and(pl.program_id(0) == 0, pl.program_id(1) == 0))
    def _():
        _gdn_init_masks(scr[0])

    cws = (cwq, cwk, cwv)
    zero = jnp.zeros((kc.shape[1], vc.shape[1]), F32)
    states = _gdn_stream((qc, kc, vc), cws, gc.at[0, 0], zc, onorm, yc_ref, (zero, zero), scr, unroll)
    _gdn_stream((ql, kl, vl), cws, gl.at[0, 0], zl, onorm, yl_ref, states, scr, unroll)


def gdn_mix(p, col_q, col_z, gates_c, gates_l, conv_w, out_norm, *, unroll=2):
    h, dk, dv = H_B, DK_B, DV_B
    assert dk == LANES and dv == LANES and col_q % LANES == 0 and col_z % LANES == 0
    cq, cz = col_q // LANES, col_z // LANES
    ctx_blk0 = N_LAT // CTX_LEN
    ng_l = SEQ // GROUP

    def slab(t, row0, col0):
        return pl.BlockSpec((t, LANES), lambda bi, hi: (row0 + bi, col0 + hi))

    def stream_specs(t, row0, gates):
        return [slab(t, row0, cq), slab(t, row0, cq + h), slab(t, row0, cq + 2 * h), slab(t, row0, cz),
                pl.BlockSpec((1, 1) + gates.shape[2:], lambda bi, hi: (bi, hi, 0, 0, 0))]

    in_specs = stream_specs(CTX_LEN, ctx_blk0, gates_c) + stream_specs(SEQ, 0, gates_l)
    in_specs += [pl.BlockSpec((SHORT_CONV, LANES), functools.partial(lambda bi, hi, off: (0, off + hi), off=off))
                 for off in (0, h, 2 * h)]
    in_specs.append(pl.BlockSpec((1, dv), lambda bi, hi: (0, 0)))
    scratch = [
        pltpu.VMEM((N_MASKS, GROUP, GROUP), F32),
        pltpu.VMEM((SEQ, dk), F32), pltpu.VMEM((SEQ, dk), F32), pltpu.VMEM((SEQ, dv), F32),
        pltpu.VMEM((SEQ, dv), F32),
        pltpu.VMEM((2, ng_l, GROUP, dv), F32),
        pltpu.VMEM((2, SEQ // CHUNK, 2 * CHUNK, dk), BF16),
        pltpu.VMEM((2, ng_l, GROUP, dk), BF16),
        pltpu.VMEM((2, ng_l, GROUP, GROUP), BF16),
        pltpu.VMEM((2, SEQ // CHUNK, 8, LANES), F32),
    ]
    return pl.pallas_call(
        functools.partial(_gdn_kernel, unroll=unroll),
        grid=(BATCH, h),
        in_specs=in_specs,
        out_specs=[pl.BlockSpec((CTX_LEN, dv), lambda bi, hi: (bi, hi)),
                   pl.BlockSpec((SEQ, dv), lambda bi, hi: (bi, hi))],
        out_shape=[jax.ShapeDtypeStruct((N_CTX, h * dv), BF16), jax.ShapeDtypeStruct((N_LAT, h * dv), BF16)],
        scratch_shapes=scratch,
        compiler_params=_cparams(("arbitrary", "arbitrary")),
        name="gdn_mix",
    )(p, p, p, p, gates_c, p, p, p, p, gates_l, conv_w, conv_w, conv_w, out_norm.reshape(1, dv))


def _tri_masks(n, rev):
    ii = lax.broadcasted_iota(jnp.int32, (n, n), 0)
    jj = lax.broadcasted_iota(jnp.int32, (n, n), 1)
    eye = ii == jj
    if rev:
        be, be_t = jj >= ii, ii >= jj
    else:
        be, be_t = jj <= ii, ii <= jj
    return eye, be, be_t


def _mlstm_chunk(q, k, v, i_row, f_row, carry, c_ref, rev):
    n_vec, m = carry
    n = q.shape[0]
    eye, be, be_t = _tri_masks(n, rev)
    f_col = _col_from_row(f_row, eye)
    i_col = _col_from_row(i_row, eye)
    b_col = jnp.sum(jnp.where(be, f_row, 0.0), axis=1, keepdims=True)
    b_row = jnp.sum(jnp.where(be_t, f_col, 0.0), axis=0, keepdims=True)
    b_tot = jnp.sum(f_row, axis=1, keepdims=True)
    d = jnp.where(be, b_col - b_row + i_row, -jnp.inf)
    inter = b_col + m
    m_t = jnp.maximum(inter, jnp.max(d, axis=1, keepdims=True))
    a_inter = jnp.exp(inter - m_t)
    qb = q.astype(BF16)
    kbf = k.astype(BF16)
    s = lax.dot_general(qb, kbf, (((1,), (1,)), ((), ())), preferred_element_type=F32)
    p = jnp.exp(d - m_t) * s
    c_mat = c_ref[...]
    num = a_inter * jnp.dot(qb, c_mat.astype(BF16), preferred_element_type=F32) + jnp.dot(
        p.astype(BF16), v.astype(BF16), preferred_element_type=F32)
    den = a_inter * jnp.sum(q * n_vec, axis=1, keepdims=True) + jnp.sum(p, axis=1, keepdims=True)
    h = num / jnp.maximum(jnp.abs(den), jnp.exp(-m_t))
    w_end_row = b_tot - b_row + i_row
    w_end_col = b_tot - b_col + i_col
    m_new = jnp.maximum(b_tot + m, jnp.max(w_end_row, axis=1, keepdims=True))
    g_old = jnp.exp(b_tot + m - m_new)
    gs_col = jnp.exp(w_end_col - m_new)
    upd = lax.dot_general(kbf, (gs_col * v).astype(BF16), (((0,), (0,)), ((), ())), preferred_element_type=F32)
    c_ref[...] = g_old * c_mat + upd
    n_new = g_old * n_vec + jnp.sum(gs_col * k, axis=0, keepdims=True)
    return h, (n_new, m_new)


def _mlstm_stream(q_ref, k_ref, v_ref, gates_ref, opre_ref, onorm_ref, y_ref, carries, o_ref, c_refs, unroll):
    t_len = q_ref.shape[0]
    n_chunks = t_len // CHUNK
    half = n_chunks // 2
    assert n_chunks == 2 * half
    scale = q_ref.shape[1] ** -0.5

    def chunk(c, carry, rev):
        r = _rows(c, CHUNK)
        gates = gates_ref[c]
        d = 1 if rev else 0
        return _mlstm_chunk(q_ref[r, :].astype(F32) * scale, k_ref[r, :].astype(F32), v_ref[r, :],
                            gates[d:d + 1, :], gates[2 + d:3 + d, :], carry, c_refs[d], rev)

    def step(t, carry, accumulate):
        cf, cb = carry
        c_b = n_chunks - 1 - t
        o_f, cf = chunk(t, cf, False)
        o_b, cb = chunk(c_b, cb, True)
        if accumulate:
            o_ref[_rows(t, CHUNK), :] += o_f
            o_ref[_rows(c_b, CHUNK), :] += o_b
        else:
            o_ref[_rows(t, CHUNK), :] = o_f
            o_ref[_rows(c_b, CHUNK), :] = o_b
        return cf, cb

    carries = lax.fori_loop(0, half, lambda t, c: step(t, c, False), carries, unroll=unroll)
    carries = lax.fori_loop(half, n_chunks, lambda t, c: step(t, c, True), carries, unroll=unroll)
    o = o_ref[pl.ds(0, t_len), :]
    o = o * lax.rsqrt(jnp.mean(o * o, axis=-1, keepdims=True) + EPS) * onorm_ref[0]
    y_ref[...] = (o * jax.nn.sigmoid(opre_ref[...].astype(F32))).astype(y_ref.dtype)
    return carries


def _mlstm_kernel(qc, kc, vc, oc, gc, ql, kl, vl, ol, gl, onorm, yc_ref, yl_ref, o_ref, cf_ref, cb_ref, *, unroll):
    cf_ref[...] = jnp.zeros(cf_ref.shape, F32)
    cb_ref[...] = jnp.zeros(cb_ref.shape, F32)
    init = (jnp.zeros((1, kc.shape[1]), F32), jnp.zeros((1, 1), F32))
    c_refs = (cf_ref, cb_ref)
    carries = _mlstm_stream(qc, kc, vc, gc.at[0, 0], oc, onorm, yc_ref, (init, init), o_ref, c_refs, unroll)
    _mlstm_stream(ql, kl, vl, gl.at[0, 0], ol, onorm, yl_ref, carries, o_ref, c_refs, unroll)


def mlstm_mix(p, col_q, col_k, col_v, col_o, gates_c, gates_l, out_norm, *, unroll=2):
    h, dk, dv = H_C, DK_C, DV_C
    assert col_q % dk == 0 and col_k % dk == 0 and col_v % dv == 0 and col_o % dv == 0
    ctx_blk0 = N_LAT // CTX_LEN

    def slab(t, row0, width, col):
        c0 = col // width
        return pl.BlockSpec((t, width), lambda bi, hi: (row0 + bi, c0 + hi))

    def stream_specs(t, row0, gates):
        return [slab(t, row0, dk, col_q), slab(t, row0, dk, col_k), slab(t, row0, dv, col_v), slab(t, row0, dv, col_o),
                pl.BlockSpec((1, 1) + gates.shape[2:], lambda bi, hi: (bi, hi, 0, 0, 0))]

    in_specs = stream_specs(CTX_LEN, ctx_blk0, gates_c) + stream_specs(SEQ, 0, gates_l)
    in_specs.append(pl.BlockSpec((1, 1, dv), lambda bi, hi: (hi, 0, 0)))
    return pl.pallas_call(
        functools.partial(_mlstm_kernel, unroll=unroll),
        grid=(BATCH, h),
        in_specs=in_specs,
        out_specs=[pl.BlockSpec((CTX_LEN, dv), lambda bi, hi: (bi, hi)),
                   pl.BlockSpec((SEQ, dv), lambda bi, hi: (bi, hi))],
        out_shape=[jax.ShapeDtypeStruct((N_CTX, h * dv), BF16), jax.ShapeDtypeStruct((N_LAT, h * dv), BF16)],
        scratch_shapes=[pltpu.VMEM((SEQ, dv), F32), pltpu.VMEM((dk, dv), F32), pltpu.VMEM((dk, dv), F32)],
        compiler_params=_cparams(("parallel", "parallel")),
        name="mlstm_mix",
    )(p, p, p, p, gates_c, p, p, p, p, gates_l, out_norm.reshape(h, 1, dv))


def _rms(x, w):
    x = x.astype(F32)
    return x * lax.rsqrt(jnp.mean(x * x, axis=-1, keepdims=True) + EPS) * w


def _axial_rope(n_rows, rot_dim):
    r, col = jnp.meshgrid(jnp.arange(n_rows), jnp.arange(GRID_W), indexing="ij")
    r = r.reshape(-1).astype(F32)
    col = col.reshape(-1).astype(F32)
    n_freq = rot_dim // 4
    inv = ROPE_THETA ** (-jnp.arange(n_freq, dtype=F32) / n_freq)
    ang = jnp.concatenate([r[:, None] * inv, col[:, None] * inv], axis=-1)
    return jnp.cos(ang), jnp.sin(ang)


def _rope(x, cos, sin):
    xr = x.reshape(*x.shape[:-1], -1, 2)
    x0, x1 = xr[..., 0], xr[..., 1]
    cc, ss = cos[:, None, :], sin[:, None, :]
    return jnp.stack([x0 * cc - x1 * ss, x0 * ss + x1 * cc], axis=-1).reshape(x.shape)


def _gate_rows(g, width):
    b, t, h = g[0].shape
    rows = [a.transpose(0, 2, 1).reshape(b, h, t // width, 1, width) for a in g]
    rows += [jnp.zeros_like(rows[0])] * (8 - len(rows))
    return jnp.concatenate(rows, axis=3)


def _streams(a):
    c = a.shape[-1]
    return a[N_LAT:].reshape(BATCH, CTX_LEN, c), a[:N_LAT].reshape(BATCH, SEQ, c)


EV_CQ, EV_CKV, EV_QKV, EV_Z = 0, 1024, 1536, 7680
P_EVEN_MAIN = 9728
OD_MQ, OD_MK, OD_MV, OD_MO, OD_GQ, OD_GK, OD_GV = 0, 1024, 2048, 4096, 6144, 8192, 8704
P_ODD_MAIN = 9216


def _split_even(w_in):
    cq, ckv, kpe, qkv, z, ab = jnp.split(w_in, np.cumsum([1024, 512, 64, 6144, 2048]).tolist(), axis=1)
    return jnp.concatenate([cq, ckv, qkv, z], axis=1), jnp.concatenate([kpe, ab], axis=1)


def _split_odd(w_in):
    mq, mk, mv, mo, mg, gq, gk, gv = jnp.split(
        w_in, np.cumsum([1024, 1024, 2048, 2048, 16, 2048, 512]).tolist(), axis=1)
    pad = jnp.zeros((w_in.shape[0], LANES - mg.shape[1]), w_in.dtype)
    return jnp.concatenate([mq, mk, mv, mo, gq, gk, gv], axis=1), jnp.concatenate([mg, pad], axis=1)


def _even_mixers(p, pg, need_ctx, rope, w_qb, q_lora_norm, w_kvb, kv_lora_norm, q_head_norm, k_head_norm,
                 gdn_conv, a_log, dt_bias, out_norm):
    m = p.shape[0]
    qn = rms_rows(p, EV_CQ // Q_LORA, Q_LORA, q_lora_norm)
    kvn = rms_rows(p, EV_CKV // KV_LORA, KV_LORA, kv_lora_norm)
    q = matmul(qn, w_qb, tn=1024, out_dtype=F32)
    kv = matmul(kvn, w_kvb, tn=1024, out_dtype=F32)
    q = q.reshape(m, H_A, NOPE_A + ROPE_A)
    kv = kv.reshape(m, H_A, NOPE_A + V_A)
    q_nope = _rms(q[..., :NOPE_A], q_head_norm[:NOPE_A])
    q_pe = _rms(q[..., NOPE_A:], q_head_norm[NOPE_A:])
    k_nope = _rms(kv[..., :NOPE_A], k_head_norm[:NOPE_A])
    v = kv[..., NOPE_A:]
    k_pe = _rms(pg[:, :ROPE_A], k_head_norm[NOPE_A:])[:, None, :]
    cos, sin = rope

    def stream(a, lo, hi, t, roped):
        a = a[lo:hi].reshape(BATCH, t, a.shape[1], a.shape[2])
        return _rope(a, cos, sin) if roped else a

    def assemble(lo, hi, t, roped):
        qh = jnp.concatenate([stream(q_nope, lo, hi, t, False), stream(q_pe, lo, hi, t, roped)], axis=-1)
        kp = jnp.broadcast_to(stream(k_pe, lo, hi, t, roped), (BATCH, t, H_A, ROPE_A))
        kh = jnp.concatenate([stream(k_nope, lo, hi, t, False), kp], axis=-1)
        vh = stream(v, lo, hi, t, False)
        tr = lambda a: a.transpose(0, 2, 1, 3).astype(BF16)
        return tr(qh), tr(kh), tr(vh)

    q_l, k_l, v_l = assemble(0, N_LAT, SEQ, True)
    q_c, k_c, v_c = assemble(N_LAT, m, CTX_LEN, False)
    a_lat = attention(q_l, jnp.concatenate([k_c, k_l], axis=2), jnp.concatenate([v_c, v_l], axis=2))
    def gdn_gates(ab):
        ab = ab.reshape(ab.shape[0], ab.shape[1], 4, H_B)
        g = -jnp.exp(a_log)[None, None] * jax.nn.softplus(ab[:, :, :2] + dt_bias[None, None])
        beta = jax.nn.sigmoid(ab[:, :, 2:])
        return _gate_rows([g[:, :, 0], g[:, :, 1], beta[:, :, 0], beta[:, :, 1]], GROUP)

    ab_c, ab_l = _streams(pg[:, ROPE_A:ROPE_A + 4 * H_B])
    b_ctx, b_lat = gdn_mix(p, EV_QKV, EV_Z, gdn_gates(ab_c), gdn_gates(ab_l), gdn_conv, out_norm)
    a_lat = a_lat.reshape(N_LAT, -1)
    if not need_ctx:
        return [a_lat, b_lat]
    a_ctx = attention(q_c, k_c, v_c).reshape(N_CTX, -1)
    return [jnp.concatenate([a_lat, a_ctx], axis=0), jnp.concatenate([b_lat, b_ctx], axis=0)]


def _odd_mixers(p, pg, need_ctx, rope, gate_bias, out_norm, q_norm, k_norm):
    m = p.shape[0]
    def mlstm_gates(g):
        g = g.reshape(g.shape[0], g.shape[1], 4, H_C) + gate_bias
        g = GATE_CAP * jnp.tanh(g / GATE_CAP)
        return _gate_rows([g[:, :, 0], g[:, :, 1], jax.nn.log_sigmoid(g[:, :, 2]), jax.nn.log_sigmoid(g[:, :, 3])],
                          CHUNK)

    mg_c, mg_l = _streams(pg[:, :4 * H_C])
    c_ctx, c_lat = mlstm_mix(p, OD_MQ, OD_MK, OD_MV, OD_MO, mlstm_gates(mg_c), mlstm_gates(mg_l), out_norm)
    cos, sin = rope

    def gqa(lo, hi, t, roped):
        sl = lambda off, h: p[lo:hi, off:off + h * HD_D].reshape(BATCH, t, h, HD_D)
        qd = _rms(sl(OD_GQ, H_D), q_norm)
        kd = _rms(sl(OD_GK, KV_D), k_norm)
        vd = sl(OD_GV, KV_D)
        if roped:
            qd, kd = _rope(qd, cos, sin), _rope(kd, cos, sin)
        tr = lambda a: a.transpose(0, 2, 1, 3).astype(BF16)
        return tr(qd), tr(kd), tr(vd)

    q_l, k_l, v_l = gqa(0, N_LAT, SEQ, True)
    q_c, k_c, v_c = gqa(N_LAT, m, CTX_LEN, False)
    d_lat = attention(q_l, jnp.concatenate([k_c, k_l], axis=2), jnp.concatenate([v_c, v_l], axis=2))
    d_lat = d_lat.reshape(N_LAT, -1)
    if not need_ctx:
        return [c_lat, d_lat]
    d_ctx = attention(q_c, k_c, v_c).reshape(N_CTX, -1)
    return [jnp.concatenate([c_lat, c_ctx], axis=0), jnp.concatenate([d_lat, d_ctx], axis=0)]


def kernel(x, c, ctx, c_ctx, mod_down, mod_up, mod_bias, norm_mix, norm_ffn, ffn_up, ffn_conv, ffn_down,
           ev_w_in, ev_w_out, mla_w_qb, mla_q_lora_norm, mla_w_kvb, mla_kv_lora_norm, mla_q_head_norm,
           mla_k_head_norm, gdn_conv, gdn_a_log, gdn_dt_bias, gdn_out_norm, od_w_in, od_w_out,
           mlstm_gate_bias, mlstm_out_norm, gqa_q_norm, gqa_k_norm):
    n_rows = SEQ // GRID_W
    rope_a = _axial_rope(n_rows, ROPE_A)
    rope_d = _axial_rope(n_rows, HD_D)
    cond = jnp.concatenate([c, c_ctx[None, :], jnp.zeros((MOD_ROWS - BATCH - 1, D_MODEL), F32)], axis=0)
    mods = adaln_all(cond, mod_down, mod_up, mod_bias)
    xs = jnp.concatenate([x.reshape(N_LAT, D_MODEL), ctx.reshape(N_CTX, D_MODEL)], axis=0)
    for layer in range(DEPTH):
        need_ctx = layer < DEPTH - 1
        j = layer // 2
        mod = mods[layer].reshape(MOD_ROWS, 1, 6 * D_MODEL)
        u = norm_mod(xs, norm_mix[layer], mod, 0)
        w_main, w_gate = _split_even(ev_w_in[j]) if layer % 2 == 0 else _split_odd(od_w_in[j])
        p = matmul(u, w_main.astype(BF16), tn=1024, out_dtype=BF16)
        pg = matmul(u, w_gate.astype(BF16), tn=LANES, out_dtype=F32)
        if layer % 2 == 0:
            y = _even_mixers(p, pg, need_ctx, rope_a, mla_w_qb[j].astype(BF16), mla_q_lora_norm[j],
                             mla_w_kvb[j].astype(BF16), mla_kv_lora_norm[j], mla_q_head_norm[j],
                             mla_k_head_norm[j], gdn_conv[j], gdn_a_log[j], gdn_dt_bias[j], gdn_out_norm[j])
            w_out = ev_w_out[j]
        else:
            y = _odd_mixers(p, pg, need_ctx, rope_d, mlstm_gate_bias[j], mlstm_out_norm[j], gqa_q_norm[j],
                            gqa_k_norm[j])
            w_out = od_w_out[j]
        if not need_ctx:
            xs = xs[:N_LAT]
        xs = matmul(y, w_out.astype(BF16), tn=1024, out_dtype=F32, res=xs, mod=mod, gate_chunk=2)
        u2 = norm_mod(xs, norm_ffn[layer], mod, 3)
        hmid = ffn_up_proj(u2, ffn_up[layer].astype(BF16), ffn_conv[layer])
        xs = matmul(hmid, ffn_down[layer].astype(BF16), tn=256, out_dtype=F32, res=xs, mod=mod, gate_chunk=5,
                    single_buffer_a=True)
    return xs.reshape(BATCH, SEQ, D_MODEL)
```

```python
import functools
import math
from typing import NamedTuple, Optional

import numpy as np
import jax
import jax.numpy as jnp
from jax import lax
from jax.experimental import pallas as pl
from jax.experimental.pallas import tpu as pltpu

F32 = jnp.float32
BF16 = jnp.bfloat16

D_MODEL = 4096
BATCH = 4
SEQ = 4096
DEPTH = 4
CTX_LEN = 256
GRID_W = 64
MOD_RANK = 256
H_A, Q_LORA, KV_LORA, NOPE_A, ROPE_A, V_A = 16, 1024, 512, 128, 64, 128
H_B, DK_B, DV_B, SHORT_CONV = 16, 128, 128, 3
H_C, DK_C, DV_C, GATE_CAP = 4, 256, 512, 15.0
H_D, KV_D, HD_D = 16, 4, 128
D_FF, FFN_CONV = 11008, 3
CHUNK = 64
ROPE_THETA = 10000.0
EPS = 1e-6

N_LAT = BATCH * SEQ
N_CTX = BATCH * CTX_LEN
MOD_ROWS = 8
LANES = 128

VMEM_LIMIT_V7X = 56 * 1024 * 1024

TM = 1024
TM_NORM = 256
TM_RMS = 512
TQ = 512
Q_SUB = 256
GROUP = 256


def _cparams(sem):
    return pltpu.CompilerParams(dimension_semantics=sem, vmem_limit_bytes=VMEM_LIMIT_V7X)


def _mod_row(i, tm):
    return jnp.minimum((i * tm) // SEQ, BATCH)


def _adaln_kernel(cond_ref, down_ref, up_ref, bias_ref, o_ref):
    cond = cond_ref[...]
    a = cond * jax.nn.sigmoid(cond)
    t = jnp.dot(a, down_ref[0], preferred_element_type=F32, precision=lax.Precision.HIGHEST)
    o = jnp.dot(t, up_ref[0], preferred_element_type=F32, precision=lax.Precision.HIGHEST)
    o_ref[0] = o + bias_ref[0]


def adaln_all(cond, mod_down, mod_up, mod_bias):
    depth, d, rank = mod_down.shape
    n = mod_up.shape[-1]
    tn = d
    return pl.pallas_call(
        _adaln_kernel,
        grid=(depth, n // tn),
        in_specs=[
            pl.BlockSpec((MOD_ROWS, d), lambda l, j: (0, 0)),
            pl.BlockSpec((1, d, rank), lambda l, j: (l, 0, 0)),
            pl.BlockSpec((1, rank, tn), lambda l, j: (l, 0, j)),
            pl.BlockSpec((1, 1, tn), lambda l, j: (l, 0, j)),
        ],
        out_specs=pl.BlockSpec((1, MOD_ROWS, tn), lambda l, j: (l, 0, j)),
        out_shape=jax.ShapeDtypeStruct((depth, MOD_ROWS, n), F32),
        compiler_params=_cparams(("arbitrary", "arbitrary")),
        name="adaln",
    )(cond, mod_down, mod_up, mod_bias.reshape(depth, 1, n))


def _norm_mod_kernel(x_ref, w_ref, sh_ref, sc_ref, o_ref):
    x = x_ref[...]
    y = x * lax.rsqrt(jnp.mean(x * x, axis=-1, keepdims=True) + EPS)
    y = y * w_ref[...]
    o_ref[...] = (y * (1.0 + sc_ref[0]) + sh_ref[0]).astype(o_ref.dtype)


def norm_mod(xs, w, mod, shift_chunk):
    m, d = xs.shape
    tm = TM_NORM
    return pl.pallas_call(
        _norm_mod_kernel,
        grid=(m // tm,),
        in_specs=[
            pl.BlockSpec((tm, d), lambda i: (i, 0)),
            pl.BlockSpec((1, d), lambda i: (0, 0)),
            pl.BlockSpec((1, 1, d), lambda i: (_mod_row(i, tm), 0, shift_chunk)),
            pl.BlockSpec((1, 1, d), lambda i: (_mod_row(i, tm), 0, shift_chunk + 1)),
        ],
        out_specs=pl.BlockSpec((tm, d), lambda i: (i, 0)),
        out_shape=jax.ShapeDtypeStruct((m, d), BF16),
        compiler_params=_cparams(("parallel",)),
        name="norm_mod",
    )(xs, w.reshape(1, d), mod, mod)


def _rms_rows_kernel(x_ref, w_ref, o_ref):
    x = x_ref[...].astype(F32)
    y = x * lax.rsqrt(jnp.mean(x * x, axis=-1, keepdims=True) + EPS)
    o_ref[...] = (y * w_ref[...]).astype(o_ref.dtype)


def rms_rows(p, col_block, width, w):
    m = p.shape[0]
    tm = TM_RMS
    return pl.pallas_call(
        _rms_rows_kernel,
        grid=(m // tm,),
        in_specs=[
            pl.BlockSpec((tm, width), lambda i: (i, col_block)),
            pl.BlockSpec((1, width), lambda i: (0, 0)),
        ],
        out_specs=pl.BlockSpec((tm, width), lambda i: (i, 0)),
        out_shape=jax.ShapeDtypeStruct((m, width), BF16),
        compiler_params=_cparams(("parallel",)),
        name="rms_rows",
    )(p, w.reshape(1, width))


def _mm_kernel(*refs, n_parts, has_res):
    a_refs, w_refs, rest = refs[:n_parts], refs[n_parts:2 * n_parts], refs[2 * n_parts:]
    acc = jnp.dot(a_refs[0][...], w_refs[0][...], preferred_element_type=F32)
    for a_ref, w_ref in zip(a_refs[1:], w_refs[1:]):
        acc += jnp.dot(a_ref[...], w_ref[...], preferred_element_type=F32)
    if has_res:
        res_ref, gate_ref, o_ref = rest
        acc = res_ref[...] + gate_ref[0] * acc
    else:
        (o_ref,) = rest
    o_ref[...] = acc.astype(o_ref.dtype)


def matmul(a_parts, w, *, tn, out_dtype, res=None, mod=None, gate_chunk=None, single_buffer_a=False):
    if not isinstance(a_parts, (list, tuple)):
        a_parts = [a_parts]
    n_parts = len(a_parts)
    m, kp = a_parts[0].shape
    n = w.shape[1]
    tm, tn = TM, min(tn, n)
    assert m % tm == 0 and w.shape[0] == kp * n_parts and all(a.shape == (m, kp) for a in a_parts)
    has_res = res is not None
    a_mode = dict(pipeline_mode=pl.Buffered(1)) if single_buffer_a else {}
    in_specs = [pl.BlockSpec((tm, kp), lambda i, j: (i, 0), **a_mode) for _ in a_parts]
    in_specs += [pl.BlockSpec((kp, tn), functools.partial(lambda i, j, part: (part, j), part=part))
                 for part in range(n_parts)]
    args = list(a_parts) + [w] * n_parts
    if has_res:
        assert n % tn == 0 and mod.shape[-1] % tn == 0
        gate_off = gate_chunk * (n // tn)
        in_specs += [
            pl.BlockSpec((tm, tn), lambda i, j: (i, j)),
            pl.BlockSpec((1, 1, tn), lambda i, j: (_mod_row(i, tm), 0, gate_off + j)),
        ]
        args += [res, mod]
    return pl.pallas_call(
        functools.partial(_mm_kernel, n_parts=n_parts, has_res=has_res),
        grid=(m // tm, pl.cdiv(n, tn)),
        in_specs=in_specs,
        out_specs=pl.BlockSpec((tm, tn), lambda i, j: (i, j)),
        out_shape=jax.ShapeDtypeStruct((m, n), out_dtype),
        compiler_params=_cparams(("parallel", "arbitrary")),
        name="matmul",
    )(*args)


HALO = 16
FFN_SUB = 1024


def _seq_pos(i, tm):
    r = i * tm + lax.broadcasted_iota(jnp.int32, (tm, 1), 0)
    pos = jnp.where(r < N_LAT, r & (SEQ - 1), (r - N_LAT) & (CTX_LEN - 1))
    last = jnp.where(r < N_LAT, SEQ - 1, CTX_LEN - 1)
    return pos == 0, pos == last


def _conv3_rows(h, h_prev_row, h_next_row, cw, first, last):
    tm = h.shape[0]
    t = lax.broadcasted_iota(jnp.int32, (tm, 1), 0)
    dn = jnp.where(t == 0, h_prev_row, pltpu.roll(h, 1, 0))
    dn = jnp.where(first, 0.0, dn)
    up = jnp.where(t == tm - 1, h_next_row, pltpu.roll(h, tm - 1, 0))
    up = jnp.where(last, 0.0, up)
    return cw[0:1, :] * dn + cw[1:2, :] * h + cw[2:3, :] * up


def _ffn_up_kernel(a_ref, wg_ref, wu_ref, cg_ref, cu_ref, o_ref, *, tm):
    i = pl.program_id(0)
    first, last = _seq_pos(i, tm)
    sub = min(tm, FFN_SUB)
    ext = sub + 2 * HALO
    mid = slice(HALO, HALO + sub)
    for r0 in range(0, tm, sub):
        a = a_ref[r0:r0 + ext, :]
        f, l = first[r0:r0 + sub], last[r0:r0 + sub]

        def branch(w_ref, c_ref):
            h = jnp.dot(a, w_ref[...], preferred_element_type=F32)
            cw = c_ref[...]
            dn = jnp.where(f, 0.0, pltpu.roll(h, 1, 0)[mid])
            up = jnp.where(l, 0.0, pltpu.roll(h, ext - 1, 0)[mid])
            return cw[0:1, :] * dn + cw[1:2, :] * h[mid] + cw[2:3, :] * up

        g = branch(wg_ref, cg_ref)
        u = branch(wu_ref, cu_ref)
        o_ref[r0:r0 + sub, :] = (g * jax.nn.sigmoid(g) * u).astype(o_ref.dtype)


def ffn_up_proj(u2, w_up, conv_w, *, tn=256):
    m, d = u2.shape
    tm = TM
    dff = w_up.shape[1] // 2
    nj = dff // tn
    assert dff % tn == 0 and m % tm == 0
    pad = jnp.zeros((HALO, d), u2.dtype)
    u2p = jnp.concatenate([pad, u2, pad], axis=0)
    return pl.pallas_call(
        functools.partial(_ffn_up_kernel, tm=tm),
        grid=(m // tm, nj),
        in_specs=[
            pl.BlockSpec((pl.Element(tm + 2 * HALO), pl.Element(d)), lambda i, j: (i * tm, 0)),
            pl.BlockSpec((d, tn), lambda i, j: (0, j)),
            pl.BlockSpec((d, tn), lambda i, j: (0, j + nj)),
            pl.BlockSpec((FFN_CONV, tn), lambda i, j: (0, j)),
            pl.BlockSpec((FFN_CONV, tn), lambda i, j: (0, j + nj)),
        ],
        out_specs=pl.BlockSpec((tm, tn), lambda i, j: (i, j)),
        out_shape=jax.ShapeDtypeStruct((m, dff), BF16),
        compiler_params=_cparams(("parallel", "arbitrary")),
        name="ffn_up",
    )(u2p, w_up, w_up, conv_w, conv_w)


class QKPart(NamedTuple):
    q_arr: jax.Array
    q_col: int
    k_arr: jax.Array
    k_col: int
    k_per_head: bool
    wq: jax.Array
    wk: jax.Array
    n_valid: int
    tabs: Optional[tuple]


def _swap_halves(y, n_valid):
    half = n_valid // 2
    if n_valid == LANES:
        return pltpu.roll(y, half, 1)
    lane = lax.broadcasted_iota(jnp.int32, (1, LANES), 1)
    return jnp.where(lane < half, pltpu.roll(y, LANES - half, 1), pltpu.roll(y, half, 1))


def _norm_rope(x, w, n_valid, tabs):
    if n_valid < LANES:
        lane = lax.broadcasted_iota(jnp.int32, (1, LANES), 1)
        x = jnp.where(lane < n_valid, x, 0.0)
    y = x * lax.rsqrt(jnp.sum(x * x, axis=-1, keepdims=True) * (1.0 / n_valid) + EPS) * w
    if tabs is not None:
        y = y * tabs[0] + _swap_halves(y, n_valid) * tabs[1]
    return y


def _attn_kernel(*refs, n_parts, n_valid, roped, segs, tq, scale):
    n_seg = len(segs)
    it = iter(refs)
    q_refs = [next(it) for _ in range(n_parts)]
    k_refs = [[next(it) for _ in range(n_parts)] for _ in range(n_seg)]
    v_refs = [next(it) for _ in range(n_seg)]
    wq_refs = [next(it) for _ in range(n_parts)]
    wk_refs = [next(it) for _ in range(n_parts)]
    tab_refs = [(next(it), next(it)) if roped[p] else None for p in range(n_parts)]
    o_ref = next(it)
    k_scr = [next(it) for _ in range(n_seg)]
    v_scr = [next(it) for _ in range(n_seg)]
    t = pl.program_id(3)

    @pl.when(jnp.logical_and(pl.program_id(2) == 0, t == 0))
    def _():
        for si, seg in enumerate(segs):
            for p in range(n_parts):
                tabs = None
                if seg == "lat" and roped[p]:
                    tabs = (tab_refs[p][0][...], tab_refs[p][1][...])
                y = _norm_rope(k_refs[si][p][...].astype(F32), wk_refs[p][...], n_valid[p], tabs)
                k_scr[si][:, p * LANES:(p + 1) * LANES] = y.astype(BF16)
            rows = v_scr[si].shape[0]
            lane = lax.broadcasted_iota(jnp.int32, (rows, LANES), 1)
            v_scr[si][:, :LANES] = v_refs[si][...]
            v_scr[si][:, LANES:] = jnp.where(lane == 0, 1.0, 0.0).astype(BF16)

    parts = []
    for p in range(n_parts):
        tabs = None
        if roped[p]:
            r = pl.ds(pl.multiple_of(t * tq, tq), tq)
            tabs = (tab_refs[p][0][r, :], tab_refs[p][1][r, :])
        parts.append(_norm_rope(q_refs[p][...].astype(F32), wq_refs[p][...], n_valid[p], tabs) * scale)
    q = jnp.concatenate(parts, axis=1).astype(BF16)
    dn_t = (((1,), (1,)), ((), ()))
    sub = min(tq, Q_SUB)
    for r0 in range(0, tq, sub):
        qs = q[r0:r0 + sub]
        ss = [lax.dot_general(qs, k_scr[si][...], dn_t, preferred_element_type=F32) for si in range(n_seg)]
        m = jnp.max(ss[0], axis=-1, keepdims=True)
        for s in ss[1:]:
            m = jnp.maximum(m, jnp.max(s, axis=-1, keepdims=True))
        o_ext = None
        for si, s in enumerate(ss):
            pv = jnp.dot(jnp.exp((s - m).astype(BF16)), v_scr[si][...], preferred_element_type=F32)
            o_ext = pv if o_ext is None else o_ext + pv
        o_ref[r0:r0 + sub, :] = (o_ext[:, :LANES] / o_ext[:, LANES:LANES + 1]).astype(o_ref.dtype)


def attention_mix(parts, v_arr, v_col, n_heads, n_kv, dk, q_stream):
    g = n_heads // n_kv
    ctx_blk0 = N_LAT // CTX_LEN
    lat = q_stream == "lat"
    tq = min(TQ, SEQ) if lat else CTX_LEN
    nt = SEQ // tq if lat else 1
    segs = ("ctx", "lat") if lat else ("ctx",)
    roped = tuple(lat and part.tabs is not None for part in parts)
    seg_rows = {"ctx": CTX_LEN, "lat": SEQ}
    seg_blk0 = {"ctx": ctx_blk0, "lat": 0}
    q_blk0 = 0 if lat else ctx_blk0

    def slab(rows, blk0, col, per_head):
        return pl.BlockSpec((rows, LANES), lambda b, hk, gi, t: (blk0 + b, col + (hk if per_head else 0)))

    in_specs, args = [], []
    for part in parts:
        in_specs.append(pl.BlockSpec(
            (tq, LANES), functools.partial(lambda b, hk, gi, t, col: (q_blk0 + b * nt + t, col + hk * g + gi),
                                           col=part.q_col)))
        args.append(part.q_arr)
    for seg in segs:
        for part in parts:
            in_specs.append(slab(seg_rows[seg], seg_blk0[seg], part.k_col, part.k_per_head))
            args.append(part.k_arr)
    for seg in segs:
        in_specs.append(slab(seg_rows[seg], seg_blk0[seg], v_col, True))
        args.append(v_arr)
    const = lambda b, hk, gi, t: (0, 0)
    for w in [part.wq for part in parts] + [part.wk for part in parts]:
        in_specs.append(pl.BlockSpec((1, LANES), const))
        args.append(w.reshape(1, LANES))
    for part, r in zip(parts, roped):
        if r:
            in_specs += [pl.BlockSpec((SEQ, LANES), const)] * 2
            args += list(part.tabs)
    n_parts = len(parts)
    scratch = [pltpu.VMEM((seg_rows[seg], n_parts * LANES), BF16) for seg in segs]
    scratch += [pltpu.VMEM((seg_rows[seg], 2 * LANES), BF16) for seg in segs]
    q_rows = N_LAT if lat else N_CTX
    return pl.pallas_call(
        functools.partial(_attn_kernel, n_parts=n_parts, n_valid=tuple(part.n_valid for part in parts), roped=roped,
                          segs=segs, tq=tq, scale=dk ** -0.5),
        grid=(BATCH, n_kv, g, nt),
        in_specs=in_specs,
        out_specs=pl.BlockSpec((tq, LANES), lambda b, hk, gi, t: (b * nt + t, hk * g + gi)),
        out_shape=jax.ShapeDtypeStruct((q_rows, n_heads * LANES), BF16),
        scratch_shapes=scratch,
        compiler_params=_cparams(("arbitrary",) * 4),
        name="attention",
    )(*args)


def _col_from_row(row, eye):
    return jnp.sum(jnp.where(eye, row, 0.0), axis=1, keepdims=True)


def _rows(idx, size):
    return pl.ds(pl.multiple_of(idx * size, size), size)


def _seq_conv_silu(x, cw):
    zero = jnp.zeros((1, x.shape[1]), F32)
    t = lax.broadcasted_iota(jnp.int32, (x.shape[0], 1), 0)
    c = _conv3_rows(x, zero, zero, cw, t == 0, t == x.shape[0] - 1)
    return c * jax.nn.sigmoid(c)


N_LEVELS = int(math.log2(CHUNK))
MASK_SAME, MASK_EYE, MASK_BE = 0, 1, 2
MASK_LEVEL = 4
N_MASKS = MASK_LEVEL + 2 * N_LEVELS


def _gdn_init_masks(mask_ref):
    n = GROUP
    ii = lax.broadcasted_iota(jnp.int32, (n, n), 0)
    jj = lax.broadcasted_iota(jnp.int32, (n, n), 1)
    same = (ii // CHUNK) == (jj // CHUNK)

    def put(idx, m):
        mask_ref[idx] = jnp.where(m, 1.0, 0.0)

    put(MASK_SAME, same)
    put(MASK_EYE, ii == jj)
    put(MASK_BE, jnp.logical_and(same, jj <= ii))
    put(MASK_BE + 1, jnp.logical_and(same, jj >= ii))
    for level in range(N_LEVELS):
        s = 1 << level
        blk = (ii // (2 * s)) == (jj // (2 * s))
        ih = (ii // s) & 1
        jh = (jj // s) & 1
        put(MASK_LEVEL + level, jnp.logical_and(blk, jnp.logical_and(ih == 1, jh == 0)))
        put(MASK_LEVEL + N_LEVELS + level, jnp.logical_and(blk, jnp.logical_and(ih == 0, jh == 1)))


def _gdn_prep_group(q, k, v, gates, mask_ref):
    def mask(idx):
        return mask_ref[idx] > 0.5

    eye = mask(MASK_EYE)
    same = mask(MASK_SAME)
    k16 = k.astype(BF16)
    dn_t = (((1,), (1,)), ((), ()))
    kk = lax.dot_general(k16, k16, dn_t, preferred_element_type=F32)
    qk0 = lax.dot_general(q.astype(BF16), k16, dn_t, preferred_element_type=F32)
    outs = []
    for d in range(2):
        be = mask(MASK_BE + d)
        be_t = mask(MASK_BE + 1 - d)
        g_row, b_row = gates[d:d + 1, :], gates[2 + d:3 + d, :]
        g_col = _col_from_row(g_row, eye)
        b_col = _col_from_row(b_row, eye)
        gc_col = jnp.sum(jnp.where(be, g_row, 0.0), axis=1, keepdims=True)
        gc_row = jnp.sum(jnp.where(be_t, g_col, 0.0), axis=0, keepdims=True)
        tot_col = jnp.sum(jnp.where(same, g_row, 0.0), axis=1, keepdims=True)
        decay = jnp.where(be, jnp.exp(jnp.where(be, gc_col - gc_row, 0.0)), 0.0)
        mm = jnp.where(jnp.logical_and(be, jnp.logical_not(eye)), b_col * kk * decay, 0.0)
        qk = qk0 * decay
        t_inv = jnp.where(eye, 1.0, 0.0) - jnp.where(mask(MASK_LEVEL + d * N_LEVELS), mm, 0.0)
        for level in range(1, N_LEVELS):
            mc = jnp.where(mask(MASK_LEVEL + d * N_LEVELS + level), mm, 0.0)
            tb = t_inv.astype(BF16)
            y = jnp.dot(tb, mc.astype(BF16), preferred_element_type=F32)
            t_inv = t_inv - jnp.dot(y.astype(BF16), tb, preferred_element_type=F32)
        egc = jnp.exp(gc_col)
        kb = k * b_col
        rhs = jnp.concatenate([v * b_col, kb * egc], axis=1).astype(BF16)
        uw = jnp.dot(t_inv.astype(BF16), rhs, preferred_element_type=F32)
        dv = v.shape[1]
        outs.append((uw[:, :dv], uw[:, dv:], q * egc, k * jnp.exp(tot_col - gc_col), qk, jnp.exp(tot_col)))
    return outs


def _gdn_stream(x_refs, cw_refs, gates_ref, z_ref, onorm_ref, y_ref, states, scr, unroll):
    q_ref, k_ref, v_ref = x_refs
    mask_ref, qs_ref, ks_ref, vs_ref, o_ref, u_ref, wq_ref, kg_ref, qk_ref, gl_ref = scr
    t_len = q_ref.shape[0]
    n_groups = t_len // GROUP
    cpg = GROUP // CHUNK
    dk = q_ref.shape[1]

    def l2(x):
        return x * lax.rsqrt(jnp.sum(x * x, axis=-1, keepdims=True) + EPS)

    rows_all = pl.ds(0, t_len)
    qs_ref[rows_all, :] = l2(_seq_conv_silu(q_ref[...].astype(F32), cw_refs[0][...])) * dk ** -0.5
    ks_ref[rows_all, :] = l2(_seq_conv_silu(k_ref[...].astype(F32), cw_refs[1][...]))
    vs_ref[rows_all, :] = _seq_conv_silu(v_ref[...].astype(F32), cw_refs[2][...])

    def prep(g, carry):
        r = _rows(g, GROUP)
        outs = _gdn_prep_group(qs_ref[r, :], ks_ref[r, :], vs_ref[r, :], gates_ref[g], mask_ref)
        for d, (u, w, qg, kg, qk, glast) in enumerate(outs):
            u_ref[d, g] = u
            kg_ref[d, g] = kg.astype(BF16)
            qk_ref[d, g] = qk.astype(BF16)
            for c in range(cpg):
                rc = slice(c * CHUNK, (c + 1) * CHUNK)
                wq_ref[d, g * cpg + c] = jnp.concatenate([w[rc], qg[rc]], axis=0).astype(BF16)
                gl_ref[d, g * cpg + c] = jnp.broadcast_to(glast[c * CHUNK:c * CHUNK + 8, :], (8, LANES))
        return carry

    if n_groups == 1:
        prep(0, 0)
    else:
        lax.fori_loop(0, n_groups, prep, 0, unroll=unroll)

    def scan_group(d, g, state):
        u = u_ref[d, g]
        order = range(cpg) if d == 0 else range(cpg - 1, -1, -1)
        v_parts, o_parts = [None] * cpg, [None] * cpg
        for c in order:
            rc = slice(c * CHUNK, (c + 1) * CHUNK)
            ws = jnp.dot(wq_ref[d, g * cpg + c], state.astype(BF16), preferred_element_type=F32)
            v_new = u[rc] - ws[:CHUNK]
            o_parts[c] = ws[CHUNK:]
            upd = lax.dot_general(kg_ref[d, g, rc, :], v_new.astype(BF16), (((0,), (0,)), ((), ())),
                                  preferred_element_type=F32)
            state = state * gl_ref[d, g * cpg + c][0:1, :] + upd
            v_parts[c] = v_new
        v_all = jnp.concatenate(v_parts, axis=0).astype(BF16)
        o = jnp.concatenate(o_parts, axis=0) + jnp.dot(qk_ref[d, g], v_all, preferred_element_type=F32)
        return o, state

    def step(t, carry, accumulate):
        s_f, s_b = carry
        g_b = n_groups - 1 - t
        o_f, s_f = scan_group(0, t, s_f)
        o_b, s_b = scan_group(1, g_b, s_b)
        if accumulate:
            o_ref[_rows(t, GROUP), :] += o_f
            o_ref[_rows(g_b, GROUP), :] += o_b
        else:
            o_ref[_rows(t, GROUP), :] = o_f
            o_ref[_rows(g_b, GROUP), :] = o_b
        return s_f, s_b

    if n_groups == 1:
        o_f, s_f = scan_group(0, 0, states[0])
        o_b, s_b = scan_group(1, 0, states[1])
        o_ref[pl.ds(0, GROUP), :] = o_f + o_b
        states = (s_f, s_b)
    else:
        half = n_groups // 2
        assert n_groups == 2 * half
        states = lax.fori_loop(0, half, lambda t, c: step(t, c, False), states)
        states = lax.fori_loop(half, n_groups, lambda t, c: step(t, c, True), states)

    o = o_ref[rows_all, :]
    o = o * lax.rsqrt(jnp.mean(o * o, axis=-1, keepdims=True) + EPS) * onorm_ref[...]
    z = z_ref[...].astype(F32)
    y_ref[...] = (o * (z * jax.nn.sigmoid(z))).astype(y_ref.dtype)
    return states


def _gdn_kernel(qc, kc, vc, zc, gc, ql, kl, vl, zl, gl, cwq, cwk, cwv, onorm, yc_ref, yl_ref, *scr, unroll):
    @pl.when(jnp.logical_and(pl.program_id(0) == 0, pl.program_id(1) == 0))
    def _():
        _gdn_init_masks(scr[0])

    cws = (cwq, cwk, cwv)
    zero = jnp.zeros((kc.shape[1], vc.shape[1]), F32)
    states = _gdn_stream((qc, kc, vc), cws, gc.at[0, 0], zc, onorm, yc_ref, (zero, zero), scr, unroll)
    _gdn_stream((ql, kl, vl), cws, gl.at[0, 0], zl, onorm, yl_ref, states, scr, unroll)


def gdn_mix(p, col_q, col_z, gates_c, gates_l, conv_w, out_norm, *, unroll=2):
    h, dk, dv = H_B, DK_B, DV_B
    assert dk == LANES and dv == LANES and col_q % LANES == 0 and col_z % LANES == 0
    cq, cz = col_q // LANES, col_z // LANES
    ctx_blk0 = N_LAT // CTX_LEN
    ng_l = SEQ // GROUP

    def slab(t, row0, col0):
        return pl.BlockSpec((t, LANES), lambda bi, hi: (row0 + bi, col0 + hi))

    def stream_specs(t, row0, gates):
        return [slab(t, row0, cq), slab(t, row0, cq + h), slab(t, row0, cq + 2 * h), slab(t, row0, cz),
                pl.BlockSpec((1, 1) + gates.shape[2:], lambda bi, hi: (bi, hi, 0, 0, 0))]

    in_specs = stream_specs(CTX_LEN, ctx_blk0, gates_c) + stream_specs(SEQ, 0, gates_l)
    in_specs += [pl.BlockSpec((SHORT_CONV, LANES), functools.partial(lambda bi, hi, off: (0, off + hi), off=off))
                 for off in (0, h, 2 * h)]
    in_specs.append(pl.BlockSpec((1, dv), lambda bi, hi: (0, 0)))
    scratch = [
        pltpu.VMEM((N_MASKS, GROUP, GROUP), F32),
        pltpu.VMEM((SEQ, dk), F32), pltpu.VMEM((SEQ, dk), F32), pltpu.VMEM((SEQ, dv), F32),
        pltpu.VMEM((SEQ, dv), F32),
        pltpu.VMEM((2, ng_l, GROUP, dv), F32),
        pltpu.VMEM((2, SEQ // CHUNK, 2 * CHUNK, dk), BF16),
        pltpu.VMEM((2, ng_l, GROUP, dk), BF16),
        pltpu.VMEM((2, ng_l, GROUP, GROUP), BF16),
        pltpu.VMEM((2, SEQ // CHUNK, 8, LANES), F32),
    ]
    return pl.pallas_call(
        functools.partial(_gdn_kernel, unroll=unroll),
        grid=(BATCH, h),
        in_specs=in_specs,
        out_specs=[pl.BlockSpec((CTX_LEN, dv), lambda bi, hi: (bi, hi)),
                   pl.BlockSpec((SEQ, dv), lambda bi, hi: (bi, hi))],
        out_shape=[jax.ShapeDtypeStruct((N_CTX, h * dv), BF16), jax.ShapeDtypeStruct((N_LAT, h * dv), BF16)],
        scratch_shapes=scratch,
        compiler_params=_cparams(("arbitrary", "arbitrary")),
        name="gdn_mix",
    )(p, p, p, p, gates_c, p, p, p, p, gates_l, conv_w, conv_w, conv_w, out_norm.reshape(1, dv))


def _tri_masks(n, rev):
    ii = lax.broadcasted_iota(jnp.int32, (n, n), 0)
    jj = lax.broadcasted_iota(jnp.int32, (n, n), 1)
    eye = ii == jj
    if rev:
        be, be_t = jj >= ii, ii >= jj
    else:
        be, be_t = jj <= ii, ii <= jj
    return eye, be, be_t


def _mlstm_chunk(q, k, v, i_row, f_row, carry, c_ref, rev):
    n_vec, m = carry
    n = q.shape[0]
    eye, be, be_t = _tri_masks(n, rev)
    f_col = _col_from_row(f_row, eye)
    i_col = _col_from_row(i_row, eye)
    b_col = jnp.sum(jnp.where(be, f_row, 0.0), axis=1, keepdims=True)
    b_row = jnp.sum(jnp.where(be_t, f_col, 0.0), axis=0, keepdims=True)
    b_tot = jnp.sum(f_row, axis=1, keepdims=True)
    d = jnp.where(be, b_col - b_row + i_row, -jnp.inf)
    inter = b_col + m
    m_t = jnp.maximum(inter, jnp.max(d, axis=1, keepdims=True))
    a_inter = jnp.exp(inter - m_t)
    qb = q.astype(BF16)
    kbf = k.astype(BF16)
    s = lax.dot_general(qb, kbf, (((1,), (1,)), ((), ())), preferred_element_type=F32)
    p = jnp.exp(d - m_t) * s
    c_mat = c_ref[...]
    num = a_inter * jnp.dot(qb, c_mat.astype(BF16), preferred_element_type=F32) + jnp.dot(
        p.astype(BF16), v.astype(BF16), preferred_element_type=F32)
    den = a_inter * jnp.sum(q * n_vec, axis=1, keepdims=True) + jnp.sum(p, axis=1, keepdims=True)
    h = num / jnp.maximum(jnp.abs(den), jnp.exp(-m_t))
    w_end_row = b_tot - b_row + i_row
    w_end_col = b_tot - b_col + i_col
    m_new = jnp.maximum(b_tot + m, jnp.max(w_end_row, axis=1, keepdims=True))
    g_old = jnp.exp(b_tot + m - m_new)
    gs_col = jnp.exp(w_end_col - m_new)
    upd = lax.dot_general(kbf, (gs_col * v).astype(BF16), (((0,), (0,)), ((), ())), preferred_element_type=F32)
    c_ref[...] = g_old * c_mat + upd
    n_new = g_old * n_vec + jnp.sum(gs_col * k, axis=0, keepdims=True)
    return h, (n_new, m_new)


def _mlstm_stream(q_ref, k_ref, v_ref, gates_ref, opre_ref, onorm_ref, y_ref, carries, o_ref, c_refs, unroll):
    t_len = q_ref.shape[0]
    n_chunks = t_len // CHUNK
    half = n_chunks // 2
    assert n_chunks == 2 * half
    scale = q_ref.shape[1] ** -0.5

    def chunk(c, carry, rev):
        r = _rows(c, CHUNK)
        gates = gates_ref[c]
        d = 1 if rev else 0
        return _mlstm_chunk(q_ref[r, :].astype(F32) * scale, k_ref[r, :].astype(F32), v_ref[r, :],
                            gates[d:d + 1, :], gates[2 + d:3 + d, :], carry, c_refs[d], rev)

    def step(t, carry, accumulate):
        cf, cb = carry
        c_b = n_chunks - 1 - t
        o_f, cf = chunk(t, cf, False)
        o_b, cb = chunk(c_b, cb, True)
        if accumulate:
            o_ref[_rows(t, CHUNK), :] += o_f
            o_ref[_rows(c_b, CHUNK), :] += o_b
        else:
            o_ref[_rows(t, CHUNK), :] = o_f
            o_ref[_rows(c_b, CHUNK), :] = o_b
        return cf, cb

    carries = lax.fori_loop(0, half, lambda t, c: step(t, c, False), carries, unroll=unroll)
    carries = lax.fori_loop(half, n_chunks, lambda t, c: step(t, c, True), carries, unroll=unroll)
    o = o_ref[pl.ds(0, t_len), :]
    o = o * lax.rsqrt(jnp.mean(o * o, axis=-1, keepdims=True) + EPS) * onorm_ref[0]
    y_ref[...] = (o * jax.nn.sigmoid(opre_ref[...].astype(F32))).astype(y_ref.dtype)
    return carries


def _mlstm_kernel(qc, kc, vc, oc, gc, ql, kl, vl, ol, gl, onorm, yc_ref, yl_ref, o_ref, cf_ref, cb_ref, *, unroll):
    cf_ref[...] = jnp.zeros(cf_ref.shape, F32)
    cb_ref[...] = jnp.zeros(cb_ref.shape, F32)
    init = (jnp.zeros((1, kc.shape[1]), F32), jnp.zeros((1, 1), F32))
    c_refs = (cf_ref, cb_ref)
    carries = _mlstm_stream(qc, kc, vc, gc.at[0, 0], oc, onorm, yc_ref, (init, init), o_ref, c_refs, unroll)
    _mlstm_stream(ql, kl, vl, gl.at[0, 0], ol, onorm, yl_ref, carries, o_ref, c_refs, unroll)


def mlstm_mix(p, col_q, col_k, col_v, col_o, gates_c, gates_l, out_norm, *, unroll=2):
    h, dk, dv = H_C, DK_C, DV_C
    assert col_q % dk == 0 and col_k % dk == 0 and col_v % dv == 0 and col_o % dv == 0
    ctx_blk0 = N_LAT // CTX_LEN

    def slab(t, row0, width, col):
        c0 = col // width
        return pl.BlockSpec((t, width), lambda bi, hi: (row0 + bi, c0 + hi))

    def stream_specs(t, row0, gates):
        return [slab(t, row0, dk, col_q), slab(t, row0, dk, col_k), slab(t, row0, dv, col_v), slab(t, row0, dv, col_o),
                pl.BlockSpec((1, 1) + gates.shape[2:], lambda bi, hi: (bi, hi, 0, 0, 0))]

    in_specs = stream_specs(CTX_LEN, ctx_blk0, gates_c) + stream_specs(SEQ, 0, gates_l)
    in_specs.append(pl.BlockSpec((1, 1, dv), lambda bi, hi: (hi, 0, 0)))
    return pl.pallas_call(
        functools.partial(_mlstm_kernel, unroll=unroll),
        grid=(BATCH, h),
        in_specs=in_specs,
        out_specs=[pl.BlockSpec((CTX_LEN, dv), lambda bi, hi: (bi, hi)),
                   pl.BlockSpec((SEQ, dv), lambda bi, hi: (bi, hi))],
        out_shape=[jax.ShapeDtypeStruct((N_CTX, h * dv), BF16), jax.ShapeDtypeStruct((N_LAT, h * dv), BF16)],
        scratch_shapes=[pltpu.VMEM((SEQ, dv), F32), pltpu.VMEM((dk, dv), F32), pltpu.VMEM((dk, dv), F32)],
        compiler_params=_cparams(("parallel", "parallel")),
        name="mlstm_mix",
    )(p, p, p, p, gates_c, p, p, p, p, gates_l, out_norm.reshape(h, 1, dv))


def _axial_rope(n_rows, rot_dim):
    r, col = jnp.meshgrid(jnp.arange(n_rows), jnp.arange(GRID_W), indexing="ij")
    r = r.reshape(-1).astype(F32)
    col = col.reshape(-1).astype(F32)
    n_freq = rot_dim // 4
    inv = ROPE_THETA ** (-jnp.arange(n_freq, dtype=F32) / n_freq)
    ang = jnp.concatenate([r[:, None] * inv, col[:, None] * inv], axis=-1)
    return jnp.cos(ang), jnp.sin(ang)


def _deinterleave(n):
    return np.concatenate([np.arange(0, n, 2), np.arange(1, n, 2)])


def _pad_lanes(a):
    pad = LANES - a.shape[-1]
    return a if pad == 0 else jnp.concatenate([a, jnp.zeros(a.shape[:-1] + (pad,), a.dtype)], axis=-1)


def _rope_tables(n_rows, rot_dim):
    cos, sin = _axial_rope(n_rows, rot_dim)
    return _pad_lanes(jnp.concatenate([cos, cos], axis=1)), _pad_lanes(jnp.concatenate([-sin, sin], axis=1))


def _gate_rows(g, width):
    b, t, h = g[0].shape
    rows = [a.transpose(0, 2, 1).reshape(b, h, t // width, 1, width) for a in g]
    rows += [jnp.zeros_like(rows[0])] * (8 - len(rows))
    return jnp.concatenate(rows, axis=3)


def _streams(a):
    c = a.shape[-1]
    return a[N_LAT:].reshape(BATCH, CTX_LEN, c), a[:N_LAT].reshape(BATCH, SEQ, c)


EV_CQ, EV_CKV, EV_QKV, EV_Z = 0, 1024, 1536, 7680
P_EVEN_MAIN = 9728
OD_MQ, OD_MK, OD_MV, OD_MO, OD_GQ, OD_GK, OD_GV = 0, 1024, 2048, 4096, 6144, 8192, 8704
P_ODD_MAIN = 9216


def _split_even(w_in):
    cq, ckv, kpe, qkv, z, ab = jnp.split(w_in, np.cumsum([1024, 512, 64, 6144, 2048]).tolist(), axis=1)
    kpe = kpe[:, _deinterleave(ROPE_A)]
    return jnp.concatenate([cq, ckv, qkv, z], axis=1), jnp.concatenate([kpe, ab], axis=1)


def _split_odd(w_in):
    mq, mk, mv, mo, mg, gq, gk, gv = jnp.split(
        w_in, np.cumsum([1024, 1024, 2048, 2048, 16, 2048, 512]).tolist(), axis=1)
    perm = _deinterleave(HD_D)
    per_head = lambda w: w.reshape(w.shape[0], -1, HD_D)[:, :, perm].reshape(w.shape)
    return jnp.concatenate([mq, mk, mv, mo, per_head(gq), per_head(gk), gv], axis=1), _pad_lanes(mg)


def _mla_weights(w_qb, w_kvb):
    wq = w_qb.reshape(Q_LORA, H_A, NOPE_A + ROPE_A)
    q_pe = _pad_lanes(wq[:, :, NOPE_A:][:, :, _deinterleave(ROPE_A)])
    wq = jnp.concatenate([wq[:, :, :NOPE_A].reshape(Q_LORA, -1), q_pe.reshape(Q_LORA, -1)], axis=1)
    wkv = w_kvb.reshape(KV_LORA, H_A, NOPE_A + V_A)
    wkv = jnp.concatenate([wkv[:, :, :NOPE_A].reshape(KV_LORA, -1), wkv[:, :, NOPE_A:].reshape(KV_LORA, -1)], axis=1)
    return wq, wkv


def _even_mixers(p, pg, need_ctx, rope, w_qb, q_lora_norm, w_kvb, kv_lora_norm, q_head_norm, k_head_norm,
                 gdn_conv, a_log, dt_bias, out_norm):
    qn = rms_rows(p, EV_CQ // Q_LORA, Q_LORA, q_lora_norm)
    kvn = rms_rows(p, EV_CKV // KV_LORA, KV_LORA, kv_lora_norm)
    wq, wkv = _mla_weights(w_qb, w_kvb)
    q_all = matmul(qn, wq.astype(BF16), tn=1024, out_dtype=BF16)
    kv_all = matmul(kvn, wkv.astype(BF16), tn=1024, out_dtype=BF16)
    perm = _deinterleave(ROPE_A)
    parts = [
        QKPart(q_all, 0, kv_all, 0, True, q_head_norm[:NOPE_A], k_head_norm[:NOPE_A], NOPE_A, None),
        QKPart(q_all, H_A, pg, 0, False, _pad_lanes(q_head_norm[NOPE_A:][perm]),
               _pad_lanes(k_head_norm[NOPE_A:][perm]), ROPE_A, rope),
    ]
    a_lat = attention_mix(parts, kv_all, H_A, H_A, H_A, NOPE_A + ROPE_A, "lat")
    def gdn_gates(ab):
        ab = ab.reshape(ab.shape[0], ab.shape[1], 4, H_B)
        g = -jnp.exp(a_log)[None, None] * jax.nn.softplus(ab[:, :, :2] + dt_bias[None, None])
        beta = jax.nn.sigmoid(ab[:, :, 2:])
        return _gate_rows([g[:, :, 0], g[:, :, 1], beta[:, :, 0], beta[:, :, 1]], GROUP)

    ab_c, ab_l = _streams(pg[:, ROPE_A:ROPE_A + 4 * H_B])
    b_ctx, b_lat = gdn_mix(p, EV_QKV, EV_Z, gdn_gates(ab_c), gdn_gates(ab_l), gdn_conv, out_norm)
    if not need_ctx:
        return [a_lat, b_lat]
    a_ctx = attention_mix(parts, kv_all, H_A, H_A, H_A, NOPE_A + ROPE_A, "ctx")
    return [jnp.concatenate([a_lat, a_ctx], axis=0), jnp.concatenate([b_lat, b_ctx], axis=0)]


def _odd_mixers(p, pg, need_ctx, rope, gate_bias, out_norm, q_norm, k_norm):
    def mlstm_gates(g):
        g = g.reshape(g.shape[0], g.shape[1], 4, H_C) + gate_bias
        g = GATE_CAP * jnp.tanh(g / GATE_CAP)
        return _gate_rows([g[:, :, 0], g[:, :, 1], jax.nn.log_sigmoid(g[:, :, 2]), jax.nn.log_sigmoid(g[:, :, 3])],
                          CHUNK)

    mg_c, mg_l = _streams(pg[:, :4 * H_C])
    c_ctx, c_lat = mlstm_mix(p, OD_MQ, OD_MK, OD_MV, OD_MO, mlstm_gates(mg_c), mlstm_gates(mg_l), out_norm)
    perm = _deinterleave(HD_D)
    parts = [QKPart(p, OD_GQ // LANES, p, OD_GK // LANES, True, q_norm[perm], k_norm[perm], HD_D, rope)]
    d_lat = attention_mix(parts, p, OD_GV // LANES, H_D, KV_D, HD_D, "lat")
    if not need_ctx:
        return [c_lat, d_lat]
    d_ctx = attention_mix(parts, p, OD_GV // LANES, H_D, KV_D, HD_D, "ctx")
    return [jnp.concatenate([c_lat, c_ctx], axis=0), jnp.concatenate([d_lat, d_ctx], axis=0)]


def kernel(x, c, ctx, c_ctx, mod_down, mod_up, mod_bias, norm_mix, norm_ffn, ffn_up, ffn_conv, ffn_down,
           ev_w_in, ev_w_out, mla_w_qb, mla_q_lora_norm, mla_w_kvb, mla_kv_lora_norm, mla_q_head_norm,
           mla_k_head_norm, gdn_conv, gdn_a_log, gdn_dt_bias, gdn_out_norm, od_w_in, od_w_out,
           mlstm_gate_bias, mlstm_out_norm, gqa_q_norm, gqa_k_norm):
    n_rows = SEQ // GRID_W
    rope_a = _rope_tables(n_rows, ROPE_A)
    rope_d = _rope_tables(n_rows, HD_D)
    cond = jnp.concatenate([c, c_ctx[None, :], jnp.zeros((MOD_ROWS - BATCH - 1, D_MODEL), F32)], axis=0)
    mods = adaln_all(cond, mod_down, mod_up, mod_bias)
    xs = jnp.concatenate([x.reshape(N_LAT, D_MODEL), ctx.reshape(N_CTX, D_MODEL)], axis=0)
    for layer in range(DEPTH):
        need_ctx = layer < DEPTH - 1
        j = layer // 2
        mod = mods[layer].reshape(MOD_ROWS, 1, 6 * D_MODEL)
        u = norm_mod(xs, norm_mix[layer], mod, 0)
        w_main, w_gate = _split_even(ev_w_in[j]) if layer % 2 == 0 else _split_odd(od_w_in[j])
        p = matmul(u, w_main.astype(BF16), tn=1024, out_dtype=BF16)
        pg = matmul(u, w_gate.astype(BF16), tn=LANES, out_dtype=F32)
        if layer % 2 == 0:
            y = _even_mixers(p, pg, need_ctx, rope_a, mla_w_qb[j], mla_q_lora_norm[j],
                             mla_w_kvb[j], mla_kv_lora_norm[j], mla_q_head_norm[j],
                             mla_k_head_norm[j], gdn_conv[j], gdn_a_log[j], gdn_dt_bias[j], gdn_out_norm[j])
            w_out = ev_w_out[j]
        else:
            y = _odd_mixers(p, pg, need_ctx, rope_d, mlstm_gate_bias[j], mlstm_out_norm[j], gqa_q_norm[j],
                            gqa_k_norm[j])
            w_out = od_w_out[j]
        if not need_ctx:
            xs = xs[:N_LAT]
        xs = matmul(y, w_out.astype(BF16), tn=1024, out_dtype=F32, res=xs, mod=mod, gate_chunk=2)
        u2 = norm_mod(xs, norm_ffn[layer], mod, 3)
        hmid = ffn_up_proj(u2, ffn_up[layer].astype(BF16), ffn_conv[layer])
        xs = matmul(hmid, ffn_down[layer].astype(BF16), tn=256, out_dtype=F32, res=xs, mod=mod, gate_chunk=5,
                    single_buffer_a=True)
    return xs.reshape(BATCH, SEQ, D_MODEL)
```

```python
import functools
import math
from typing import NamedTuple, Optional

import numpy as np
import jax
import jax.numpy as jnp
from jax import lax
from jax.experimental import pallas as pl
from jax.experimental.pallas import tpu as pltpu

F32 = jnp.float32
BF16 = jnp.bfloat16

D_MODEL = 4096
BATCH = 4
SEQ = 4096
DEPTH = 4
CTX_LEN = 256
GRID_W = 64
MOD_RANK = 256
H_A, Q_LORA, KV_LORA, NOPE_A, ROPE_A, V_A = 16, 1024, 512, 128, 64, 128
H_B, DK_B, DV_B, SHORT_CONV = 16, 128, 128, 3
H_C, DK_C, DV_C, GATE_CAP = 4, 256, 512, 15.0
H_D, KV_D, HD_D = 16, 4, 128
D_FF, FFN_CONV = 11008, 3
CHUNK = 64
ROPE_THETA = 10000.0
EPS = 1e-6

N_LAT = BATCH * SEQ
N_CTX = BATCH * CTX_LEN
MOD_ROWS = 8
LANES = 128

VMEM_LIMIT_V7X = 56 * 1024 * 1024

TM = 1024
TM_NORM = 256
TM_RMS = 512
TQ = 1024
Q_SUB = 256
GROUP = 256


def _cparams(sem):
    return pltpu.CompilerParams(dimension_semantics=sem, vmem_limit_bytes=VMEM_LIMIT_V7X)


def _mod_row(i, tm):
    return jnp.minimum((i * tm) // SEQ, BATCH)


def _adaln_kernel(cond_ref, down_ref, up_ref, bias_ref, o_ref):
    cond = cond_ref[...]
    a = cond * jax.nn.sigmoid(cond)
    t = jnp.dot(a, down_ref[0], preferred_element_type=F32, precision=lax.Precision.HIGHEST)
    o = jnp.dot(t, up_ref[0], preferred_element_type=F32, precision=lax.Precision.HIGHEST)
    o_ref[0] = o + bias_ref[0]


def adaln_all(cond, mod_down, mod_up, mod_bias):
    depth, d, rank = mod_down.shape
    n = mod_up.shape[-1]
    tn = d
    return pl.pallas_call(
        _adaln_kernel,
        grid=(depth, n // tn),
        in_specs=[
            pl.BlockSpec((MOD_ROWS, d), lambda l, j: (0, 0)),
            pl.BlockSpec((1, d, rank), lambda l, j: (l, 0, 0)),
            pl.BlockSpec((1, rank, tn), lambda l, j: (l, 0, j)),
            pl.BlockSpec((1, 1, tn), lambda l, j: (l, 0, j)),
        ],
        out_specs=pl.BlockSpec((1, MOD_ROWS, tn), lambda l, j: (l, 0, j)),
        out_shape=jax.ShapeDtypeStruct((depth, MOD_ROWS, n), F32),
        compiler_params=_cparams(("arbitrary", "arbitrary")),
        name="adaln",
    )(cond, mod_down, mod_up, mod_bias.reshape(depth, 1, n))


def _norm_mod_kernel(x_ref, w_ref, sh_ref, sc_ref, o_ref):
    x = x_ref[...]
    y = x * lax.rsqrt(jnp.mean(x * x, axis=-1, keepdims=True) + EPS)
    y = y * w_ref[...]
    o_ref[...] = (y * (1.0 + sc_ref[0]) + sh_ref[0]).astype(o_ref.dtype)


def norm_mod(xs, w, mod, shift_chunk, pad=0):
    m, d = xs.shape
    tm = TM_NORM
    out = pl.pallas_call(
        _norm_mod_kernel,
        grid=(m // tm,),
        in_specs=[
            pl.BlockSpec((tm, d), lambda i: (i, 0)),
            pl.BlockSpec((1, d), lambda i: (0, 0)),
            pl.BlockSpec((1, 1, d), lambda i: (_mod_row(i, tm), 0, shift_chunk)),
            pl.BlockSpec((1, 1, d), lambda i: (_mod_row(i, tm), 0, shift_chunk + 1)),
        ],
        out_specs=pl.BlockSpec((pl.Element(tm), pl.Element(d)),
                               lambda i: (pl.multiple_of(pad + i * tm, math.gcd(pad, tm) or tm), 0)),
        out_shape=jax.ShapeDtypeStruct((m + 2 * pad, d), BF16),
        compiler_params=_cparams(("parallel",)),
        name="norm_mod",
    )(xs, w.reshape(1, d), mod, mod)
    if pad:
        zeros = jnp.zeros((pad, d), BF16)
        out = lax.dynamic_update_slice(out, zeros, (0, 0))
        out = lax.dynamic_update_slice(out, zeros, (m + pad, 0))
    return out


def _rms_rows_kernel(x_ref, w_ref, o_ref):
    x = x_ref[...].astype(F32)
    y = x * lax.rsqrt(jnp.mean(x * x, axis=-1, keepdims=True) + EPS)
    o_ref[...] = (y * w_ref[...]).astype(o_ref.dtype)


def rms_rows(p, col_block, width, w):
    m = p.shape[0]
    tm = TM_RMS
    return pl.pallas_call(
        _rms_rows_kernel,
        grid=(m // tm,),
        in_specs=[
            pl.BlockSpec((tm, width), lambda i: (i, col_block)),
            pl.BlockSpec((1, width), lambda i: (0, 0)),
        ],
        out_specs=pl.BlockSpec((tm, width), lambda i: (i, 0)),
        out_shape=jax.ShapeDtypeStruct((m, width), BF16),
        compiler_params=_cparams(("parallel",)),
        name="rms_rows",
    )(p, w.reshape(1, width))


def _mm_kernel(*refs, n_parts, has_res, tail_off):
    a_refs, w_refs, rest = refs[:n_parts], refs[n_parts:2 * n_parts], refs[2 * n_parts:]
    acc = jnp.dot(a_refs[0][...], w_refs[0][...], preferred_element_type=F32)
    for a_ref, w_ref in zip(a_refs[1:], w_refs[1:]):
        acc += jnp.dot(a_ref[...], w_ref[...], preferred_element_type=F32)
    if has_res:
        res_ref, gate_ref, o_ref = rest
        acc = res_ref[...] + gate_ref[0] * acc
    elif tail_off is not None:
        o_ref, tail_ref = rest

        @pl.when(pl.program_id(1) == pl.num_programs(1) - 1)
        def _():
            tail_ref[...] = acc[:, tail_off:tail_off + LANES]
    else:
        (o_ref,) = rest
    o_ref[...] = acc.astype(o_ref.dtype)


def matmul(a_parts, w, *, tn, out_dtype, res=None, mod=None, gate_chunk=None, single_buffer_a=False,
           f32_tail=False):
    if not isinstance(a_parts, (list, tuple)):
        a_parts = [a_parts]
    n_parts = len(a_parts)
    m, kp = a_parts[0].shape
    n = w.shape[1]
    tm, tn = TM, min(tn, n)
    assert m % tm == 0 and w.shape[0] == kp * n_parts and all(a.shape == (m, kp) for a in a_parts)
    has_res = res is not None
    nj = pl.cdiv(n, tn)
    a_mode = dict(pipeline_mode=pl.Buffered(1)) if single_buffer_a else {}
    in_specs = [pl.BlockSpec((tm, kp), lambda i, j: (i, 0), **a_mode) for _ in a_parts]
    in_specs += [pl.BlockSpec((kp, tn), functools.partial(lambda i, j, part: (part, j), part=part))
                 for part in range(n_parts)]
    args = list(a_parts) + [w] * n_parts
    out_specs = pl.BlockSpec((tm, tn), lambda i, j: (i, j))
    out_shape = jax.ShapeDtypeStruct((m, n), out_dtype)
    tail_off = None
    if has_res:
        assert n % tn == 0 and mod.shape[-1] % tn == 0 and not f32_tail
        gate_off = gate_chunk * (n // tn)
        in_specs += [
            pl.BlockSpec((tm, tn), lambda i, j: (i, j)),
            pl.BlockSpec((1, 1, tn), lambda i, j: (_mod_row(i, tm), 0, gate_off + j)),
        ]
        args += [res, mod]
    elif f32_tail:
        tail_off = n - LANES - (nj - 1) * tn
        assert 0 <= tail_off and tail_off % LANES == 0
        out_specs = [out_specs, pl.BlockSpec((tm, LANES), lambda i, j: (i, 0))]
        out_shape = [out_shape, jax.ShapeDtypeStruct((m, LANES), F32)]
    return pl.pallas_call(
        functools.partial(_mm_kernel, n_parts=n_parts, has_res=has_res, tail_off=tail_off),
        grid=(m // tm, nj),
        in_specs=in_specs,
        out_specs=out_specs,
        out_shape=out_shape,
        compiler_params=_cparams(("parallel", "arbitrary")),
        name="matmul",
    )(*args)


HALO = 16
FFN_SUB = 1024


def _seq_pos(i, tm):
    r = i * tm + lax.broadcasted_iota(jnp.int32, (tm, 1), 0)
    pos = jnp.where(r < N_LAT, r & (SEQ - 1), (r - N_LAT) & (CTX_LEN - 1))
    last = jnp.where(r < N_LAT, SEQ - 1, CTX_LEN - 1)
    return pos == 0, pos == last


def _conv3_rows(h, h_prev_row, h_next_row, cw, first, last):
    tm = h.shape[0]
    t = lax.broadcasted_iota(jnp.int32, (tm, 1), 0)
    dn = jnp.where(t == 0, h_prev_row, pltpu.roll(h, 1, 0))
    dn = jnp.where(first, 0.0, dn)
    up = jnp.where(t == tm - 1, h_next_row, pltpu.roll(h, tm - 1, 0))
    up = jnp.where(last, 0.0, up)
    return cw[0:1, :] * dn + cw[1:2, :] * h + cw[2:3, :] * up


def _ffn_up_kernel(a_ref, wg_ref, wu_ref, cg_ref, cu_ref, o_ref, *, tm):
    i = pl.program_id(0)
    first, last = _seq_pos(i, tm)
    sub = min(tm, FFN_SUB)
    ext = sub + 2 * HALO
    mid = slice(HALO, HALO + sub)
    for r0 in range(0, tm, sub):
        a = a_ref[r0:r0 + ext, :]
        f, l = first[r0:r0 + sub], last[r0:r0 + sub]

        def branch(w_ref, c_ref):
            h = jnp.dot(a, w_ref[...], preferred_element_type=F32)
            cw = c_ref[...]
            dn = jnp.where(f, 0.0, pltpu.roll(h, 1, 0)[mid])
            up = jnp.where(l, 0.0, pltpu.roll(h, ext - 1, 0)[mid])
            return cw[0:1, :] * dn + cw[1:2, :] * h[mid] + cw[2:3, :] * up

        g = branch(wg_ref, cg_ref)
        u = branch(wu_ref, cu_ref)
        o_ref[r0:r0 + sub, :] = (g * jax.nn.sigmoid(g) * u).astype(o_ref.dtype)


def ffn_up_proj(u2p, w_up, conv_w, *, tn=256):
    m, d = u2p.shape[0] - 2 * HALO, u2p.shape[1]
    tm = TM
    dff = w_up.shape[1] // 2
    nj = dff // tn
    assert dff % tn == 0 and m % tm == 0
    return pl.pallas_call(
        functools.partial(_ffn_up_kernel, tm=tm),
        grid=(m // tm, nj),
        in_specs=[
            pl.BlockSpec((pl.Element(tm + 2 * HALO), pl.Element(d)), lambda i, j: (i * tm, 0)),
            pl.BlockSpec((d, tn), lambda i, j: (0, j)),
            pl.BlockSpec((d, tn), lambda i, j: (0, j + nj)),
            pl.BlockSpec((FFN_CONV, tn), lambda i, j: (0, j)),
            pl.BlockSpec((FFN_CONV, tn), lambda i, j: (0, j + nj)),
        ],
        out_specs=pl.BlockSpec((tm, tn), lambda i, j: (i, j)),
        out_shape=jax.ShapeDtypeStruct((m, dff), BF16),
        compiler_params=_cparams(("parallel", "arbitrary")),
        name="ffn_up",
    )(u2p, w_up, w_up, conv_w, conv_w)


class QKPart(NamedTuple):
    q_arr: jax.Array
    q_col: int
    k_arr: jax.Array
    k_col: int
    k_per_head: bool
    wq: jax.Array
    wk: jax.Array
    n_valid: int
    tabs: Optional[tuple]


def _swap_halves(y, n_valid):
    half = n_valid // 2
    if n_valid == LANES:
        return pltpu.roll(y, half, 1)
    lane = lax.broadcasted_iota(jnp.int32, (1, LANES), 1)
    return jnp.where(lane < half, pltpu.roll(y, LANES - half, 1), pltpu.roll(y, half, 1))


def _norm_rope(x, w, n_valid, tabs):
    if n_valid < LANES:
        lane = lax.broadcasted_iota(jnp.int32, (1, LANES), 1)
        x = jnp.where(lane < n_valid, x, 0.0)
    y = x * lax.rsqrt(jnp.sum(x * x, axis=-1, keepdims=True) * (1.0 / n_valid) + EPS) * w
    if tabs is not None:
        y = y * tabs[0] + _swap_halves(y, n_valid) * tabs[1]
    return y


def _attn_kernel(*refs, n_parts, n_valid, roped, segs, tq, scale):
    n_seg = len(segs)
    it = iter(refs)
    q_refs = [next(it) for _ in range(n_parts)]
    k_refs = [[next(it) for _ in range(n_parts)] for _ in range(n_seg)]
    v_refs = [next(it) for _ in range(n_seg)]
    wq_refs = [next(it) for _ in range(n_parts)]
    wk_refs = [next(it) for _ in range(n_parts)]
    tab_refs = [(next(it), next(it)) if roped[p] else None for p in range(n_parts)]
    o_ref = next(it)
    k_scr = [next(it) for _ in range(n_seg)]
    v_scr = [next(it) for _ in range(n_seg)]
    t = pl.program_id(3)

    @pl.when(jnp.logical_and(pl.program_id(2) == 0, t == 0))
    def _():
        for si, seg in enumerate(segs):
            for p in range(n_parts):
                tabs = None
                if seg == "lat" and roped[p]:
                    tabs = (tab_refs[p][0][...], tab_refs[p][1][...])
                y = _norm_rope(k_refs[si][p][...].astype(F32), wk_refs[p][...], n_valid[p], tabs)
                k_scr[si][:, p * LANES:(p + 1) * LANES] = y.astype(BF16)
            rows = v_scr[si].shape[0]
            lane = lax.broadcasted_iota(jnp.int32, (rows, LANES), 1)
            v_scr[si][:, :LANES] = v_refs[si][...]
            v_scr[si][:, LANES:] = jnp.where(lane == 0, 1.0, 0.0).astype(BF16)

    parts = []
    for p in range(n_parts):
        tabs = None
        if roped[p]:
            r = pl.ds(pl.multiple_of(t * tq, tq), tq)
            tabs = (tab_refs[p][0][r, :], tab_refs[p][1][r, :])
        parts.append(_norm_rope(q_refs[p][...].astype(F32), wq_refs[p][...], n_valid[p], tabs) * scale)
    q = jnp.concatenate(parts, axis=1).astype(BF16)
    dn_t = (((1,), (1,)), ((), ()))
    sub = min(tq, Q_SUB)
    for r0 in range(0, tq, sub):
        qs = q[r0:r0 + sub]
        ss = [lax.dot_general(qs, k_scr[si][...], dn_t, preferred_element_type=F32) for si in range(n_seg)]
        m = jnp.max(ss[0], axis=-1, keepdims=True)
        for s in ss[1:]:
            m = jnp.maximum(m, jnp.max(s, axis=-1, keepdims=True))
        o_ext = None
        for si, s in enumerate(ss):
            pv = jnp.dot(jnp.exp((s - m).astype(BF16)), v_scr[si][...], preferred_element_type=F32)
            o_ext = pv if o_ext is None else o_ext + pv
        o_ref[r0:r0 + sub, :] = (o_ext[:, :LANES] / o_ext[:, LANES:LANES + 1]).astype(o_ref.dtype)


def attention_mix(parts, v_arr, v_col, n_heads, n_kv, dk, q_stream):
    g = n_heads // n_kv
    ctx_blk0 = N_LAT // CTX_LEN
    lat = q_stream == "lat"
    tq = min(TQ, SEQ) if lat else CTX_LEN
    nt = SEQ // tq if lat else 1
    segs = ("ctx", "lat") if lat else ("ctx",)
    roped = tuple(lat and part.tabs is not None for part in parts)
    seg_rows = {"ctx": CTX_LEN, "lat": SEQ}
    seg_blk0 = {"ctx": ctx_blk0, "lat": 0}
    q_blk0 = 0 if lat else ctx_blk0

    def slab(rows, blk0, col, per_head):
        return pl.BlockSpec((rows, LANES), lambda b, hk, gi, t: (blk0 + b, col + (hk if per_head else 0)))

    in_specs, args = [], []
    for part in parts:
        in_specs.append(pl.BlockSpec(
            (tq, LANES), functools.partial(lambda b, hk, gi, t, col: (q_blk0 + b * nt + t, col + hk * g + gi),
                                           col=part.q_col)))
        args.append(part.q_arr)
    for seg in segs:
        for part in parts:
            in_specs.append(slab(seg_rows[seg], seg_blk0[seg], part.k_col, part.k_per_head))
            args.append(part.k_arr)
    for seg in segs:
        in_specs.append(slab(seg_rows[seg], seg_blk0[seg], v_col, True))
        args.append(v_arr)
    const = lambda b, hk, gi, t: (0, 0)
    for w in [part.wq for part in parts] + [part.wk for part in parts]:
        in_specs.append(pl.BlockSpec((1, LANES), const))
        args.append(w.reshape(1, LANES))
    for part, r in zip(parts, roped):
        if r:
            in_specs += [pl.BlockSpec((SEQ, LANES), const)] * 2
            args += list(part.tabs)
    n_parts = len(parts)
    scratch = [pltpu.VMEM((seg_rows[seg], n_parts * LANES), BF16) for seg in segs]
    scratch += [pltpu.VMEM((seg_rows[seg], 2 * LANES), BF16) for seg in segs]
    q_rows = N_LAT if lat else N_CTX
    return pl.pallas_call(
        functools.partial(_attn_kernel, n_parts=n_parts, n_valid=tuple(part.n_valid for part in parts), roped=roped,
                          segs=segs, tq=tq, scale=dk ** -0.5),
        grid=(BATCH, n_kv, g, nt),
        in_specs=in_specs,
        out_specs=pl.BlockSpec((tq, LANES), lambda b, hk, gi, t: (b * nt + t, hk * g + gi)),
        out_shape=jax.ShapeDtypeStruct((q_rows, n_heads * LANES), BF16),
        scratch_shapes=scratch,
        compiler_params=_cparams(("arbitrary",) * 4),
        name="attention",
    )(*args)


def _col_from_row(row, eye):
    return jnp.sum(jnp.where(eye, row, 0.0), axis=1, keepdims=True)


def _rows(idx, size):
    return pl.ds(pl.multiple_of(idx * size, size), size)


def _seq_conv_silu(x, cw):
    zero = jnp.zeros((1, x.shape[1]), F32)
    t = lax.broadcasted_iota(jnp.int32, (x.shape[0], 1), 0)
    c = _conv3_rows(x, zero, zero, cw, t == 0, t == x.shape[0] - 1)
    return c * jax.nn.sigmoid(c)


N_LEVELS = int(math.log2(CHUNK))
MASK_SAME, MASK_EYE, MASK_BE = 0, 1, 2
MASK_LEVEL = 4
N_MASKS = MASK_LEVEL + 2 * N_LEVELS


def _gdn_init_masks(mask_ref):
    n = GROUP
    ii = lax.broadcasted_iota(jnp.int32, (n, n), 0)
    jj = lax.broadcasted_iota(jnp.int32, (n, n), 1)
    same = (ii // CHUNK) == (jj // CHUNK)

    def put(idx, m):
        mask_ref[idx] = jnp.where(m, 1.0, 0.0)

    put(MASK_SAME, same)
    put(MASK_EYE, ii == jj)
    put(MASK_BE, jnp.logical_and(same, jj <= ii))
    put(MASK_BE + 1, jnp.logical_and(same, jj >= ii))
    for level in range(N_LEVELS):
        s = 1 << level
        blk = (ii // (2 * s)) == (jj // (2 * s))
        ih = (ii // s) & 1
        jh = (jj // s) & 1
        put(MASK_LEVEL + level, jnp.logical_and(blk, jnp.logical_and(ih == 1, jh == 0)))
        put(MASK_LEVEL + N_LEVELS + level, jnp.logical_and(blk, jnp.logical_and(ih == 0, jh == 1)))


def _gdn_prep_dir(q, k, v, gates, d, mask_ref):
    def mask(idx):
        return mask_ref[idx] > 0.5

    eye = mask(MASK_EYE)
    same = mask(MASK_SAME)
    be = mask(MASK_BE + d)
    be_t = mask(MASK_BE + 1 - d)
    k16 = k.astype(BF16)
    dn_t = (((1,), (1,)), ((), ()))
    kk = lax.dot_general(k16, k16, dn_t, preferred_element_type=F32)
    qk0 = lax.dot_general(q.astype(BF16), k16, dn_t, preferred_element_type=F32)
    g_row, b_row = gates[d:d + 1, :], gates[2 + d:3 + d, :]
    g_col = _col_from_row(g_row, eye)
    b_col = _col_from_row(b_row, eye)
    gc_col = jnp.sum(jnp.where(be, g_row, 0.0), axis=1, keepdims=True)
    gc_row = jnp.sum(jnp.where(be_t, g_col, 0.0), axis=0, keepdims=True)
    tot_col = jnp.sum(jnp.where(same, g_row, 0.0), axis=1, keepdims=True)
    decay = jnp.where(be, jnp.exp(jnp.where(be, gc_col - gc_row, 0.0)), 0.0)
    mm = jnp.where(jnp.logical_and(be, jnp.logical_not(eye)), b_col * kk * decay, 0.0)
    qk = qk0 * decay
    t_inv = jnp.where(eye, 1.0, 0.0) - jnp.where(mask(MASK_LEVEL + d * N_LEVELS), mm, 0.0)
    for level in range(1, N_LEVELS):
        mc = jnp.where(mask(MASK_LEVEL + d * N_LEVELS + level), mm, 0.0)
        tb = t_inv.astype(BF16)
        y = jnp.dot(tb, mc.astype(BF16), preferred_element_type=F32)
        t_inv = t_inv - jnp.dot(y.astype(BF16), tb, preferred_element_type=F32)
    egc = jnp.exp(gc_col)
    kb = k * b_col
    rhs = jnp.concatenate([v * b_col, kb * egc], axis=1).astype(BF16)
    uw = jnp.dot(t_inv.astype(BF16), rhs, preferred_element_type=F32)
    dv = v.shape[1]
    return uw[:, :dv], uw[:, dv:], q * egc, k * jnp.exp(tot_col - gc_col), qk, jnp.exp(tot_col)


def _gdn_stream(x_refs, cw_refs, gates_ref, z_ref, onorm_ref, y_ref, states, scr):
    q_ref, k_ref, v_ref = x_refs
    mask_ref, qs_ref, ks_ref, vs_ref, o_ref = scr[:5]
    slots = (scr[5:10], scr[10:15])
    t_len = q_ref.shape[0]
    n_groups = t_len // GROUP
    cpg = GROUP // CHUNK
    dk = q_ref.shape[1]

    def l2(x):
        return x * lax.rsqrt(jnp.sum(x * x, axis=-1, keepdims=True) + EPS)

    rows_all = pl.ds(0, t_len)
    qs_ref[rows_all, :] = l2(_seq_conv_silu(q_ref[...].astype(F32), cw_refs[0][...])) * dk ** -0.5
    ks_ref[rows_all, :] = l2(_seq_conv_silu(k_ref[...].astype(F32), cw_refs[1][...]))
    vs_ref[rows_all, :] = _seq_conv_silu(v_ref[...].astype(F32), cw_refs[2][...])

    def load_group(g):
        r = _rows(g, GROUP)
        return qs_ref[r, :], ks_ref[r, :], vs_ref[r, :], gates_ref[g]

    def prep(slot, d, inputs):
        u_ref, wq_ref, kg_ref, qk_ref, gl_ref = slot
        u, w, qg, kg, qk, glast = _gdn_prep_dir(*inputs, d, mask_ref)
        u_ref[d] = u
        kg_ref[d] = kg.astype(BF16)
        qk_ref[d] = qk.astype(BF16)
        for c in range(cpg):
            rc = slice(c * CHUNK, (c + 1) * CHUNK)
            wq_ref[d, c] = jnp.concatenate([w[rc], qg[rc]], axis=0).astype(BF16)
            gl_ref[d, c] = jnp.broadcast_to(glast[c * CHUNK:c * CHUNK + 8, :], (8, LANES))

    def scan_group(slot, d, state):
        u_ref, wq_ref, kg_ref, qk_ref, gl_ref = slot
        u = u_ref[d]
        order = range(cpg) if d == 0 else range(cpg - 1, -1, -1)
        v_parts, o_parts = [None] * cpg, [None] * cpg
        for c in order:
            rc = slice(c * CHUNK, (c + 1) * CHUNK)
            ws = jnp.dot(wq_ref[d, c], state.astype(BF16), preferred_element_type=F32)
            v_new = u[rc] - ws[:CHUNK]
            o_parts[c] = ws[CHUNK:]
            upd = lax.dot_general(kg_ref[d, rc, :], v_new.astype(BF16), (((0,), (0,)), ((), ())),
                                  preferred_element_type=F32)
            state = state * gl_ref[d, c][0:1, :] + upd
            v_parts[c] = v_new
        v_all = jnp.concatenate(v_parts, axis=0).astype(BF16)
        o = jnp.concatenate(o_parts, axis=0) + jnp.dot(qk_ref[d], v_all, preferred_element_type=F32)
        return o, state

    def step(t, carry, cur, nxt, accumulate, prefetch):
        s_f, s_b = carry
        g_b = n_groups - 1 - t
        if prefetch:
            in_f, in_b = load_group(t + 1), load_group(g_b - 1)
            prep(nxt, 0, in_f)
            prep(nxt, 1, in_b)
        o_f, s_f = scan_group(cur, 0, s_f)
        o_b, s_b = scan_group(cur, 1, s_b)
        if accumulate:
            o_ref[_rows(t, GROUP), :] += o_f
            o_ref[_rows(g_b, GROUP), :] += o_b
        else:
            o_ref[_rows(t, GROUP), :] = o_f
            o_ref[_rows(g_b, GROUP), :] = o_b
        return s_f, s_b

    in_f, in_b = load_group(0), load_group(n_groups - 1)
    prep(slots[0], 0, in_f)
    prep(slots[0], 1, in_b)
    if n_groups == 1:
        o_f, s_f = scan_group(slots[0], 0, states[0])
        o_b, s_b = scan_group(slots[0], 1, states[1])
        o_ref[pl.ds(0, GROUP), :] = o_f + o_b
        states = (s_f, s_b)
    else:
        n_pairs = n_groups // 2
        assert n_groups == 2 * n_pairs and n_pairs % 2 == 0

        def pair(p, carry, accumulate, last=False):
            carry = step(2 * p, carry, slots[0], slots[1], accumulate, True)
            return step(2 * p + 1, carry, slots[1], slots[0], accumulate, not last)

        states = lax.fori_loop(0, n_pairs // 2, lambda p, c: pair(p, c, False), states)
        states = lax.fori_loop(n_pairs // 2, n_pairs - 1, lambda p, c: pair(p, c, True), states)
        states = pair(n_pairs - 1, states, True, last=True)

    o = o_ref[rows_all, :]
    o = o * lax.rsqrt(jnp.mean(o * o, axis=-1, keepdims=True) + EPS) * onorm_ref[...]
    z = z_ref[...].astype(F32)
    y_ref[...] = (o * (z * jax.nn.sigmoid(z))).astype(y_ref.dtype)
    return states


def _gdn_kernel(qc, kc, vc, zc, gc, ql, kl, vl, zl, gl, cwq, cwk, cwv, onorm, yc_ref, yl_ref, *scr):
    @pl.when(jnp.logical_and(pl.program_id(0) == 0, pl.program_id(1) == 0))
    def _():
        _gdn_init_masks(scr[0])

    cws = (cwq, cwk, cwv)
    zero = jnp.zeros((kc.shape[1], vc.shape[1]), F32)
    states = _gdn_stream((qc, kc, vc), cws, gc.at[0, 0], zc, onorm, yc_ref, (zero, zero), scr)
    _gdn_stream((ql, kl, vl), cws, gl.at[0, 0], zl, onorm, yl_ref, states, scr)


def gdn_mix(p, col_q, col_z, gates_c, gates_l, conv_w, out_norm):
    h, dk, dv = H_B, DK_B, DV_B
    assert dk == LANES and dv == LANES and col_q % LANES == 0 and col_z % LANES == 0
    cq, cz = col_q // LANES, col_z // LANES
    ctx_blk0 = N_LAT // CTX_LEN
    cpg = GROUP // CHUNK

    def slab(t, row0, col0):
        return pl.BlockSpec((t, LANES), lambda bi, hi: (row0 + bi, col0 + hi))

    def stream_specs(t, row0, gates):
        return [slab(t, row0, cq), slab(t, row0, cq + h), slab(t, row0, cq + 2 * h), slab(t, row0, cz),
                pl.BlockSpec((1, 1) + gates.shape[2:], lambda bi, hi: (bi, hi, 0, 0, 0))]

    in_specs = stream_specs(CTX_LEN, ctx_blk0, gates_c) + stream_specs(SEQ, 0, gates_l)
    in_specs += [pl.BlockSpec((SHORT_CONV, LANES), functools.partial(lambda bi, hi, off: (0, off + hi), off=off))
                 for off in (0, h, 2 * h)]
    in_specs.append(pl.BlockSpec((1, dv), lambda bi, hi: (0, 0)))
    slot = [
        pltpu.VMEM((2, GROUP, dv), F32),
        pltpu.VMEM((2, cpg, 2 * CHUNK, dk), BF16),
        pltpu.VMEM((2, GROUP, dk), BF16),
        pltpu.VMEM((2, GROUP, GROUP), BF16),
        pltpu.VMEM((2, cpg, 8, LANES), F32),
    ]
    scratch = [
        pltpu.VMEM((N_MASKS, GROUP, GROUP), F32),
        pltpu.VMEM((SEQ, dk), F32), pltpu.VMEM((SEQ, dk), F32), pltpu.VMEM((SEQ, dv), F32),
        pltpu.VMEM((SEQ, dv), F32),
    ] + slot + slot
    return pl.pallas_call(
        _gdn_kernel,
        grid=(BATCH, h),
        in_specs=in_specs,
        out_specs=[pl.BlockSpec((CTX_LEN, dv), lambda bi, hi: (bi, hi)),
                   pl.BlockSpec((SEQ, dv), lambda bi, hi: (bi, hi))],
        out_shape=[jax.ShapeDtypeStruct((N_CTX, h * dv), BF16), jax.ShapeDtypeStruct((N_LAT, h * dv), BF16)],
        scratch_shapes=scratch,
        compiler_params=_cparams(("arbitrary", "arbitrary")),
        name="gdn_mix",
    )(p, p, p, p, gates_c, p, p, p, p, gates_l, conv_w, conv_w, conv_w, out_norm.reshape(1, dv))


def _tri_masks(n, rev):
    ii = lax.broadcasted_iota(jnp.int32, (n, n), 0)
    jj = lax.broadcasted_iota(jnp.int32, (n, n), 1)
    eye = ii == jj
    if rev:
        be, be_t = jj >= ii, ii >= jj
    else:
        be, be_t = jj <= ii, ii <= jj
    return eye, be, be_t


def _mlstm_chunk(q, k, v, i_row, f_row, carry, c_ref, rev):
    n_vec, m = carry
    n = q.shape[0]
    eye, be, be_t = _tri_masks(n, rev)
    f_col = _col_from_row(f_row, eye)
    i_col = _col_from_row(i_row, eye)
    b_col = jnp.sum(jnp.where(be, f_row, 0.0), axis=1, keepdims=True)
    b_row = jnp.sum(jnp.where(be_t, f_col, 0.0), axis=0, keepdims=True)
    b_tot = jnp.sum(f_row, axis=1, keepdims=True)
    d = jnp.where(be, b_col - b_row + i_row, -jnp.inf)
    inter = b_col + m
    m_t = jnp.maximum(inter, jnp.max(d, axis=1, keepdims=True))
    a_inter = jnp.exp(inter - m_t)
    qb = q.astype(BF16)
    kbf = k.astype(BF16)
    s = lax.dot_general(qb, kbf, (((1,), (1,)), ((), ())), preferred_element_type=F32)
    p = jnp.exp(d - m_t) * s
    c_mat = c_ref[...]
    num = a_inter * jnp.dot(qb, c_mat.astype(BF16), preferred_element_type=F32) + jnp.dot(
        p.astype(BF16), v.astype(BF16), preferred_element_type=F32)
    den = a_inter * jnp.sum(q * n_vec, axis=1, keepdims=True) + jnp.sum(p, axis=1, keepdims=True)
    h = num / jnp.maximum(jnp.abs(den), jnp.exp(-m_t))
    w_end_row = b_tot - b_row + i_row
    w_end_col = b_tot - b_col + i_col
    m_new = jnp.maximum(b_tot + m, jnp.max(w_end_row, axis=1, keepdims=True))
    g_old = jnp.exp(b_tot + m - m_new)
    gs_col = jnp.exp(w_end_col - m_new)
    upd = lax.dot_general(kbf, (gs_col * v).astype(BF16), (((0,), (0,)), ((), ())), preferred_element_type=F32)
    c_ref[...] = g_old * c_mat + upd
    n_new = g_old * n_vec + jnp.sum(gs_col * k, axis=0, keepdims=True)
    return h, (n_new, m_new)


def _mlstm_stream(q_ref, k_ref, v_ref, gates_ref, opre_ref, onorm_ref, y_ref, carries, o_ref, c_refs, unroll):
    t_len = q_ref.shape[0]
    n_chunks = t_len // CHUNK
    half = n_chunks // 2
    assert n_chunks == 2 * half
    scale = q_ref.shape[1] ** -0.5

    def chunk(c, carry, rev):
        r = _rows(c, CHUNK)
        gates = gates_ref[c]
        d = 1 if rev else 0
        return _mlstm_chunk(q_ref[r, :].astype(F32) * scale, k_ref[r, :].astype(F32), v_ref[r, :],
                            gates[d:d + 1, :], gates[2 + d:3 + d, :], carry, c_refs[d], rev)

    def step(t, carry, accumulate):
        cf, cb = carry
        c_b = n_chunks - 1 - t
        o_f, cf = chunk(t, cf, False)
        o_b, cb = chunk(c_b, cb, True)
        if accumulate:
            o_ref[_rows(t, CHUNK), :] += o_f
            o_ref[_rows(c_b, CHUNK), :] += o_b
        else:
            o_ref[_rows(t, CHUNK), :] = o_f
            o_ref[_rows(c_b, CHUNK), :] = o_b
        return cf, cb

    carries = lax.fori_loop(0, half, lambda t, c: step(t, c, False), carries, unroll=unroll)
    carries = lax.fori_loop(half, n_chunks, lambda t, c: step(t, c, True), carries, unroll=unroll)
    o = o_ref[pl.ds(0, t_len), :]
    o = o * lax.rsqrt(jnp.mean(o * o, axis=-1, keepdims=True) + EPS) * onorm_ref[0]
    y_ref[...] = (o * jax.nn.sigmoid(opre_ref[...].astype(F32))).astype(y_ref.dtype)
    return carries


def _mlstm_kernel(qc, kc, vc, oc, gc, ql, kl, vl, ol, gl, onorm, yc_ref, yl_ref, o_ref, cf_ref, cb_ref, *, unroll):
    cf_ref[...] = jnp.zeros(cf_ref.shape, F32)
    cb_ref[...] = jnp.zeros(cb_ref.shape, F32)
    init = (jnp.zeros((1, kc.shape[1]), F32), jnp.zeros((1, 1), F32))
    c_refs = (cf_ref, cb_ref)
    carries = _mlstm_stream(qc, kc, vc, gc.at[0, 0], oc, onorm, yc_ref, (init, init), o_ref, c_refs, unroll)
    _mlstm_stream(ql, kl, vl, gl.at[0, 0], ol, onorm, yl_ref, carries, o_ref, c_refs, unroll)


def mlstm_mix(p, col_q, col_k, col_v, col_o, gates_c, gates_l, out_norm, *, unroll=2):
    h, dk, dv = H_C, DK_C, DV_C
    assert col_q % dk == 0 and col_k % dk == 0 and col_v % dv == 0 and col_o % dv == 0
    ctx_blk0 = N_LAT // CTX_LEN

    def slab(t, row0, width, col):
        c0 = col // width
        return pl.BlockSpec((t, width), lambda bi, hi: (row0 + bi, c0 + hi))

    def stream_specs(t, row0, gates):
        return [slab(t, row0, dk, col_q), slab(t, row0, dk, col_k), slab(t, row0, dv, col_v), slab(t, row0, dv, col_o),
                pl.BlockSpec((1, 1) + gates.shape[2:], lambda bi, hi: (bi, hi, 0, 0, 0))]

    in_specs = stream_specs(CTX_LEN, ctx_blk0, gates_c) + stream_specs(SEQ, 0, gates_l)
    in_specs.append(pl.BlockSpec((1, 1, dv), lambda bi, hi: (hi, 0, 0)))
    return pl.pallas_call(
        functools.partial(_mlstm_kernel, unroll=unroll),
        grid=(BATCH, h),
        in_specs=in_specs,
        out_specs=[pl.BlockSpec((CTX_LEN, dv), lambda bi, hi: (bi, hi)),
                   pl.BlockSpec((SEQ, dv), lambda bi, hi: (bi, hi))],
        out_shape=[jax.ShapeDtypeStruct((N_CTX, h * dv), BF16), jax.ShapeDtypeStruct((N_LAT, h * dv), BF16)],
        scratch_shapes=[pltpu.VMEM((SEQ, dv), F32), pltpu.VMEM((dk, dv), F32), pltpu.VMEM((dk, dv), F32)],
        compiler_params=_cparams(("parallel", "parallel")),
        name="mlstm_mix",
    )(p, p, p, p, gates_c, p, p, p, p, gates_l, out_norm.reshape(h, 1, dv))


def _axial_rope(n_rows, rot_dim):
    r, col = jnp.meshgrid(jnp.arange(n_rows), jnp.arange(GRID_W), indexing="ij")
    r = r.reshape(-1).astype(F32)
    col = col.reshape(-1).astype(F32)
    n_freq = rot_dim // 4
    inv = ROPE_THETA ** (-jnp.arange(n_freq, dtype=F32) / n_freq)
    ang = jnp.concatenate([r[:, None] * inv, col[:, None] * inv], axis=-1)
    return jnp.cos(ang), jnp.sin(ang)


def _deinterleave(n):
    return np.concatenate([np.arange(0, n, 2), np.arange(1, n, 2)])


def _pad_lanes(a):
    pad = LANES - a.shape[-1]
    return a if pad == 0 else jnp.concatenate([a, jnp.zeros(a.shape[:-1] + (pad,), a.dtype)], axis=-1)


def _rope_tables(n_rows, rot_dim):
    cos, sin = _axial_rope(n_rows, rot_dim)
    return _pad_lanes(jnp.concatenate([cos, cos], axis=1)), _pad_lanes(jnp.concatenate([-sin, sin], axis=1))


def _gate_rows(g, width):
    b, t, h = g[0].shape
    rows = [a.transpose(0, 2, 1).reshape(b, h, t // width, 1, width) for a in g]
    rows += [jnp.zeros_like(rows[0])] * (8 - len(rows))
    return jnp.concatenate(rows, axis=3)


def _streams(a):
    c = a.shape[-1]
    return a[N_LAT:].reshape(BATCH, CTX_LEN, c), a[:N_LAT].reshape(BATCH, SEQ, c)


EV_CQ, EV_CKV, EV_QKV, EV_Z = 0, 1024, 1536, 7680
P_EVEN_MAIN = 9728
OD_MQ, OD_MK, OD_MV, OD_MO, OD_GQ, OD_GK, OD_GV = 0, 1024, 2048, 4096, 6144, 8192, 8704
P_ODD_MAIN = 9216


def _split_even(w_in):
    cq, ckv, kpe, qkv, z, ab = jnp.split(w_in, np.cumsum([1024, 512, 64, 6144, 2048]).tolist(), axis=1)
    kpe = kpe[:, _deinterleave(ROPE_A)]
    return jnp.concatenate([cq, ckv, qkv, z, kpe, ab], axis=1)


def _split_odd(w_in):
    mq, mk, mv, mo, mg, gq, gk, gv = jnp.split(
        w_in, np.cumsum([1024, 1024, 2048, 2048, 16, 2048, 512]).tolist(), axis=1)
    perm = _deinterleave(HD_D)
    per_head = lambda w: w.reshape(w.shape[0], -1, HD_D)[:, :, perm].reshape(w.shape)
    return jnp.concatenate([mq, mk, mv, mo, per_head(gq), per_head(gk), gv, _pad_lanes(mg)], axis=1)


def _mla_weights(w_qb, w_kvb):
    wq = w_qb.reshape(Q_LORA, H_A, NOPE_A + ROPE_A)
    q_pe = _pad_lanes(wq[:, :, NOPE_A:][:, :, _deinterleave(ROPE_A)])
    wq = jnp.concatenate([wq[:, :, :NOPE_A].reshape(Q_LORA, -1), q_pe.reshape(Q_LORA, -1)], axis=1)
    wkv = w_kvb.reshape(KV_LORA, H_A, NOPE_A + V_A)
    wkv = jnp.concatenate([wkv[:, :, :NOPE_A].reshape(KV_LORA, -1), wkv[:, :, NOPE_A:].reshape(KV_LORA, -1)], axis=1)
    return wq, wkv


def _even_mixers(p, pg, need_ctx, rope, w_qb, q_lora_norm, w_kvb, kv_lora_norm, q_head_norm, k_head_norm,
                 gdn_conv, a_log, dt_bias, out_norm):
    qn = rms_rows(p, EV_CQ // Q_LORA, Q_LORA, q_lora_norm)
    kvn = rms_rows(p, EV_CKV // KV_LORA, KV_LORA, kv_lora_norm)
    wq, wkv = _mla_weights(w_qb.astype(BF16), w_kvb.astype(BF16))
    q_all = matmul(qn, wq, tn=1024, out_dtype=BF16)
    kv_all = matmul(kvn, wkv, tn=1024, out_dtype=BF16)
    perm = _deinterleave(ROPE_A)
    parts = [
        QKPart(q_all, 0, kv_all, 0, True, q_head_norm[:NOPE_A], k_head_norm[:NOPE_A], NOPE_A, None),
        QKPart(q_all, H_A, pg, 0, False, _pad_lanes(q_head_norm[NOPE_A:][perm]),
               _pad_lanes(k_head_norm[NOPE_A:][perm]), ROPE_A, rope),
    ]
    a_lat = attention_mix(parts, kv_all, H_A, H_A, H_A, NOPE_A + ROPE_A, "lat")
    def gdn_gates(ab):
        ab = ab.reshape(ab.shape[0], ab.shape[1], 4, H_B)
        g = -jnp.exp(a_log)[None, None] * jax.nn.softplus(ab[:, :, :2] + dt_bias[None, None])
        beta = jax.nn.sigmoid(ab[:, :, 2:])
        return _gate_rows([g[:, :, 0], g[:, :, 1], beta[:, :, 0], beta[:, :, 1]], GROUP)

    ab_c, ab_l = _streams(pg[:, ROPE_A:ROPE_A + 4 * H_B])
    b_ctx, b_lat = gdn_mix(p, EV_QKV, EV_Z, gdn_gates(ab_c), gdn_gates(ab_l), gdn_conv, out_norm)
    if not need_ctx:
        return [a_lat, b_lat]
    a_ctx = attention_mix(parts, kv_all, H_A, H_A, H_A, NOPE_A + ROPE_A, "ctx")
    return [jnp.concatenate([a_lat, a_ctx], axis=0), jnp.concatenate([b_lat, b_ctx], axis=0)]


def _odd_mixers(p, pg, need_ctx, rope, gate_bias, out_norm, q_norm, k_norm):
    def mlstm_gates(g):
        g = g.reshape(g.shape[0], g.shape[1], 4, H_C) + gate_bias
        g = GATE_CAP * jnp.tanh(g / GATE_CAP)
        return _gate_rows([g[:, :, 0], g[:, :, 1], jax.nn.log_sigmoid(g[:, :, 2]), jax.nn.log_sigmoid(g[:, :, 3])],
                          CHUNK)

    mg_c, mg_l = _streams(pg[:, :4 * H_C])
    c_ctx, c_lat = mlstm_mix(p, OD_MQ, OD_MK, OD_MV, OD_MO, mlstm_gates(mg_c), mlstm_gates(mg_l), out_norm)
    perm = _deinterleave(HD_D)
    parts = [QKPart(p, OD_GQ // LANES, p, OD_GK // LANES, True, q_norm[perm], k_norm[perm], HD_D, rope)]
    d_lat = attention_mix(parts, p, OD_GV // LANES, H_D, KV_D, HD_D, "lat")
    if not need_ctx:
        return [c_lat, d_lat]
    d_ctx = attention_mix(parts, p, OD_GV // LANES, H_D, KV_D, HD_D, "ctx")
    return [jnp.concatenate([c_lat, c_ctx], axis=0), jnp.concatenate([d_lat, d_ctx], axis=0)]


def kernel(x, c, ctx, c_ctx, mod_down, mod_up, mod_bias, norm_mix, norm_ffn, ffn_up, ffn_conv, ffn_down,
           ev_w_in, ev_w_out, mla_w_qb, mla_q_lora_norm, mla_w_kvb, mla_kv_lora_norm, mla_q_head_norm,
           mla_k_head_norm, gdn_conv, gdn_a_log, gdn_dt_bias, gdn_out_norm, od_w_in, od_w_out,
           mlstm_gate_bias, mlstm_out_norm, gqa_q_norm, gqa_k_norm):
    n_rows = SEQ // GRID_W
    rope_a = _rope_tables(n_rows, ROPE_A)
    rope_d = _rope_tables(n_rows, HD_D)
    cond = jnp.concatenate([c, c_ctx[None, :], jnp.zeros((MOD_ROWS - BATCH - 1, D_MODEL), F32)], axis=0)
    mods = adaln_all(cond, mod_down, mod_up, mod_bias)
    xs = jnp.concatenate([x.reshape(N_LAT, D_MODEL), ctx.reshape(N_CTX, D_MODEL)], axis=0)
    for layer in range(DEPTH):
        need_ctx = layer < DEPTH - 1
        j = layer // 2
        mod = mods[layer].reshape(MOD_ROWS, 1, 6 * D_MODEL)
        u = norm_mod(xs, norm_mix[layer], mod, 0)
        w_in = (ev_w_in if layer % 2 == 0 else od_w_in)[j].astype(BF16)
        w_in = _split_even(w_in) if layer % 2 == 0 else _split_odd(w_in)
        p, pg = matmul(u, w_in, tn=1024, out_dtype=BF16, f32_tail=True)
        if layer % 2 == 0:
            y = _even_mixers(p, pg, need_ctx, rope_a, mla_w_qb[j], mla_q_lora_norm[j],
                             mla_w_kvb[j], mla_kv_lora_norm[j], mla_q_head_norm[j],
                             mla_k_head_norm[j], gdn_conv[j], gdn_a_log[j], gdn_dt_bias[j], gdn_out_norm[j])
            w_out = ev_w_out[j]
        else:
            y = _odd_mixers(p, pg, need_ctx, rope_d, mlstm_gate_bias[j], mlstm_out_norm[j], gqa_q_norm[j],
                            gqa_k_norm[j])
            w_out = od_w_out[j]
        if not need_ctx:
            xs = xs[:N_LAT]
        xs = matmul(y, w_out.astype(BF16), tn=1024, out_dtype=F32, res=xs, mod=mod, gate_chunk=2)
        u2 = norm_mod(xs, norm_ffn[layer], mod, 3, pad=HALO)
        hmid = ffn_up_proj(u2, ffn_up[layer].astype(BF16), ffn_conv[layer])
        xs = matmul(hmid, ffn_down[layer].astype(BF16), tn=256, out_dtype=F32, res=xs, mod=mod, gate_chunk=5,
                    single_buffer_a=True)
    return xs.reshape(BATCH, SEQ, D_MODEL)
```

```python
import functools
import math
from typing import NamedTuple, Optional

import numpy as np
import jax
import jax.numpy as jnp
from jax import lax
from jax.experimental import pallas as pl
from jax.experimental.pallas import tpu as pltpu

F32 = jnp.float32
BF16 = jnp.bfloat16

D_MODEL = 4096
BATCH = 4
SEQ = 4096
DEPTH = 4
CTX_LEN = 256
GRID_W = 64
MOD_RANK = 256
H_A, Q_LORA, KV_LORA, NOPE_A, ROPE_A, V_A = 16, 1024, 512, 128, 64, 128
H_B, DK_B, DV_B, SHORT_CONV = 16, 128, 128, 3
H_C, DK_C, DV_C, GATE_CAP = 4, 256, 512, 15.0
H_D, KV_D, HD_D = 16, 4, 128
D_FF, FFN_CONV = 11008, 3
CHUNK = 64
ROPE_THETA = 10000.0
EPS = 1e-6

N_LAT = BATCH * SEQ
N_CTX = BATCH * CTX_LEN
MOD_ROWS = 8
LANES = 128

VMEM_LIMIT_V7X = 56 * 1024 * 1024

TM = 1024
TM_NORM = 256
TM_RMS = 512
TQ = 1024
Q_SUB = 256
GROUP = 256


def _cparams(sem):
    return pltpu.CompilerParams(dimension_semantics=sem, vmem_limit_bytes=VMEM_LIMIT_V7X)


def _mod_row(i, tm):
    return jnp.minimum((i * tm) // SEQ, BATCH)


def _adaln_kernel(cond_ref, down_ref, up_ref, bias_ref, o_ref):
    cond = cond_ref[...]
    a = cond * jax.nn.sigmoid(cond)
    t = jnp.dot(a, down_ref[0], preferred_element_type=F32, precision=lax.Precision.HIGHEST)
    o = jnp.dot(t, up_ref[0], preferred_element_type=F32, precision=lax.Precision.HIGHEST)
    o_ref[0] = o + bias_ref[0]


def adaln_all(cond, mod_down, mod_up, mod_bias):
    depth, d, rank = mod_down.shape
    n = mod_up.shape[-1]
    tn = d
    return pl.pallas_call(
        _adaln_kernel,
        grid=(depth, n // tn),
        in_specs=[
            pl.BlockSpec((MOD_ROWS, d), lambda l, j: (0, 0)),
            pl.BlockSpec((1, d, rank), lambda l, j: (l, 0, 0)),
            pl.BlockSpec((1, rank, tn), lambda l, j: (l, 0, j)),
            pl.BlockSpec((1, 1, tn), lambda l, j: (l, 0, j)),
        ],
        out_specs=pl.BlockSpec((1, MOD_ROWS, tn), lambda l, j: (l, 0, j)),
        out_shape=jax.ShapeDtypeStruct((depth, MOD_ROWS, n), F32),
        compiler_params=_cparams(("arbitrary", "arbitrary")),
        name="adaln",
    )(cond, mod_down, mod_up, mod_bias.reshape(depth, 1, n))


def _norm_mod_kernel(x_ref, w_ref, sh_ref, sc_ref, o_ref):
    x = x_ref[...]
    y = x * lax.rsqrt(jnp.mean(x * x, axis=-1, keepdims=True) + EPS)
    y = y * w_ref[...]
    o_ref[...] = (y * (1.0 + sc_ref[0]) + sh_ref[0]).astype(o_ref.dtype)


def norm_mod(xs, w, mod, shift_chunk, pad=0):
    m, d = xs.shape
    tm = TM_NORM
    out = pl.pallas_call(
        _norm_mod_kernel,
        grid=(m // tm,),
        in_specs=[
            pl.BlockSpec((tm, d), lambda i: (i, 0)),
            pl.BlockSpec((1, d), lambda i: (0, 0)),
            pl.BlockSpec((1, 1, d), lambda i: (_mod_row(i, tm), 0, shift_chunk)),
            pl.BlockSpec((1, 1, d), lambda i: (_mod_row(i, tm), 0, shift_chunk + 1)),
        ],
        out_specs=pl.BlockSpec((pl.Element(tm), pl.Element(d)),
                               lambda i: (pl.multiple_of(pad + i * tm, math.gcd(pad, tm) or tm), 0)),
        out_shape=jax.ShapeDtypeStruct((m + 2 * pad, d), BF16),
        compiler_params=_cparams(("parallel",)),
        name="norm_mod",
    )(xs, w.reshape(1, d), mod, mod)
    if pad:
        zeros = jnp.zeros((pad, d), BF16)
        out = lax.dynamic_update_slice(out, zeros, (0, 0))
        out = lax.dynamic_update_slice(out, zeros, (m + pad, 0))
    return out


def _rms_rows_kernel(x_ref, w_ref, o_ref):
    x = x_ref[...].astype(F32)
    y = x * lax.rsqrt(jnp.mean(x * x, axis=-1, keepdims=True) + EPS)
    o_ref[...] = (y * w_ref[...]).astype(o_ref.dtype)


def rms_rows(p, col_block, width, w):
    m = p.shape[0]
    tm = TM_RMS
    return pl.pallas_call(
        _rms_rows_kernel,
        grid=(m // tm,),
        in_specs=[
            pl.BlockSpec((tm, width), lambda i: (i, col_block)),
            pl.BlockSpec((1, width), lambda i: (0, 0)),
        ],
        out_specs=pl.BlockSpec((tm, width), lambda i: (i, 0)),
        out_shape=jax.ShapeDtypeStruct((m, width), BF16),
        compiler_params=_cparams(("parallel",)),
        name="rms_rows",
    )(p, w.reshape(1, width))


def _mm_kernel(*refs, n_parts, has_res, tail_off, n_lat_tiles):
    split = n_lat_tiles is not None
    n_a = n_parts * (2 if split else 1)
    a_refs, c_refs = refs[:n_parts], refs[n_parts:n_a]
    w_refs, rest = refs[n_a:n_a + n_parts], refs[n_a + n_parts:]

    def run(srcs):
        acc = jnp.dot(srcs[0][...], w_refs[0][...], preferred_element_type=F32)
        for a_ref, w_ref in zip(srcs[1:], w_refs[1:]):
            acc += jnp.dot(a_ref[...], w_ref[...], preferred_element_type=F32)
        if has_res:
            res_ref, gate_ref, o_ref = rest
            acc = res_ref[...] + gate_ref[0] * acc
        elif tail_off is not None:
            o_ref, tail_ref = rest

            @pl.when(pl.program_id(1) == pl.num_programs(1) - 1)
            def _():
                tail_ref[...] = acc[:, tail_off:tail_off + LANES]
        else:
            (o_ref,) = rest
        o_ref[...] = acc.astype(o_ref.dtype)

    if split:
        is_ctx = pl.program_id(0) >= n_lat_tiles
        pl.when(jnp.logical_not(is_ctx))(lambda: run(a_refs))
        pl.when(is_ctx)(lambda: run(c_refs))
    else:
        run(a_refs)


def matmul(a_parts, w, *, tn, out_dtype, res=None, mod=None, gate_chunk=None, single_buffer_a=False,
           f32_tail=False, ctx_parts=None):
    if not isinstance(a_parts, (list, tuple)):
        a_parts = [a_parts]
    n_parts = len(a_parts)
    m_lat, kp = a_parts[0].shape
    n = w.shape[1]
    tm, tn = TM, min(tn, n)
    assert m_lat % tm == 0 and w.shape[0] == kp * n_parts and all(a.shape == (m_lat, kp) for a in a_parts)
    has_res = res is not None
    nj = pl.cdiv(n, tn)
    a_mode = dict(pipeline_mode=pl.Buffered(1)) if single_buffer_a else {}
    n_lat_tiles = None
    m = m_lat
    if ctx_parts is None:
        in_specs = [pl.BlockSpec((tm, kp), lambda i, j: (i, 0), **a_mode) for _ in a_parts]
        args = list(a_parts)
    else:
        n_lat_tiles = m_lat // tm
        m_ctx = ctx_parts[0].shape[0]
        assert m_ctx % tm == 0 and all(c.shape == (m_ctx, kp) for c in ctx_parts) and len(ctx_parts) == n_parts
        m = m_lat + m_ctx
        in_specs = [pl.BlockSpec((tm, kp), lambda i, j: (jnp.minimum(i, n_lat_tiles - 1), 0)) for _ in a_parts]
        in_specs += [pl.BlockSpec((tm, kp), lambda i, j: (jnp.maximum(i - n_lat_tiles, 0), 0),
                                  pipeline_mode=pl.Buffered(1)) for _ in ctx_parts]
        args = list(a_parts) + list(ctx_parts)
    in_specs += [pl.BlockSpec((kp, tn), functools.partial(lambda i, j, part: (part, j), part=part))
                 for part in range(n_parts)]
    args += [w] * n_parts
    out_specs = pl.BlockSpec((tm, tn), lambda i, j: (i, j))
    out_shape = jax.ShapeDtypeStruct((m, n), out_dtype)
    tail_off = None
    if has_res:
        assert n % tn == 0 and mod.shape[-1] % tn == 0 and not f32_tail
        gate_off = gate_chunk * (n // tn)
        in_specs += [
            pl.BlockSpec((tm, tn), lambda i, j: (i, j)),
            pl.BlockSpec((1, 1, tn), lambda i, j: (_mod_row(i, tm), 0, gate_off + j)),
        ]
        args += [res, mod]
    elif f32_tail:
        tail_off = n - LANES - (nj - 1) * tn
        assert 0 <= tail_off and tail_off % LANES == 0
        out_specs = [out_specs, pl.BlockSpec((tm, LANES), lambda i, j: (i, 0))]
        out_shape = [out_shape, jax.ShapeDtypeStruct((m, LANES), F32)]
    return pl.pallas_call(
        functools.partial(_mm_kernel, n_parts=n_parts, has_res=has_res, tail_off=tail_off, n_lat_tiles=n_lat_tiles),
        grid=(m // tm, nj),
        in_specs=in_specs,
        out_specs=out_specs,
        out_shape=out_shape,
        compiler_params=_cparams(("parallel", "arbitrary")),
        name="matmul",
    )(*args)


HALO = 16


def _seq_pos(i, tm):
    r = i * tm + lax.broadcasted_iota(jnp.int32, (tm, 1), 0)
    pos = jnp.where(r < N_LAT, r & (SEQ - 1), (r - N_LAT) & (CTX_LEN - 1))
    last = jnp.where(r < N_LAT, SEQ - 1, CTX_LEN - 1)
    return pos == 0, pos == last


def _conv3_rows(h, h_prev_row, h_next_row, cw, first, last):
    tm = h.shape[0]
    t = lax.broadcasted_iota(jnp.int32, (tm, 1), 0)
    dn = jnp.where(t == 0, h_prev_row, pltpu.roll(h, 1, 0))
    dn = jnp.where(first, 0.0, dn)
    up = jnp.where(t == tm - 1, h_next_row, pltpu.roll(h, tm - 1, 0))
    up = jnp.where(last, 0.0, up)
    return cw[0:1, :] * dn + cw[1:2, :] * h + cw[2:3, :] * up


def _ffn_up_kernel(a_ref, wg_ref, wu_ref, cg_ref, cu_ref, o_ref, *, tm):
    i = pl.program_id(0)
    first, last = _seq_pos(i, tm)
    a = a_ref[...]
    ext = tm + 2 * HALO
    mid = slice(HALO, HALO + tm)

    def branch(w_ref, c_ref):
        h = jnp.dot(a, w_ref[...], preferred_element_type=F32)
        cw = c_ref[...]
        dn = jnp.where(first, 0.0, pltpu.roll(h, 1, 0)[mid])
        up = jnp.where(last, 0.0, pltpu.roll(h, ext - 1, 0)[mid])
        return cw[0:1, :] * dn + cw[1:2, :] * h[mid] + cw[2:3, :] * up

    g = branch(wg_ref, cg_ref)
    u = branch(wu_ref, cu_ref)
    o_ref[...] = (g * jax.nn.sigmoid(g) * u).astype(o_ref.dtype)


def ffn_up_proj(u2p, w_up, conv_w, *, tn=256):
    m, d = u2p.shape[0] - 2 * HALO, u2p.shape[1]
    tm = TM
    dff = w_up.shape[1] // 2
    nj = dff // tn
    assert dff % tn == 0 and m % tm == 0
    return pl.pallas_call(
        functools.partial(_ffn_up_kernel, tm=tm),
        grid=(m // tm, nj),
        in_specs=[
            pl.BlockSpec((pl.Element(tm + 2 * HALO), pl.Element(d)), lambda i, j: (i * tm, 0)),
            pl.BlockSpec((d, tn), lambda i, j: (0, j)),
            pl.BlockSpec((d, tn), lambda i, j: (0, j + nj)),
            pl.BlockSpec((FFN_CONV, tn), lambda i, j: (0, j)),
            pl.BlockSpec((FFN_CONV, tn), lambda i, j: (0, j + nj)),
        ],
        out_specs=pl.BlockSpec((tm, tn), lambda i, j: (i, j)),
        out_shape=jax.ShapeDtypeStruct((m, dff), BF16),
        compiler_params=_cparams(("parallel", "arbitrary")),
        name="ffn_up",
    )(u2p, w_up, w_up, conv_w, conv_w)


class QKPart(NamedTuple):
    q_arr: jax.Array
    q_col: int
    k_arr: jax.Array
    k_col: int
    k_per_head: bool
    wq: jax.Array
    wk: jax.Array
    n_valid: int
    tabs: Optional[tuple]


def _swap_halves(y, n_valid):
    half = n_valid // 2
    if n_valid == LANES:
        return pltpu.roll(y, half, 1)
    lane = lax.broadcasted_iota(jnp.int32, (1, LANES), 1)
    return jnp.where(lane < half, pltpu.roll(y, LANES - half, 1), pltpu.roll(y, half, 1))


def _norm_rope(x, w, n_valid, tabs):
    if n_valid < LANES:
        lane = lax.broadcasted_iota(jnp.int32, (1, LANES), 1)
        x = jnp.where(lane < n_valid, x, 0.0)
    y = x * lax.rsqrt(jnp.sum(x * x, axis=-1, keepdims=True) * (1.0 / n_valid) + EPS) * w
    if tabs is not None:
        y = y * tabs[0] + _swap_halves(y, n_valid) * tabs[1]
    return y


def _attn_kernel(*refs, n_parts, n_valid, roped, segs, tq, scale):
    n_seg = len(segs)
    it = iter(refs)
    q_refs = [next(it) for _ in range(n_parts)]
    k_refs = [[next(it) for _ in range(n_parts)] for _ in range(n_seg)]
    v_refs = [next(it) for _ in range(n_seg)]
    wq_refs = [next(it) for _ in range(n_parts)]
    wk_refs = [next(it) for _ in range(n_parts)]
    tab_refs = [(next(it), next(it)) if roped[p] else None for p in range(n_parts)]
    o_ref = next(it)
    k_scr = [next(it) for _ in range(n_seg)]
    v_scr = [next(it) for _ in range(n_seg)]
    t = pl.program_id(3)

    @pl.when(jnp.logical_and(pl.program_id(2) == 0, t == 0))
    def _():
        for si, seg in enumerate(segs):
            for p in range(n_parts):
                tabs = None
                if seg == "lat" and roped[p]:
                    tabs = (tab_refs[p][0][...], tab_refs[p][1][...])
                y = _norm_rope(k_refs[si][p][...].astype(F32), wk_refs[p][...], n_valid[p], tabs)
                k_scr[si][:, p * LANES:(p + 1) * LANES] = y.astype(BF16)
            rows = v_scr[si].shape[0]
            lane = lax.broadcasted_iota(jnp.int32, (rows, LANES), 1)
            v_scr[si][:, :LANES] = v_refs[si][...]
            v_scr[si][:, LANES:] = jnp.where(lane == 0, 1.0, 0.0).astype(BF16)

    parts = []
    for p in range(n_parts):
        tabs = None
        if roped[p]:
            r = pl.ds(pl.multiple_of(t * tq, tq), tq)
            tabs = (tab_refs[p][0][r, :], tab_refs[p][1][r, :])
        parts.append(_norm_rope(q_refs[p][...].astype(F32), wq_refs[p][...], n_valid[p], tabs) * scale)
    q = jnp.concatenate(parts, axis=1).astype(BF16)
    dn_t = (((1,), (1,)), ((), ()))
    sub = min(tq, Q_SUB)
    for r0 in range(0, tq, sub):
        qs = q[r0:r0 + sub]
        ss = [lax.dot_general(qs, k_scr[si][...], dn_t, preferred_element_type=F32) for si in range(n_seg)]
        m = jnp.max(ss[0], axis=-1, keepdims=True)
        for s in ss[1:]:
            m = jnp.maximum(m, jnp.max(s, axis=-1, keepdims=True))
        o_ext = None
        for si, s in enumerate(ss):
            pv = jnp.dot(jnp.exp((s - m).astype(BF16)), v_scr[si][...], preferred_element_type=F32)
            o_ext = pv if o_ext is None else o_ext + pv
        o_ref[r0:r0 + sub, :] = (o_ext[:, :LANES] / o_ext[:, LANES:LANES + 1]).astype(o_ref.dtype)


def attention_mix(parts, v_arr, v_col, n_heads, n_kv, dk, q_stream):
    g = n_heads // n_kv
    ctx_blk0 = N_LAT // CTX_LEN
    lat = q_stream == "lat"
    tq = min(TQ, SEQ) if lat else CTX_LEN
    nt = SEQ // tq if lat else 1
    segs = ("ctx", "lat") if lat else ("ctx",)
    roped = tuple(lat and part.tabs is not None for part in parts)
    seg_rows = {"ctx": CTX_LEN, "lat": SEQ}
    seg_blk0 = {"ctx": ctx_blk0, "lat": 0}
    q_blk0 = 0 if lat else ctx_blk0

    def slab(rows, blk0, col, per_head):
        return pl.BlockSpec((rows, LANES), lambda b, hk, gi, t: (blk0 + b, col + (hk if per_head else 0)))

    in_specs, args = [], []
    for part in parts:
        in_specs.append(pl.BlockSpec(
            (tq, LANES), functools.partial(lambda b, hk, gi, t, col: (q_blk0 + b * nt + t, col + hk * g + gi),
                                           col=part.q_col)))
        args.append(part.q_arr)
    for seg in segs:
        for part in parts:
            in_specs.append(slab(seg_rows[seg], seg_blk0[seg], part.k_col, part.k_per_head))
            args.append(part.k_arr)
    for seg in segs:
        in_specs.append(slab(seg_rows[seg], seg_blk0[seg], v_col, True))
        args.append(v_arr)
    const = lambda b, hk, gi, t: (0, 0)
    for w in [part.wq for part in parts] + [part.wk for part in parts]:
        in_specs.append(pl.BlockSpec((1, LANES), const))
        args.append(w.reshape(1, LANES))
    for part, r in zip(parts, roped):
        if r:
            in_specs += [pl.BlockSpec((SEQ, LANES), const)] * 2
            args += list(part.tabs)
    n_parts = len(parts)
    scratch = [pltpu.VMEM((seg_rows[seg], n_parts * LANES), BF16) for seg in segs]
    scratch += [pltpu.VMEM((seg_rows[seg], 2 * LANES), BF16) for seg in segs]
    q_rows = N_LAT if lat else N_CTX
    return pl.pallas_call(
        functools.partial(_attn_kernel, n_parts=n_parts, n_valid=tuple(part.n_valid for part in parts), roped=roped,
                          segs=segs, tq=tq, scale=dk ** -0.5),
        grid=(BATCH, n_kv, g, nt),
        in_specs=in_specs,
        out_specs=pl.BlockSpec((tq, LANES), lambda b, hk, gi, t: (b * nt + t, hk * g + gi)),
        out_shape=jax.ShapeDtypeStruct((q_rows, n_heads * LANES), BF16),
        scratch_shapes=scratch,
        compiler_params=_cparams(("arbitrary",) * 4),
        name="attention",
    )(*args)


def _col_from_row(row, eye):
    return jnp.sum(jnp.where(eye, row, 0.0), axis=1, keepdims=True)


def _rows(idx, size):
    return pl.ds(pl.multiple_of(idx * size, size), size)


def _seq_conv_silu(x, cw):
    zero = jnp.zeros((1, x.shape[1]), F32)
    t = lax.broadcasted_iota(jnp.int32, (x.shape[0], 1), 0)
    c = _conv3_rows(x, zero, zero, cw, t == 0, t == x.shape[0] - 1)
    return c * jax.nn.sigmoid(c)


N_LEVELS = int(math.log2(CHUNK))
MASK_SAME, MASK_EYE, MASK_BE = 0, 1, 2
MASK_LEVEL = 4
N_MASKS = MASK_LEVEL + 2 * N_LEVELS


def _gdn_init_masks(mask_ref):
    n = GROUP
    ii = lax.broadcasted_iota(jnp.int32, (n, n), 0)
    jj = lax.broadcasted_iota(jnp.int32, (n, n), 1)
    same = (ii // CHUNK) == (jj // CHUNK)

    def put(idx, m):
        mask_ref[idx] = jnp.where(m, 1.0, 0.0)

    put(MASK_SAME, same)
    put(MASK_EYE, ii == jj)
    put(MASK_BE, jnp.logical_and(same, jj <= ii))
    put(MASK_BE + 1, jnp.logical_and(same, jj >= ii))
    for level in range(N_LEVELS):
        s = 1 << level
        blk = (ii // (2 * s)) == (jj // (2 * s))
        ih = (ii // s) & 1
        jh = (jj // s) & 1
        put(MASK_LEVEL + level, jnp.logical_and(blk, jnp.logical_and(ih == 1, jh == 0)))
        put(MASK_LEVEL + N_LEVELS + level, jnp.logical_and(blk, jnp.logical_and(ih == 0, jh == 1)))


def _gdn_prep_dir(q, k, v, gates, d, mask_ref):
    def mask(idx):
        return mask_ref[idx] > 0.5

    eye = mask(MASK_EYE)
    same = mask(MASK_SAME)
    be = mask(MASK_BE + d)
    be_t = mask(MASK_BE + 1 - d)
    k16 = k.astype(BF16)
    dn_t = (((1,), (1,)), ((), ()))
    kq = lax.dot_general(jnp.concatenate([k16, q.astype(BF16)], axis=0), k16, dn_t, preferred_element_type=F32)
    kk, qk0 = kq[:k.shape[0]], kq[k.shape[0]:]
    g_row, b_row = gates[d:d + 1, :], gates[2 + d:3 + d, :]
    g_col = _col_from_row(g_row, eye)
    b_col = _col_from_row(b_row, eye)
    gc_col = jnp.sum(jnp.where(be, g_row, 0.0), axis=1, keepdims=True)
    gc_row = jnp.sum(jnp.where(be_t, g_col, 0.0), axis=0, keepdims=True)
    tot_col = jnp.sum(jnp.where(same, g_row, 0.0), axis=1, keepdims=True)
    decay = jnp.where(be, jnp.exp(jnp.where(be, gc_col - gc_row, 0.0)), 0.0)
    mm = jnp.where(jnp.logical_and(be, jnp.logical_not(eye)), b_col * kk * decay, 0.0)
    qk = qk0 * decay
    t_inv = jnp.where(eye, 1.0, 0.0) - jnp.where(mask(MASK_LEVEL + d * N_LEVELS), mm, 0.0)
    for level in range(1, N_LEVELS):
        mc = jnp.where(mask(MASK_LEVEL + d * N_LEVELS + level), mm, 0.0)
        tb = t_inv.astype(BF16)
        y = jnp.dot(tb, mc.astype(BF16), preferred_element_type=F32)
        t_inv = t_inv - jnp.dot(y.astype(BF16), tb, preferred_element_type=F32)
    egc = jnp.exp(gc_col)
    kb = k * b_col
    rhs = jnp.concatenate([v * b_col, kb * egc], axis=1).astype(BF16)
    uw = jnp.dot(t_inv.astype(BF16), rhs, preferred_element_type=F32)
    dv = v.shape[1]
    return uw[:, :dv], uw[:, dv:], q * egc, k * jnp.exp(tot_col - gc_col), qk, jnp.exp(tot_col)


def _gdn_stream(x_refs, cw_refs, gates_ref, z_ref, onorm_ref, y_ref, states, scr):
    q_ref, k_ref, v_ref = x_refs
    mask_ref, qs_ref, ks_ref, vs_ref, o_ref = scr[:5]
    slots = (scr[5:9], scr[9:13])
    t_len = q_ref.shape[0]
    n_groups = t_len // GROUP
    cpg = GROUP // CHUNK
    dk = q_ref.shape[1]

    def l2(x):
        return x * lax.rsqrt(jnp.sum(x * x, axis=-1, keepdims=True) + EPS)

    rows_all = pl.ds(0, t_len)
    qs_ref[rows_all, :] = l2(_seq_conv_silu(q_ref[...].astype(F32), cw_refs[0][...])) * dk ** -0.5
    ks_ref[rows_all, :] = l2(_seq_conv_silu(k_ref[...].astype(F32), cw_refs[1][...]))
    vs_ref[rows_all, :] = _seq_conv_silu(v_ref[...].astype(F32), cw_refs[2][...])

    def load_group(g):
        r = _rows(g, GROUP)
        return qs_ref[r, :], ks_ref[r, :], vs_ref[r, :], gates_ref[g]

    def prep(slot, d, inputs):
        u_ref, wq_ref, lt_ref, gl_ref = slot
        u, w, qg, kg, qk, glast = _gdn_prep_dir(*inputs, d, mask_ref)
        u_ref[d] = u
        kg_t = kg.T
        for c in range(cpg):
            rc = slice(c * CHUNK, (c + 1) * CHUNK)
            wq_ref[d, c] = jnp.concatenate([w[rc], qg[rc]], axis=0).astype(BF16)
            lt_ref[d, c] = jnp.concatenate([kg_t[:, rc], qk[rc, rc]], axis=0).astype(BF16)
            gl_ref[d, c] = jnp.broadcast_to(glast[c * CHUNK:c * CHUNK + 8, :], (8, LANES))

    def scan_group(slot, d, state):
        u_ref, wq_ref, lt_ref, gl_ref = slot
        u = u_ref[d]
        order = range(cpg) if d == 0 else range(cpg - 1, -1, -1)
        o_parts = [None] * cpg
        for c in order:
            rc = slice(c * CHUNK, (c + 1) * CHUNK)
            ws = jnp.dot(wq_ref[d, c], state.astype(BF16), preferred_element_type=F32)
            v_new = u[rc] - ws[:CHUNK]
            r = jnp.dot(lt_ref[d, c], v_new.astype(BF16), preferred_element_type=F32)
            state = state * gl_ref[d, c][0:1, :] + r[:dk]
            o_parts[c] = ws[CHUNK:] + r[dk:]
        return jnp.concatenate(o_parts, axis=0), state

    def step(t, carry, cur, nxt, accumulate, prefetch):
        s_f, s_b = carry
        g_b = n_groups - 1 - t
        if prefetch:
            in_f, in_b = load_group(t + 1), load_group(g_b - 1)
            prep(nxt, 0, in_f)
            prep(nxt, 1, in_b)
        o_f, s_f = scan_group(cur, 0, s_f)
        o_b, s_b = scan_group(cur, 1, s_b)
        if accumulate:
            o_ref[_rows(t, GROUP), :] += o_f
            o_ref[_rows(g_b, GROUP), :] += o_b
        else:
            o_ref[_rows(t, GROUP), :] = o_f
            o_ref[_rows(g_b, GROUP), :] = o_b
        return s_f, s_b

    in_f, in_b = load_group(0), load_group(n_groups - 1)
    prep(slots[0], 0, in_f)
    prep(slots[0], 1, in_b)
    if n_groups == 1:
        o_f, s_f = scan_group(slots[0], 0, states[0])
        o_b, s_b = scan_group(slots[0], 1, states[1])
        o_ref[pl.ds(0, GROUP), :] = o_f + o_b
        states = (s_f, s_b)
    else:
        n_pairs = n_groups // 2
        assert n_groups == 2 * n_pairs and n_pairs % 2 == 0

        def pair(p, carry, accumulate, last=False):
            carry = step(2 * p, carry, slots[0], slots[1], accumulate, True)
            return step(2 * p + 1, carry, slots[1], slots[0], accumulate, not last)

        states = lax.fori_loop(0, n_pairs // 2, lambda p, c: pair(p, c, False), states)
        states = lax.fori_loop(n_pairs // 2, n_pairs - 1, lambda p, c: pair(p, c, True), states)
        states = pair(n_pairs - 1, states, True, last=True)

    o = o_ref[rows_all, :]
    o = o * lax.rsqrt(jnp.mean(o * o, axis=-1, keepdims=True) + EPS) * onorm_ref[...]
    z = z_ref[...].astype(F32)
    y_ref[...] = (o * (z * jax.nn.sigmoid(z))).astype(y_ref.dtype)
    return states


def _gdn_kernel(qc, kc, vc, zc, gc, ql, kl, vl, zl, gl, cwq, cwk, cwv, onorm, yc_ref, yl_ref, *scr):
    @pl.when(jnp.logical_and(pl.program_id(0) == 0, pl.program_id(1) == 0))
    def _():
        _gdn_init_masks(scr[0])

    cws = (cwq, cwk, cwv)
    zero = jnp.zeros((kc.shape[1], vc.shape[1]), F32)
    states = _gdn_stream((qc, kc, vc), cws, gc.at[0, 0], zc, onorm, yc_ref, (zero, zero), scr)
    _gdn_stream((ql, kl, vl), cws, gl.at[0, 0], zl, onorm, yl_ref, states, scr)


def gdn_mix(p, col_q, col_z, gates_c, gates_l, conv_w, out_norm):
    h, dk, dv = H_B, DK_B, DV_B
    assert dk == LANES and dv == LANES and col_q % LANES == 0 and col_z % LANES == 0
    cq, cz = col_q // LANES, col_z // LANES
    ctx_blk0 = N_LAT // CTX_LEN
    cpg = GROUP // CHUNK

    def slab(t, row0, col0):
        return pl.BlockSpec((t, LANES), lambda bi, hi: (row0 + bi, col0 + hi))

    def stream_specs(t, row0, gates):
        return [slab(t, row0, cq), slab(t, row0, cq + h), slab(t, row0, cq + 2 * h), slab(t, row0, cz),
                pl.BlockSpec((1, 1) + gates.shape[2:], lambda bi, hi: (bi, hi, 0, 0, 0))]

    in_specs = stream_specs(CTX_LEN, ctx_blk0, gates_c) + stream_specs(SEQ, 0, gates_l)
    in_specs += [pl.BlockSpec((SHORT_CONV, LANES), functools.partial(lambda bi, hi, off: (0, off + hi), off=off))
                 for off in (0, h, 2 * h)]
    in_specs.append(pl.BlockSpec((1, dv), lambda bi, hi: (0, 0)))
    slot = [
        pltpu.VMEM((2, GROUP, dv), F32),
        pltpu.VMEM((2, cpg, 2 * CHUNK, dk), BF16),
        pltpu.VMEM((2, cpg, dk + CHUNK, CHUNK), BF16),
        pltpu.VMEM((2, cpg, 8, LANES), F32),
    ]
    scratch = [
        pltpu.VMEM((N_MASKS, GROUP, GROUP), F32),
        pltpu.VMEM((SEQ, dk), F32), pltpu.VMEM((SEQ, dk), F32), pltpu.VMEM((SEQ, dv), F32),
        pltpu.VMEM((SEQ, dv), F32),
    ] + slot + slot
    return pl.pallas_call(
        _gdn_kernel,
        grid=(BATCH, h),
        in_specs=in_specs,
        out_specs=[pl.BlockSpec((CTX_LEN, dv), lambda bi, hi: (bi, hi)),
                   pl.BlockSpec((SEQ, dv), lambda bi, hi: (bi, hi))],
        out_shape=[jax.ShapeDtypeStruct((N_CTX, h * dv), BF16), jax.ShapeDtypeStruct((N_LAT, h * dv), BF16)],
        scratch_shapes=scratch,
        compiler_params=_cparams(("arbitrary", "arbitrary")),
        name="gdn_mix",
    )(p, p, p, p, gates_c, p, p, p, p, gates_l, conv_w, conv_w, conv_w, out_norm.reshape(1, dv))


def _tri_masks(n, rev):
    ii = lax.broadcasted_iota(jnp.int32, (n, n), 0)
    jj = lax.broadcasted_iota(jnp.int32, (n, n), 1)
    eye = ii == jj
    if rev:
        be, be_t = jj >= ii, ii >= jj
    else:
        be, be_t = jj <= ii, ii <= jj
    return eye, be, be_t


def _mlstm_chunk(q, k, v, i_row, f_row, carry, c_ref, rev):
    n_vec, m = carry
    n = q.shape[0]
    eye, be, be_t = _tri_masks(n, rev)
    f_col = _col_from_row(f_row, eye)
    i_col = _col_from_row(i_row, eye)
    b_col = jnp.sum(jnp.where(be, f_row, 0.0), axis=1, keepdims=True)
    b_row = jnp.sum(jnp.where(be_t, f_col, 0.0), axis=0, keepdims=True)
    b_tot = jnp.sum(f_row, axis=1, keepdims=True)
    d = jnp.where(be, b_col - b_row + i_row, -jnp.inf)
    inter = b_col + m
    m_t = jnp.maximum(inter, jnp.max(d, axis=1, keepdims=True))
    a_inter = jnp.exp(inter - m_t)
    qb = q.astype(BF16)
    kbf = k.astype(BF16)
    s = lax.dot_general(qb, kbf, (((1,), (1,)), ((), ())), preferred_element_type=F32)
    p = jnp.exp(d - m_t) * s
    c_mat = c_ref[...]
    num = a_inter * jnp.dot(qb, c_mat.astype(BF16), preferred_element_type=F32) + jnp.dot(
        p.astype(BF16), v.astype(BF16), preferred_element_type=F32)
    den = a_inter * jnp.sum(q * n_vec, axis=1, keepdims=True) + jnp.sum(p, axis=1, keepdims=True)
    h = num / jnp.maximum(jnp.abs(den), jnp.exp(-m_t))
    w_end_row = b_tot - b_row + i_row
    w_end_col = b_tot - b_col + i_col
    m_new = jnp.maximum(b_tot + m, jnp.max(w_end_row, axis=1, keepdims=True))
    g_old = jnp.exp(b_tot + m - m_new)
    gs_col = jnp.exp(w_end_col - m_new)
    upd = lax.dot_general(kbf, (gs_col * v).astype(BF16), (((0,), (0,)), ((), ())), preferred_element_type=F32)
    c_ref[...] = g_old * c_mat + upd
    n_new = g_old * n_vec + jnp.sum(gs_col * k, axis=0, keepdims=True)
    return h, (n_new, m_new)


def _mlstm_stream(q_ref, k_ref, v_ref, gates_ref, opre_ref, onorm_ref, y_ref, carries, o_ref, c_refs, unroll):
    t_len = q_ref.shape[0]
    n_chunks = t_len // CHUNK
    half = n_chunks // 2
    assert n_chunks == 2 * half
    scale = q_ref.shape[1] ** -0.5

    def chunk(c, carry, rev):
        r = _rows(c, CHUNK)
        gates = gates_ref[c]
        d = 1 if rev else 0
        return _mlstm_chunk(q_ref[r, :].astype(F32) * scale, k_ref[r, :].astype(F32), v_ref[r, :],
                            gates[d:d + 1, :], gates[2 + d:3 + d, :], carry, c_refs[d], rev)

    def step(t, carry, accumulate):
        cf, cb = carry
        c_b = n_chunks - 1 - t
        o_f, cf = chunk(t, cf, False)
        o_b, cb = chunk(c_b, cb, True)
        if accumulate:
            o_ref[_rows(t, CHUNK), :] += o_f
            o_ref[_rows(c_b, CHUNK), :] += o_b
        else:
            o_ref[_rows(t, CHUNK), :] = o_f
            o_ref[_rows(c_b, CHUNK), :] = o_b
        return cf, cb

    carries = lax.fori_loop(0, half, lambda t, c: step(t, c, False), carries, unroll=unroll)
    carries = lax.fori_loop(half, n_chunks, lambda t, c: step(t, c, True), carries, unroll=unroll)
    o = o_ref[pl.ds(0, t_len), :]
    o = o * lax.rsqrt(jnp.mean(o * o, axis=-1, keepdims=True) + EPS) * onorm_ref[0]
    y_ref[...] = (o * jax.nn.sigmoid(opre_ref[...].astype(F32))).astype(y_ref.dtype)
    return carries


def _mlstm_kernel(qc, kc, vc, oc, gc, ql, kl, vl, ol, gl, onorm, yc_ref, yl_ref, o_ref, cf_ref, cb_ref, *, unroll):
    cf_ref[...] = jnp.zeros(cf_ref.shape, F32)
    cb_ref[...] = jnp.zeros(cb_ref.shape, F32)
    init = (jnp.zeros((1, kc.shape[1]), F32), jnp.zeros((1, 1), F32))
    c_refs = (cf_ref, cb_ref)
    carries = _mlstm_stream(qc, kc, vc, gc.at[0, 0], oc, onorm, yc_ref, (init, init), o_ref, c_refs, unroll)
    _mlstm_stream(ql, kl, vl, gl.at[0, 0], ol, onorm, yl_ref, carries, o_ref, c_refs, unroll)


def mlstm_mix(p, col_q, col_k, col_v, col_o, gates_c, gates_l, out_norm, *, unroll=2):
    h, dk, dv = H_C, DK_C, DV_C
    assert col_q % dk == 0 and col_k % dk == 0 and col_v % dv == 0 and col_o % dv == 0
    ctx_blk0 = N_LAT // CTX_LEN

    def slab(t, row0, width, col):
        c0 = col // width
        return pl.BlockSpec((t, width), lambda bi, hi: (row0 + bi, c0 + hi))

    def stream_specs(t, row0, gates):
        return [slab(t, row0, dk, col_q), slab(t, row0, dk, col_k), slab(t, row0, dv, col_v), slab(t, row0, dv, col_o),
                pl.BlockSpec((1, 1) + gates.shape[2:], lambda bi, hi: (bi, hi, 0, 0, 0))]

    in_specs = stream_specs(CTX_LEN, ctx_blk0, gates_c) + stream_specs(SEQ, 0, gates_l)
    in_specs.append(pl.BlockSpec((1, 1, dv), lambda bi, hi: (hi, 0, 0)))
    return pl.pallas_call(
        functools.partial(_mlstm_kernel, unroll=unroll),
        grid=(BATCH, h),
        in_specs=in_specs,
        out_specs=[pl.BlockSpec((CTX_LEN, dv), lambda bi, hi: (bi, hi)),
                   pl.BlockSpec((SEQ, dv), lambda bi, hi: (bi, hi))],
        out_shape=[jax.ShapeDtypeStruct((N_CTX, h * dv), BF16), jax.ShapeDtypeStruct((N_LAT, h * dv), BF16)],
        scratch_shapes=[pltpu.VMEM((SEQ, dv), F32), pltpu.VMEM((dk, dv), F32), pltpu.VMEM((dk, dv), F32)],
        compiler_params=_cparams(("parallel", "parallel")),
        name="mlstm_mix",
    )(p, p, p, p, gates_c, p, p, p, p, gates_l, out_norm.reshape(h, 1, dv))


def _axial_rope(n_rows, rot_dim):
    r, col = jnp.meshgrid(jnp.arange(n_rows), jnp.arange(GRID_W), indexing="ij")
    r = r.reshape(-1).astype(F32)
    col = col.reshape(-1).astype(F32)
    n_freq = rot_dim // 4
    inv = ROPE_THETA ** (-jnp.arange(n_freq, dtype=F32) / n_freq)
    ang = jnp.concatenate([r[:, None] * inv, col[:, None] * inv], axis=-1)
    return jnp.cos(ang), jnp.sin(ang)


def _deinterleave(n):
    return np.concatenate([np.arange(0, n, 2), np.arange(1, n, 2)])


def _pad_lanes(a):
    pad = LANES - a.shape[-1]
    return a if pad == 0 else jnp.concatenate([a, jnp.zeros(a.shape[:-1] + (pad,), a.dtype)], axis=-1)


def _rope_tables(n_rows, rot_dim):
    cos, sin = _axial_rope(n_rows, rot_dim)
    return _pad_lanes(jnp.concatenate([cos, cos], axis=1)), _pad_lanes(jnp.concatenate([-sin, sin], axis=1))


def _gate_rows(g, width):
    b, t, h = g[0].shape
    rows = [a.transpose(0, 2, 1).reshape(b, h, t // width, 1, width) for a in g]
    rows += [jnp.zeros_like(rows[0])] * (8 - len(rows))
    return jnp.concatenate(rows, axis=3)


def _streams(a):
    c = a.shape[-1]
    return a[N_LAT:].reshape(BATCH, CTX_LEN, c), a[:N_LAT].reshape(BATCH, SEQ, c)


EV_CQ, EV_CKV, EV_QKV, EV_Z = 0, 1024, 1536, 7680
P_EVEN_MAIN = 9728
OD_MQ, OD_MK, OD_MV, OD_MO, OD_GQ, OD_GK, OD_GV = 0, 1024, 2048, 4096, 6144, 8192, 8704
P_ODD_MAIN = 9216


def _split_even(w_in):
    cq, ckv, kpe, qkv, z, ab = jnp.split(w_in, np.cumsum([1024, 512, 64, 6144, 2048]).tolist(), axis=1)
    kpe = kpe[:, _deinterleave(ROPE_A)]
    return jnp.concatenate([cq, ckv, qkv, z, kpe, ab], axis=1)


def _split_odd(w_in):
    mq, mk, mv, mo, mg, gq, gk, gv = jnp.split(
        w_in, np.cumsum([1024, 1024, 2048, 2048, 16, 2048, 512]).tolist(), axis=1)
    perm = _deinterleave(HD_D)
    per_head = lambda w: w.reshape(w.shape[0], -1, HD_D)[:, :, perm].reshape(w.shape)
    return jnp.concatenate([mq, mk, mv, mo, per_head(gq), per_head(gk), gv, _pad_lanes(mg)], axis=1)


def _mla_weights(w_qb, w_kvb):
    wq = w_qb.reshape(Q_LORA, H_A, NOPE_A + ROPE_A)
    q_pe = _pad_lanes(wq[:, :, NOPE_A:][:, :, _deinterleave(ROPE_A)])
    wq = jnp.concatenate([wq[:, :, :NOPE_A].reshape(Q_LORA, -1), q_pe.reshape(Q_LORA, -1)], axis=1)
    wkv = w_kvb.reshape(KV_LORA, H_A, NOPE_A + V_A)
    wkv = jnp.concatenate([wkv[:, :, :NOPE_A].reshape(KV_LORA, -1), wkv[:, :, NOPE_A:].reshape(KV_LORA, -1)], axis=1)
    return wq, wkv


def _even_mixers(p, pg, need_ctx, rope, w_qb, q_lora_norm, w_kvb, kv_lora_norm, q_head_norm, k_head_norm,
                 gdn_conv, a_log, dt_bias, out_norm):
    qn = rms_rows(p, EV_CQ // Q_LORA, Q_LORA, q_lora_norm)
    kvn = rms_rows(p, EV_CKV // KV_LORA, KV_LORA, kv_lora_norm)
    wq, wkv = _mla_weights(w_qb.astype(BF16), w_kvb.astype(BF16))
    q_all = matmul(qn, wq, tn=1024, out_dtype=BF16)
    kv_all = matmul(kvn, wkv, tn=1024, out_dtype=BF16)
    perm = _deinterleave(ROPE_A)
    parts = [
        QKPart(q_all, 0, kv_all, 0, True, q_head_norm[:NOPE_A], k_head_norm[:NOPE_A], NOPE_A, None),
        QKPart(q_all, H_A, pg, 0, False, _pad_lanes(q_head_norm[NOPE_A:][perm]),
               _pad_lanes(k_head_norm[NOPE_A:][perm]), ROPE_A, rope),
    ]
    a_lat = attention_mix(parts, kv_all, H_A, H_A, H_A, NOPE_A + ROPE_A, "lat")
    def gdn_gates(ab):
        ab = ab.reshape(ab.shape[0], ab.shape[1], 4, H_B)
        g = -jnp.exp(a_log)[None, None] * jax.nn.softplus(ab[:, :, :2] + dt_bias[None, None])
        beta = jax.nn.sigmoid(ab[:, :, 2:])
        return _gate_rows([g[:, :, 0], g[:, :, 1], beta[:, :, 0], beta[:, :, 1]], GROUP)

    ab_c, ab_l = _streams(pg[:, ROPE_A:ROPE_A + 4 * H_B])
    b_ctx, b_lat = gdn_mix(p, EV_QKV, EV_Z, gdn_gates(ab_c), gdn_gates(ab_l), gdn_conv, out_norm)
    if not need_ctx:
        return [a_lat, b_lat], None
    a_ctx = attention_mix(parts, kv_all, H_A, H_A, H_A, NOPE_A + ROPE_A, "ctx")
    return [a_lat, b_lat], [a_ctx, b_ctx]


def _odd_mixers(p, pg, need_ctx, rope, gate_bias, out_norm, q_norm, k_norm):
    def mlstm_gates(g):
        g = g.reshape(g.shape[0], g.shape[1], 4, H_C) + gate_bias
        g = GATE_CAP * jnp.tanh(g / GATE_CAP)
        return _gate_rows([g[:, :, 0], g[:, :, 1], jax.nn.log_sigmoid(g[:, :, 2]), jax.nn.log_sigmoid(g[:, :, 3])],
                          CHUNK)

    mg_c, mg_l = _streams(pg[:, :4 * H_C])
    c_ctx, c_lat = mlstm_mix(p, OD_MQ, OD_MK, OD_MV, OD_MO, mlstm_gates(mg_c), mlstm_gates(mg_l), out_norm)
    perm = _deinterleave(HD_D)
    parts = [QKPart(p, OD_GQ // LANES, p, OD_GK // LANES, True, q_norm[perm], k_norm[perm], HD_D, rope)]
    d_lat = attention_mix(parts, p, OD_GV // LANES, H_D, KV_D, HD_D, "lat")
    if not need_ctx:
        return [c_lat, d_lat], None
    d_ctx = attention_mix(parts, p, OD_GV // LANES, H_D, KV_D, HD_D, "ctx")
    return [c_lat, d_lat], [c_ctx, d_ctx]


def kernel(x, c, ctx, c_ctx, mod_down, mod_up, mod_bias, norm_mix, norm_ffn, ffn_up, ffn_conv, ffn_down,
           ev_w_in, ev_w_out, mla_w_qb, mla_q_lora_norm, mla_w_kvb, mla_kv_lora_norm, mla_q_head_norm,
           mla_k_head_norm, gdn_conv, gdn_a_log, gdn_dt_bias, gdn_out_norm, od_w_in, od_w_out,
           mlstm_gate_bias, mlstm_out_norm, gqa_q_norm, gqa_k_norm):
    n_rows = SEQ // GRID_W
    rope_a = _rope_tables(n_rows, ROPE_A)
    rope_d = _rope_tables(n_rows, HD_D)
    cond = jnp.concatenate([c, c_ctx[None, :], jnp.zeros((MOD_ROWS - BATCH - 1, D_MODEL), F32)], axis=0)
    mods = adaln_all(cond, mod_down, mod_up, mod_bias)
    xs = jnp.concatenate([x.reshape(N_LAT, D_MODEL), ctx.reshape(N_CTX, D_MODEL)], axis=0)
    for layer in range(DEPTH):
        need_ctx = layer < DEPTH - 1
        j = layer // 2
        mod = mods[layer].reshape(MOD_ROWS, 1, 6 * D_MODEL)
        u = norm_mod(xs, norm_mix[layer], mod, 0)
        w_in = (ev_w_in if layer % 2 == 0 else od_w_in)[j].astype(BF16)
        w_in = _split_even(w_in) if layer % 2 == 0 else _split_odd(w_in)
        p, pg = matmul(u, w_in, tn=1024, out_dtype=BF16, f32_tail=True)
        if layer % 2 == 0:
            y, y_ctx = _even_mixers(p, pg, need_ctx, rope_a, mla_w_qb[j], mla_q_lora_norm[j],
                                    mla_w_kvb[j], mla_kv_lora_norm[j], mla_q_head_norm[j], mla_k_head_norm[j],
                                    gdn_conv[j], gdn_a_log[j], gdn_dt_bias[j], gdn_out_norm[j])
            w_out = ev_w_out[j]
        else:
            y, y_ctx = _odd_mixers(p, pg, need_ctx, rope_d, mlstm_gate_bias[j], mlstm_out_norm[j], gqa_q_norm[j],
                                   gqa_k_norm[j])
            w_out = od_w_out[j]
        if not need_ctx:
            xs = xs[:N_LAT]
        xs = matmul(y, w_out.astype(BF16), tn=512, out_dtype=F32, res=xs, mod=mod, gate_chunk=2,
                    ctx_parts=y_ctx)
        u2 = norm_mod(xs, norm_ffn[layer], mod, 3, pad=HALO)
        hmid = ffn_up_proj(u2, ffn_up[layer].astype(BF16), ffn_conv[layer])
        xs = matmul(hmid, ffn_down[layer].astype(BF16), tn=256, out_dtype=F32, res=xs, mod=mod, gate_chunk=5,
                    single_buffer_a=True)
    return xs.reshape(BATCH, SEQ, D_MODEL)
```

```python
import functools
import math
from typing import NamedTuple, Optional

import numpy as np
import jax
import jax.numpy as jnp
from jax import lax
from jax.experimental import pallas as pl
from jax.experimental.pallas import tpu as pltpu

F32 = jnp.float32
BF16 = jnp.bfloat16

D_MODEL = 4096
BATCH = 4
SEQ = 4096
DEPTH = 4
CTX_LEN = 256
GRID_W = 64
MOD_RANK = 256
H_A, Q_LORA, KV_LORA, NOPE_A, ROPE_A, V_A = 16, 1024, 512, 128, 64, 128
H_B, DK_B, DV_B, SHORT_CONV = 16, 128, 128, 3
H_C, DK_C, DV_C, GATE_CAP = 4, 256, 512, 15.0
H_D, KV_D, HD_D = 16, 4, 128
D_FF, FFN_CONV = 11008, 3
CHUNK = 64
ROPE_THETA = 10000.0
EPS = 1e-6

N_LAT = BATCH * SEQ
N_CTX = BATCH * CTX_LEN
MOD_ROWS = 8
LANES = 128

VMEM_LIMIT_V7X = 56 * 1024 * 1024

TM = 1024
TM_NORM = 256
TM_RMS = 512
TQ = 1024
Q_SUB = 256
GROUP = 256


def _cparams(sem):
    return pltpu.CompilerParams(dimension_semantics=sem, vmem_limit_bytes=VMEM_LIMIT_V7X)


def _mod_row(i, tm):
    return jnp.minimum((i * tm) // SEQ, BATCH)


def _adaln_kernel(cond_ref, down_ref, up_ref, bias_ref, o_ref):
    cond = cond_ref[...]
    a = cond * jax.nn.sigmoid(cond)
    t = jnp.dot(a, down_ref[0], preferred_element_type=F32, precision=lax.Precision.HIGHEST)
    o = jnp.dot(t, up_ref[0], preferred_element_type=F32, precision=lax.Precision.HIGHEST)
    o_ref[0] = o + bias_ref[0]


def adaln_all(cond, mod_down, mod_up, mod_bias):
    depth, d, rank = mod_down.shape
    n = mod_up.shape[-1]
    tn = d
    return pl.pallas_call(
        _adaln_kernel,
        grid=(depth, n // tn),
        in_specs=[
            pl.BlockSpec((MOD_ROWS, d), lambda l, j: (0, 0)),
            pl.BlockSpec((1, d, rank), lambda l, j: (l, 0, 0)),
            pl.BlockSpec((1, rank, tn), lambda l, j: (l, 0, j)),
            pl.BlockSpec((1, 1, tn), lambda l, j: (l, 0, j)),
        ],
        out_specs=pl.BlockSpec((1, MOD_ROWS, tn), lambda l, j: (l, 0, j)),
        out_shape=jax.ShapeDtypeStruct((depth, MOD_ROWS, n), F32),
        compiler_params=_cparams(("arbitrary", "arbitrary")),
        name="adaln",
    )(cond, mod_down, mod_up, mod_bias.reshape(depth, 1, n))


def _norm_mod_kernel(x_ref, w_ref, sh_ref, sc_ref, o_ref):
    x = x_ref[...]
    y = x * lax.rsqrt(jnp.mean(x * x, axis=-1, keepdims=True) + EPS)
    y = y * w_ref[...]
    o_ref[...] = (y * (1.0 + sc_ref[0]) + sh_ref[0]).astype(o_ref.dtype)


def norm_mod(xs, w, mod, shift_chunk, pad=0):
    m, d = xs.shape
    tm = TM_NORM
    out = pl.pallas_call(
        _norm_mod_kernel,
        grid=(m // tm,),
        in_specs=[
            pl.BlockSpec((tm, d), lambda i: (i, 0)),
            pl.BlockSpec((1, d), lambda i: (0, 0)),
            pl.BlockSpec((1, 1, d), lambda i: (_mod_row(i, tm), 0, shift_chunk)),
            pl.BlockSpec((1, 1, d), lambda i: (_mod_row(i, tm), 0, shift_chunk + 1)),
        ],
        out_specs=pl.BlockSpec((pl.Element(tm), pl.Element(d)),
                               lambda i: (pl.multiple_of(pad + i * tm, math.gcd(pad, tm) or tm), 0)),
        out_shape=jax.ShapeDtypeStruct((m + 2 * pad, d), BF16),
        compiler_params=_cparams(("parallel",)),
        name="norm_mod",
    )(xs, w.reshape(1, d), mod, mod)
    if pad:
        zeros = jnp.zeros((pad, d), BF16)
        out = lax.dynamic_update_slice(out, zeros, (0, 0))
        out = lax.dynamic_update_slice(out, zeros, (m + pad, 0))
    return out


def _rms_rows_kernel(x_ref, w_ref, o_ref):
    x = x_ref[...].astype(F32)
    y = x * lax.rsqrt(jnp.mean(x * x, axis=-1, keepdims=True) + EPS)
    o_ref[...] = (y * w_ref[...]).astype(o_ref.dtype)


def rms_rows(p, col_block, width, w):
    m = p.shape[0]
    tm = TM_RMS
    return pl.pallas_call(
        _rms_rows_kernel,
        grid=(m // tm,),
        in_specs=[
            pl.BlockSpec((tm, width), lambda i: (i, col_block)),
            pl.BlockSpec((1, width), lambda i: (0, 0)),
        ],
        out_specs=pl.BlockSpec((tm, width), lambda i: (i, 0)),
        out_shape=jax.ShapeDtypeStruct((m, width), BF16),
        compiler_params=_cparams(("parallel",)),
        name="rms_rows",
    )(p, w.reshape(1, width))


def _mm_kernel(*refs, n_parts, has_res, tail_off, n_lat_tiles):
    split = n_lat_tiles is not None
    n_a = n_parts * (2 if split else 1)
    a_refs, c_refs = refs[:n_parts], refs[n_parts:n_a]
    w_refs, rest = refs[n_a:n_a + n_parts], refs[n_a + n_parts:]

    def run(srcs):
        acc = jnp.dot(srcs[0][...], w_refs[0][...], preferred_element_type=F32)
        for a_ref, w_ref in zip(srcs[1:], w_refs[1:]):
            acc += jnp.dot(a_ref[...], w_ref[...], preferred_element_type=F32)
        if has_res:
            res_ref, gate_ref, o_ref = rest
            acc = res_ref[...] + gate_ref[0] * acc
        elif tail_off is not None:
            o_ref, tail_ref = rest

            @pl.when(pl.program_id(1) == pl.num_programs(1) - 1)
            def _():
                tail_ref[...] = acc[:, tail_off:tail_off + LANES]
        else:
            (o_ref,) = rest
        o_ref[...] = acc.astype(o_ref.dtype)

    if split:
        is_ctx = pl.program_id(0) >= n_lat_tiles
        pl.when(jnp.logical_not(is_ctx))(lambda: run(a_refs))
        pl.when(is_ctx)(lambda: run(c_refs))
    else:
        run(a_refs)


def matmul(a_parts, w, *, tn, out_dtype, res=None, mod=None, gate_chunk=None, single_buffer_a=False,
           f32_tail=False, ctx_parts=None, w_layer=None):
    if not isinstance(a_parts, (list, tuple)):
        a_parts = [a_parts]
    n_parts = len(a_parts)
    m_lat, kp = a_parts[0].shape
    n = w.shape[-1]
    tm, tn = TM, min(tn, n)
    assert m_lat % tm == 0 and w.shape[-2] == kp * n_parts and all(a.shape == (m_lat, kp) for a in a_parts)
    has_res = res is not None
    nj = pl.cdiv(n, tn)
    a_mode = dict(pipeline_mode=pl.Buffered(1)) if single_buffer_a else {}
    n_lat_tiles = None
    m = m_lat
    if ctx_parts is None:
        in_specs = [pl.BlockSpec((tm, kp), lambda i, j: (i, 0), **a_mode) for _ in a_parts]
        args = list(a_parts)
    else:
        n_lat_tiles = m_lat // tm
        m_ctx = ctx_parts[0].shape[0]
        assert m_ctx % tm == 0 and all(c.shape == (m_ctx, kp) for c in ctx_parts) and len(ctx_parts) == n_parts
        m = m_lat + m_ctx
        in_specs = [pl.BlockSpec((tm, kp), lambda i, j: (jnp.minimum(i, n_lat_tiles - 1), 0)) for _ in a_parts]
        in_specs += [pl.BlockSpec((tm, kp), lambda i, j: (jnp.maximum(i - n_lat_tiles, 0), 0),
                                  pipeline_mode=pl.Buffered(1)) for _ in ctx_parts]
        args = list(a_parts) + list(ctx_parts)
    if w_layer is None:
        in_specs += [pl.BlockSpec((kp, tn), functools.partial(lambda i, j, part: (part, j), part=part))
                     for part in range(n_parts)]
    else:
        in_specs += [pl.BlockSpec((None, kp, tn), functools.partial(lambda i, j, part: (w_layer, part, j), part=part))
                     for part in range(n_parts)]
    args += [w] * n_parts
    out_specs = pl.BlockSpec((tm, tn), lambda i, j: (i, j))
    out_shape = jax.ShapeDtypeStruct((m, n), out_dtype)
    tail_off = None
    if has_res:
        assert n % tn == 0 and mod.shape[-1] % tn == 0 and not f32_tail
        gate_off = gate_chunk * (n // tn)
        in_specs += [
            pl.BlockSpec((tm, tn), lambda i, j: (i, j)),
            pl.BlockSpec((1, 1, tn), lambda i, j: (_mod_row(i, tm), 0, gate_off + j)),
        ]
        args += [res, mod]
    elif f32_tail:
        tail_off = n - LANES - (nj - 1) * tn
        assert 0 <= tail_off and tail_off % LANES == 0
        out_specs = [out_specs, pl.BlockSpec((tm, LANES), lambda i, j: (i, 0))]
        out_shape = [out_shape, jax.ShapeDtypeStruct((m, LANES), F32)]
    return pl.pallas_call(
        functools.partial(_mm_kernel, n_parts=n_parts, has_res=has_res, tail_off=tail_off, n_lat_tiles=n_lat_tiles),
        grid=(m // tm, nj),
        in_specs=in_specs,
        out_specs=out_specs,
        out_shape=out_shape,
        compiler_params=_cparams(("parallel", "arbitrary")),
        name="matmul",
    )(*args)


HALO = 16


def _seq_pos(i, tm):
    r = i * tm + lax.broadcasted_iota(jnp.int32, (tm, 1), 0)
    pos = jnp.where(r < N_LAT, r & (SEQ - 1), (r - N_LAT) & (CTX_LEN - 1))
    last = jnp.where(r < N_LAT, SEQ - 1, CTX_LEN - 1)
    return pos == 0, pos == last


def _conv3_rows(h, h_prev_row, h_next_row, cw, first, last):
    tm = h.shape[0]
    t = lax.broadcasted_iota(jnp.int32, (tm, 1), 0)
    dn = jnp.where(t == 0, h_prev_row, pltpu.roll(h, 1, 0))
    dn = jnp.where(first, 0.0, dn)
    up = jnp.where(t == tm - 1, h_next_row, pltpu.roll(h, tm - 1, 0))
    up = jnp.where(last, 0.0, up)
    return cw[0:1, :] * dn + cw[1:2, :] * h + cw[2:3, :] * up


def _ffn_up_kernel(a_ref, wg_ref, wu_ref, cg_ref, cu_ref, o_ref, *, tm):
    i = pl.program_id(0)
    first, last = _seq_pos(i, tm)
    a = a_ref[...]
    ext = tm + 2 * HALO
    mid = slice(HALO, HALO + tm)

    def branch(w_ref, c_ref):
        h = jnp.dot(a, w_ref[...], preferred_element_type=F32)
        cw = c_ref[...]
        dn = jnp.where(first, 0.0, pltpu.roll(h, 1, 0)[mid])
        up = jnp.where(last, 0.0, pltpu.roll(h, ext - 1, 0)[mid])
        return cw[0:1, :] * dn + cw[1:2, :] * h[mid] + cw[2:3, :] * up

    g = branch(wg_ref, cg_ref)
    u = branch(wu_ref, cu_ref)
    o_ref[...] = (g * jax.nn.sigmoid(g) * u).astype(o_ref.dtype)


def ffn_up_proj(u2p, w_up, conv_w, layer, *, tn=256):
    m, d = u2p.shape[0] - 2 * HALO, u2p.shape[1]
    tm = TM
    dff = w_up.shape[-1] // 2
    nj = dff // tn
    assert dff % tn == 0 and m % tm == 0
    return pl.pallas_call(
        functools.partial(_ffn_up_kernel, tm=tm),
        grid=(m // tm, nj),
        in_specs=[
            pl.BlockSpec((pl.Element(tm + 2 * HALO), pl.Element(d)), lambda i, j: (i * tm, 0)),
            pl.BlockSpec((None, d, tn), lambda i, j: (layer, 0, j)),
            pl.BlockSpec((None, d, tn), lambda i, j: (layer, 0, j + nj)),
            pl.BlockSpec((None, FFN_CONV, tn), lambda i, j: (layer, 0, j)),
            pl.BlockSpec((None, FFN_CONV, tn), lambda i, j: (layer, 0, j + nj)),
        ],
        out_specs=pl.BlockSpec((tm, tn), lambda i, j: (i, j)),
        out_shape=jax.ShapeDtypeStruct((m, dff), BF16),
        compiler_params=_cparams(("parallel", "arbitrary")),
        name="ffn_up",
    )(u2p, w_up, w_up, conv_w, conv_w)


class QKPart(NamedTuple):
    q_arr: jax.Array
    q_col: int
    k_arr: jax.Array
    k_col: int
    k_per_head: bool
    wq: jax.Array
    wk: jax.Array
    n_valid: int
    tabs: Optional[tuple]


def _swap_halves(y, n_valid):
    half = n_valid // 2
    if n_valid == LANES:
        return pltpu.roll(y, half, 1)
    lane = lax.broadcasted_iota(jnp.int32, (1, LANES), 1)
    return jnp.where(lane < half, pltpu.roll(y, LANES - half, 1), pltpu.roll(y, half, 1))


def _norm_rope(x, w, n_valid, tabs):
    if n_valid < LANES:
        lane = lax.broadcasted_iota(jnp.int32, (1, LANES), 1)
        x = jnp.where(lane < n_valid, x, 0.0)
    y = x * lax.rsqrt(jnp.sum(x * x, axis=-1, keepdims=True) * (1.0 / n_valid) + EPS) * w
    if tabs is not None:
        y = y * tabs[0] + _swap_halves(y, n_valid) * tabs[1]
    return y


def _attn_kernel(*refs, n_parts, n_valid, roped, segs, tq, scale):
    n_seg = len(segs)
    it = iter(refs)
    q_refs = [next(it) for _ in range(n_parts)]
    k_refs = [[next(it) for _ in range(n_parts)] for _ in range(n_seg)]
    v_refs = [next(it) for _ in range(n_seg)]
    wq_refs = [next(it) for _ in range(n_parts)]
    wk_refs = [next(it) for _ in range(n_parts)]
    tab_refs = [(next(it), next(it)) if roped[p] else None for p in range(n_parts)]
    o_ref = next(it)
    k_scr = [next(it) for _ in range(n_seg)]
    v_scr = [next(it) for _ in range(n_seg)]
    t = pl.program_id(3)

    @pl.when(jnp.logical_and(pl.program_id(2) == 0, t == 0))
    def _():
        for si, seg in enumerate(segs):
            for p in range(n_parts):
                tabs = None
                if seg == "lat" and roped[p]:
                    tabs = (tab_refs[p][0][...], tab_refs[p][1][...])
                y = _norm_rope(k_refs[si][p][...].astype(F32), wk_refs[p][...], n_valid[p], tabs)
                k_scr[si][:, p * LANES:(p + 1) * LANES] = y.astype(BF16)
            rows = v_scr[si].shape[0]
            lane = lax.broadcasted_iota(jnp.int32, (rows, LANES), 1)
            v_scr[si][:, :LANES] = v_refs[si][...]
            v_scr[si][:, LANES:] = jnp.where(lane == 0, 1.0, 0.0).astype(BF16)

    parts = []
    for p in range(n_parts):
        tabs = None
        if roped[p]:
            r = pl.ds(pl.multiple_of(t * tq, tq), tq)
            tabs = (tab_refs[p][0][r, :], tab_refs[p][1][r, :])
        parts.append(_norm_rope(q_refs[p][...].astype(F32), wq_refs[p][...], n_valid[p], tabs) * scale)
    q = jnp.concatenate(parts, axis=1).astype(BF16)
    dn_t = (((1,), (1,)), ((), ()))
    sub = min(tq, Q_SUB)
    for r0 in range(0, tq, sub):
        qs = q[r0:r0 + sub]
        ss = [lax.dot_general(qs, k_scr[si][...], dn_t, preferred_element_type=F32) for si in range(n_seg)]
        m = jnp.max(ss[0], axis=-1, keepdims=True)
        for s in ss[1:]:
            m = jnp.maximum(m, jnp.max(s, axis=-1, keepdims=True))
        o_ext = None
        for si, s in enumerate(ss):
            pv = jnp.dot(jnp.exp((s - m).astype(BF16)), v_scr[si][...], preferred_element_type=F32)
            o_ext = pv if o_ext is None else o_ext + pv
        o_ref[r0:r0 + sub, :] = (o_ext[:, :LANES] / o_ext[:, LANES:LANES + 1]).astype(o_ref.dtype)


def attention_mix(parts, v_arr, v_col, n_heads, n_kv, dk, q_stream):
    g = n_heads // n_kv
    ctx_blk0 = N_LAT // CTX_LEN
    lat = q_stream == "lat"
    tq = min(TQ, SEQ) if lat else CTX_LEN
    nt = SEQ // tq if lat else 1
    segs = ("ctx", "lat") if lat else ("ctx",)
    roped = tuple(lat and part.tabs is not None for part in parts)
    seg_rows = {"ctx": CTX_LEN, "lat": SEQ}
    seg_blk0 = {"ctx": ctx_blk0, "lat": 0}
    q_blk0 = 0 if lat else ctx_blk0

    def slab(rows, blk0, col, per_head):
        return pl.BlockSpec((rows, LANES), lambda b, hk, gi, t: (blk0 + b, col + (hk if per_head else 0)))

    in_specs, args = [], []
    for part in parts:
        in_specs.append(pl.BlockSpec(
            (tq, LANES), functools.partial(lambda b, hk, gi, t, col: (q_blk0 + b * nt + t, col + hk * g + gi),
                                           col=part.q_col)))
        args.append(part.q_arr)
    for seg in segs:
        for part in parts:
            in_specs.append(slab(seg_rows[seg], seg_blk0[seg], part.k_col, part.k_per_head))
            args.append(part.k_arr)
    for seg in segs:
        in_specs.append(slab(seg_rows[seg], seg_blk0[seg], v_col, True))
        args.append(v_arr)
    const = lambda b, hk, gi, t: (0, 0)
    for w in [part.wq for part in parts] + [part.wk for part in parts]:
        in_specs.append(pl.BlockSpec((1, LANES), const))
        args.append(w.reshape(1, LANES))
    for part, r in zip(parts, roped):
        if r:
            in_specs += [pl.BlockSpec((SEQ, LANES), const)] * 2
            args += list(part.tabs)
    n_parts = len(parts)
    scratch = [pltpu.VMEM((seg_rows[seg], n_parts * LANES), BF16) for seg in segs]
    scratch += [pltpu.VMEM((seg_rows[seg], 2 * LANES), BF16) for seg in segs]
    q_rows = N_LAT if lat else N_CTX
    return pl.pallas_call(
        functools.partial(_attn_kernel, n_parts=n_parts, n_valid=tuple(part.n_valid for part in parts), roped=roped,
                          segs=segs, tq=tq, scale=dk ** -0.5),
        grid=(BATCH, n_kv, g, nt),
        in_specs=in_specs,
        out_specs=pl.BlockSpec((tq, LANES), lambda b, hk, gi, t: (b * nt + t, hk * g + gi)),
        out_shape=jax.ShapeDtypeStruct((q_rows, n_heads * LANES), BF16),
        scratch_shapes=scratch,
        compiler_params=_cparams(("arbitrary",) * 4),
        name="attention",
    )(*args)


def _col_from_row(row, eye):
    return jnp.sum(jnp.where(eye, row, 0.0), axis=1, keepdims=True)


def _rows(idx, size):
    return pl.ds(pl.multiple_of(idx * size, size), size)


def _seq_conv_silu(x, cw):
    zero = jnp.zeros((1, x.shape[1]), F32)
    t = lax.broadcasted_iota(jnp.int32, (x.shape[0], 1), 0)
    c = _conv3_rows(x, zero, zero, cw, t == 0, t == x.shape[0] - 1)
    return c * jax.nn.sigmoid(c)


N_LEVELS = int(math.log2(CHUNK))
MASK_SAME, MASK_EYE, MASK_BE = 0, 1, 2
MASK_LEVEL = 4
N_MASKS = MASK_LEVEL + 2 * N_LEVELS


def _gdn_init_masks(mask_ref):
    n = GROUP
    ii = lax.broadcasted_iota(jnp.int32, (n, n), 0)
    jj = lax.broadcasted_iota(jnp.int32, (n, n), 1)
    same = (ii // CHUNK) == (jj // CHUNK)

    def put(idx, m):
        mask_ref[idx] = jnp.where(m, 1.0, 0.0)

    put(MASK_SAME, same)
    put(MASK_EYE, ii == jj)
    put(MASK_BE, jnp.logical_and(same, jj <= ii))
    put(MASK_BE + 1, jnp.logical_and(same, jj >= ii))
    for level in range(N_LEVELS):
        s = 1 << level
        blk = (ii // (2 * s)) == (jj // (2 * s))
        ih = (ii // s) & 1
        jh = (jj // s) & 1
        put(MASK_LEVEL + level, jnp.logical_and(blk, jnp.logical_and(ih == 1, jh == 0)))
        put(MASK_LEVEL + N_LEVELS + level, jnp.logical_and(blk, jnp.logical_and(ih == 0, jh == 1)))


def _gdn_prep_dir(q, k, v, gates, d, mask_ref):
    def mask(idx):
        return mask_ref[idx] > 0.5

    eye = mask(MASK_EYE)
    same = mask(MASK_SAME)
    be = mask(MASK_BE + d)
    be_t = mask(MASK_BE + 1 - d)
    k16 = k.astype(BF16)
    dn_t = (((1,), (1,)), ((), ()))
    kq = lax.dot_general(jnp.concatenate([k16, q.astype(BF16)], axis=0), k16, dn_t, preferred_element_type=F32)
    kk, qk0 = kq[:k.shape[0]], kq[k.shape[0]:]
    g_row, b_row = gates[d:d + 1, :], gates[2 + d:3 + d, :]
    g_col = _col_from_row(g_row, eye)
    b_col = _col_from_row(b_row, eye)
    gc_col = jnp.sum(jnp.where(be, g_row, 0.0), axis=1, keepdims=True)
    gc_row = jnp.sum(jnp.where(be_t, g_col, 0.0), axis=0, keepdims=True)
    tot_col = jnp.sum(jnp.where(same, g_row, 0.0), axis=1, keepdims=True)
    decay = jnp.where(be, jnp.exp(jnp.where(be, gc_col - gc_row, 0.0)), 0.0)
    mm = jnp.where(jnp.logical_and(be, jnp.logical_not(eye)), b_col * kk * decay, 0.0)
    qk = qk0 * decay
    t_inv = jnp.where(eye, 1.0, 0.0) - jnp.where(mask(MASK_LEVEL + d * N_LEVELS), mm, 0.0)
    for level in range(1, N_LEVELS):
        mc = jnp.where(mask(MASK_LEVEL + d * N_LEVELS + level), mm, 0.0)
        tb = t_inv.astype(BF16)
        y = jnp.dot(tb, mc.astype(BF16), preferred_element_type=F32)
        t_inv = t_inv - jnp.dot(y.astype(BF16), tb, preferred_element_type=F32)
    egc = jnp.exp(gc_col)
    kb = k * b_col
    rhs = jnp.concatenate([v * b_col, kb * egc], axis=1).astype(BF16)
    uw = jnp.dot(t_inv.astype(BF16), rhs, preferred_element_type=F32)
    dv = v.shape[1]
    return uw[:, :dv], uw[:, dv:], q * egc, k * jnp.exp(tot_col - gc_col), qk, jnp.exp(tot_col)


def _gdn_stream(x_refs, cw_refs, gates_ref, z_ref, onorm_ref, y_ref, states, scr):
    q_ref, k_ref, v_ref = x_refs
    mask_ref, qs_ref, ks_ref, vs_ref, o_ref = scr[:5]
    slots = (scr[5:9], scr[9:13])
    t_len = q_ref.shape[0]
    n_groups = t_len // GROUP
    cpg = GROUP // CHUNK
    dk = q_ref.shape[1]

    def l2(x):
        return x * lax.rsqrt(jnp.sum(x * x, axis=-1, keepdims=True) + EPS)

    rows_all = pl.ds(0, t_len)
    qs_ref[rows_all, :] = l2(_seq_conv_silu(q_ref[...].astype(F32), cw_refs[0][...])) * dk ** -0.5
    ks_ref[rows_all, :] = l2(_seq_conv_silu(k_ref[...].astype(F32), cw_refs[1][...]))
    vs_ref[rows_all, :] = _seq_conv_silu(v_ref[...].astype(F32), cw_refs[2][...])

    def load_group(g):
        r = _rows(g, GROUP)
        return qs_ref[r, :], ks_ref[r, :], vs_ref[r, :], gates_ref[g]

    def prep(slot, d, inputs):
        u_ref, wq_ref, lt_ref, gl_ref = slot
        u, w, qg, kg, qk, glast = _gdn_prep_dir(*inputs, d, mask_ref)
        u_ref[d] = u
        kg_t = kg.T
        for c in range(cpg):
            rc = slice(c * CHUNK, (c + 1) * CHUNK)
            wq_ref[d, c] = jnp.concatenate([w[rc], qg[rc]], axis=0).astype(BF16)
            lt_ref[d, c] = jnp.concatenate([kg_t[:, rc], qk[rc, rc]], axis=0).astype(BF16)
            gl_ref[d, c] = jnp.broadcast_to(glast[c * CHUNK:c * CHUNK + 8, :], (8, LANES))

    def scan_group(slot, d, state):
        u_ref, wq_ref, lt_ref, gl_ref = slot
        u = u_ref[d]
        order = range(cpg) if d == 0 else range(cpg - 1, -1, -1)
        o_parts = [None] * cpg
        for c in order:
            rc = slice(c * CHUNK, (c + 1) * CHUNK)
            ws = jnp.dot(wq_ref[d, c], state.astype(BF16), preferred_element_type=F32)
            v_new = u[rc] - ws[:CHUNK]
            r = jnp.dot(lt_ref[d, c], v_new.astype(BF16), preferred_element_type=F32)
            state = state * gl_ref[d, c][0:1, :] + r[:dk]
            o_parts[c] = ws[CHUNK:] + r[dk:]
        return jnp.concatenate(o_parts, axis=0), state

    def step(t, carry, cur, nxt, accumulate, prefetch):
        s_f, s_b = carry
        g_b = n_groups - 1 - t
        if prefetch:
            in_f, in_b = load_group(t + 1), load_group(g_b - 1)
            prep(nxt, 0, in_f)
            prep(nxt, 1, in_b)
        o_f, s_f = scan_group(cur, 0, s_f)
        o_b, s_b = scan_group(cur, 1, s_b)
        if accumulate:
            o_ref[_rows(t, GROUP), :] += o_f
            o_ref[_rows(g_b, GROUP), :] += o_b
        else:
            o_ref[_rows(t, GROUP), :] = o_f
            o_ref[_rows(g_b, GROUP), :] = o_b
        return s_f, s_b

    in_f, in_b = load_group(0), load_group(n_groups - 1)
    prep(slots[0], 0, in_f)
    prep(slots[0], 1, in_b)
    if n_groups == 1:
        o_f, s_f = scan_group(slots[0], 0, states[0])
        o_b, s_b = scan_group(slots[0], 1, states[1])
        o_ref[pl.ds(0, GROUP), :] = o_f + o_b
        states = (s_f, s_b)
    else:
        n_pairs = n_groups // 2
        assert n_groups == 2 * n_pairs and n_pairs % 2 == 0

        def pair(p, carry, accumulate, last=False):
            carry = step(2 * p, carry, slots[0], slots[1], accumulate, True)
            return step(2 * p + 1, carry, slots[1], slots[0], accumulate, not last)

        states = lax.fori_loop(0, n_pairs // 2, lambda p, c: pair(p, c, False), states)
        states = lax.fori_loop(n_pairs // 2, n_pairs - 1, lambda p, c: pair(p, c, True), states)
        states = pair(n_pairs - 1, states, True, last=True)

    o = o_ref[rows_all, :]
    o = o * lax.rsqrt(jnp.mean(o * o, axis=-1, keepdims=True) + EPS) * onorm_ref[...]
    z = z_ref[...].astype(F32)
    y_ref[...] = (o * (z * jax.nn.sigmoid(z))).astype(y_ref.dtype)
    return states


def _gdn_kernel(qc, kc, vc, zc, gc, ql, kl, vl, zl, gl, cwq, cwk, cwv, onorm, yc_ref, yl_ref, *scr):
    @pl.when(jnp.logical_and(pl.program_id(0) == 0, pl.program_id(1) == 0))
    def _():
        _gdn_init_masks(scr[0])

    cws = (cwq, cwk, cwv)
    zero = jnp.zeros((kc.shape[1], vc.shape[1]), F32)
    states = _gdn_stream((qc, kc, vc), cws, gc.at[0, 0], zc, onorm, yc_ref, (zero, zero), scr)
    _gdn_stream((ql, kl, vl), cws, gl.at[0, 0], zl, onorm, yl_ref, states, scr)


def gdn_mix(p, col_q, col_z, gates_c, gates_l, conv_w, out_norm):
    h, dk, dv = H_B, DK_B, DV_B
    assert dk == LANES and dv == LANES and col_q % LANES == 0 and col_z % LANES == 0
    cq, cz = col_q // LANES, col_z // LANES
    ctx_blk0 = N_LAT // CTX_LEN
    cpg = GROUP // CHUNK

    def slab(t, row0, col0):
        return pl.BlockSpec((t, LANES), lambda bi, hi: (row0 + bi, col0 + hi))

    def stream_specs(t, row0, gates):
        return [slab(t, row0, cq), slab(t, row0, cq + h), slab(t, row0, cq + 2 * h), slab(t, row0, cz),
                pl.BlockSpec((1, 1) + gates.shape[2:], lambda bi, hi: (bi, hi, 0, 0, 0))]

    in_specs = stream_specs(CTX_LEN, ctx_blk0, gates_c) + stream_specs(SEQ, 0, gates_l)
    in_specs += [pl.BlockSpec((SHORT_CONV, LANES), functools.partial(lambda bi, hi, off: (0, off + hi), off=off))
                 for off in (0, h, 2 * h)]
    in_specs.append(pl.BlockSpec((1, dv), lambda bi, hi: (0, 0)))
    slot = [
        pltpu.VMEM((2, GROUP, dv), F32),
        pltpu.VMEM((2, cpg, 2 * CHUNK, dk), BF16),
        pltpu.VMEM((2, cpg, dk + CHUNK, CHUNK), BF16),
        pltpu.VMEM((2, cpg, 8, LANES), F32),
    ]
    scratch = [
        pltpu.VMEM((N_MASKS, GROUP, GROUP), F32),
        pltpu.VMEM((SEQ, dk), F32), pltpu.VMEM((SEQ, dk), F32), pltpu.VMEM((SEQ, dv), F32),
        pltpu.VMEM((SEQ, dv), F32),
    ] + slot + slot
    return pl.pallas_call(
        _gdn_kernel,
        grid=(BATCH, h),
        in_specs=in_specs,
        out_specs=[pl.BlockSpec((CTX_LEN, dv), lambda bi, hi: (bi, hi)),
                   pl.BlockSpec((SEQ, dv), lambda bi, hi: (bi, hi))],
        out_shape=[jax.ShapeDtypeStruct((N_CTX, h * dv), BF16), jax.ShapeDtypeStruct((N_LAT, h * dv), BF16)],
        scratch_shapes=scratch,
        compiler_params=_cparams(("arbitrary", "arbitrary")),
        name="gdn_mix",
    )(p, p, p, p, gates_c, p, p, p, p, gates_l, conv_w, conv_w, conv_w, out_norm.reshape(1, dv))


def _tri_masks(n, rev):
    ii = lax.broadcasted_iota(jnp.int32, (n, n), 0)
    jj = lax.broadcasted_iota(jnp.int32, (n, n), 1)
    eye = ii == jj
    if rev:
        be, be_t = jj >= ii, ii >= jj
    else:
        be, be_t = jj <= ii, ii <= jj
    return eye, be, be_t


def _mlstm_chunk(q, k, v, i_row, f_row, carry, c_ref, rev):
    n_vec, m = carry
    n = q.shape[0]
    eye, be, be_t = _tri_masks(n, rev)
    f_col = _col_from_row(f_row, eye)
    i_col = _col_from_row(i_row, eye)
    b_col = jnp.sum(jnp.where(be, f_row, 0.0), axis=1, keepdims=True)
    b_row = jnp.sum(jnp.where(be_t, f_col, 0.0), axis=0, keepdims=True)
    b_tot = jnp.sum(f_row, axis=1, keepdims=True)
    d = jnp.where(be, b_col - b_row + i_row, -jnp.inf)
    inter = b_col + m
    m_t = jnp.maximum(inter, jnp.max(d, axis=1, keepdims=True))
    a_inter = jnp.exp(inter - m_t)
    qb = q.astype(BF16)
    kbf = k.astype(BF16)
    s = lax.dot_general(qb, kbf, (((1,), (1,)), ((), ())), preferred_element_type=F32)
    p = jnp.exp(d - m_t) * s
    c_mat = c_ref[...]
    num = a_inter * jnp.dot(qb, c_mat.astype(BF16), preferred_element_type=F32) + jnp.dot(
        p.astype(BF16), v.astype(BF16), preferred_element_type=F32)
    den = a_inter * jnp.sum(q * n_vec, axis=1, keepdims=True) + jnp.sum(p, axis=1, keepdims=True)
    h = num / jnp.maximum(jnp.abs(den), jnp.exp(-m_t))
    w_end_row = b_tot - b_row + i_row
    w_end_col = b_tot - b_col + i_col
    m_new = jnp.maximum(b_tot + m, jnp.max(w_end_row, axis=1, keepdims=True))
    g_old = jnp.exp(b_tot + m - m_new)
    gs_col = jnp.exp(w_end_col - m_new)
    upd = lax.dot_general(kbf, (gs_col * v).astype(BF16), (((0,), (0,)), ((), ())), preferred_element_type=F32)
    c_ref[...] = g_old * c_mat + upd
    n_new = g_old * n_vec + jnp.sum(gs_col * k, axis=0, keepdims=True)
    return h, (n_new, m_new)


def _mlstm_stream(q_ref, k_ref, v_ref, gates_ref, opre_ref, onorm_ref, y_ref, carries, o_ref, c_refs, unroll):
    t_len = q_ref.shape[0]
    n_chunks = t_len // CHUNK
    half = n_chunks // 2
    assert n_chunks == 2 * half
    scale = q_ref.shape[1] ** -0.5

    def chunk(c, carry, rev):
        r = _rows(c, CHUNK)
        gates = gates_ref[c]
        d = 1 if rev else 0
        return _mlstm_chunk(q_ref[r, :].astype(F32) * scale, k_ref[r, :].astype(F32), v_ref[r, :],
                            gates[d:d + 1, :], gates[2 + d:3 + d, :], carry, c_refs[d], rev)

    def step(t, carry, accumulate):
        cf, cb = carry
        c_b = n_chunks - 1 - t
        o_f, cf = chunk(t, cf, False)
        o_b, cb = chunk(c_b, cb, True)
        if accumulate:
            o_ref[_rows(t, CHUNK), :] += o_f
            o_ref[_rows(c_b, CHUNK), :] += o_b
        else:
            o_ref[_rows(t, CHUNK), :] = o_f
            o_ref[_rows(c_b, CHUNK), :] = o_b
        return cf, cb

    carries = lax.fori_loop(0, half, lambda t, c: step(t, c, False), carries, unroll=unroll)
    carries = lax.fori_loop(half, n_chunks, lambda t, c: step(t, c, True), carries, unroll=unroll)
    o = o_ref[pl.ds(0, t_len), :]
    o = o * lax.rsqrt(jnp.mean(o * o, axis=-1, keepdims=True) + EPS) * onorm_ref[0]
    y_ref[...] = (o * jax.nn.sigmoid(opre_ref[...].astype(F32))).astype(y_ref.dtype)
    return carries


def _mlstm_kernel(qc, kc, vc, oc, gc, ql, kl, vl, ol, gl, onorm, yc_ref, yl_ref, o_ref, cf_ref, cb_ref, *, unroll):
    cf_ref[...] = jnp.zeros(cf_ref.shape, F32)
    cb_ref[...] = jnp.zeros(cb_ref.shape, F32)
    init = (jnp.zeros((1, kc.shape[1]), F32), jnp.zeros((1, 1), F32))
    c_refs = (cf_ref, cb_ref)
    carries = _mlstm_stream(qc, kc, vc, gc.at[0, 0], oc, onorm, yc_ref, (init, init), o_ref, c_refs, unroll)
    _mlstm_stream(ql, kl, vl, gl.at[0, 0], ol, onorm, yl_ref, carries, o_ref, c_refs, unroll)


def mlstm_mix(p, col_q, col_k, col_v, col_o, gates_c, gates_l, out_norm, *, unroll=2):
    h, dk, dv = H_C, DK_C, DV_C
    assert col_q % dk == 0 and col_k % dk == 0 and col_v % dv == 0 and col_o % dv == 0
    ctx_blk0 = N_LAT // CTX_LEN

    def slab(t, row0, width, col):
        c0 = col // width
        return pl.BlockSpec((t, width), lambda bi, hi: (row0 + bi, c0 + hi))

    def stream_specs(t, row0, gates):
        return [slab(t, row0, dk, col_q), slab(t, row0, dk, col_k), slab(t, row0, dv, col_v), slab(t, row0, dv, col_o),
                pl.BlockSpec((1, 1) + gates.shape[2:], lambda bi, hi: (bi, hi, 0, 0, 0))]

    in_specs = stream_specs(CTX_LEN, ctx_blk0, gates_c) + stream_specs(SEQ, 0, gates_l)
    in_specs.append(pl.BlockSpec((1, 1, dv), lambda bi, hi: (hi, 0, 0)))
    return pl.pallas_call(
        functools.partial(_mlstm_kernel, unroll=unroll),
        grid=(BATCH, h),
        in_specs=in_specs,
        out_specs=[pl.BlockSpec((CTX_LEN, dv), lambda bi, hi: (bi, hi)),
                   pl.BlockSpec((SEQ, dv), lambda bi, hi: (bi, hi))],
        out_shape=[jax.ShapeDtypeStruct((N_CTX, h * dv), BF16), jax.ShapeDtypeStruct((N_LAT, h * dv), BF16)],
        scratch_shapes=[pltpu.VMEM((SEQ, dv), F32), pltpu.VMEM((dk, dv), F32), pltpu.VMEM((dk, dv), F32)],
        compiler_params=_cparams(("parallel", "parallel")),
        name="mlstm_mix",
    )(p, p, p, p, gates_c, p, p, p, p, gates_l, out_norm.reshape(h, 1, dv))


def _axial_rope(n_rows, rot_dim):
    r, col = jnp.meshgrid(jnp.arange(n_rows), jnp.arange(GRID_W), indexing="ij")
    r = r.reshape(-1).astype(F32)
    col = col.reshape(-1).astype(F32)
    n_freq = rot_dim // 4
    inv = ROPE_THETA ** (-jnp.arange(n_freq, dtype=F32) / n_freq)
    ang = jnp.concatenate([r[:, None] * inv, col[:, None] * inv], axis=-1)
    return jnp.cos(ang), jnp.sin(ang)


def _deinterleave(n):
    return np.concatenate([np.arange(0, n, 2), np.arange(1, n, 2)])


def _pad_lanes(a):
    pad = LANES - a.shape[-1]
    return a if pad == 0 else jnp.concatenate([a, jnp.zeros(a.shape[:-1] + (pad,), a.dtype)], axis=-1)


def _rope_tables(n_rows, rot_dim):
    cos, sin = _axial_rope(n_rows, rot_dim)
    return _pad_lanes(jnp.concatenate([cos, cos], axis=1)), _pad_lanes(jnp.concatenate([-sin, sin], axis=1))


def _gate_rows(g, width):
    b, t, h = g[0].shape
    rows = [a.transpose(0, 2, 1).reshape(b, h, t // width, 1, width) for a in g]
    rows += [jnp.zeros_like(rows[0])] * (8 - len(rows))
    return jnp.concatenate(rows, axis=3)


def _streams(a):
    c = a.shape[-1]
    return a[N_LAT:].reshape(BATCH, CTX_LEN, c), a[:N_LAT].reshape(BATCH, SEQ, c)


EV_CQ, EV_CKV, EV_QKV, EV_Z = 0, 1024, 1536, 7680
P_EVEN_MAIN = 9728
OD_MQ, OD_MK, OD_MV, OD_MO, OD_GQ, OD_GK, OD_GV = 0, 1024, 2048, 4096, 6144, 8192, 8704
P_ODD_MAIN = 9216


def _split_even(w_in):
    cq, ckv, kpe, qkv, z, ab = jnp.split(w_in, np.cumsum([1024, 512, 64, 6144, 2048]).tolist(), axis=1)
    kpe = kpe[:, _deinterleave(ROPE_A)]
    return jnp.concatenate([cq, ckv, qkv, z, kpe, ab], axis=1)


def _split_odd(w_in):
    mq, mk, mv, mo, mg, gq, gk, gv = jnp.split(
        w_in, np.cumsum([1024, 1024, 2048, 2048, 16, 2048, 512]).tolist(), axis=1)
    perm = _deinterleave(HD_D)
    per_head = lambda w: w.reshape(w.shape[0], -1, HD_D)[:, :, perm].reshape(w.shape)
    return jnp.concatenate([mq, mk, mv, mo, per_head(gq), per_head(gk), gv, _pad_lanes(mg)], axis=1)


def _mla_weights(w_qb, w_kvb):
    wq = w_qb.reshape(Q_LORA, H_A, NOPE_A + ROPE_A)
    q_pe = _pad_lanes(wq[:, :, NOPE_A:][:, :, _deinterleave(ROPE_A)])
    wq = jnp.concatenate([wq[:, :, :NOPE_A].reshape(Q_LORA, -1), q_pe.reshape(Q_LORA, -1)], axis=1)
    wkv = w_kvb.reshape(KV_LORA, H_A, NOPE_A + V_A)
    wkv = jnp.concatenate([wkv[:, :, :NOPE_A].reshape(KV_LORA, -1), wkv[:, :, NOPE_A:].reshape(KV_LORA, -1)], axis=1)
    return wq, wkv


def _even_mixers(p, pg, need_ctx, rope, w_qb, q_lora_norm, w_kvb, kv_lora_norm, q_head_norm, k_head_norm,
                 gdn_conv, a_log, dt_bias, out_norm):
    qn = rms_rows(p, EV_CQ // Q_LORA, Q_LORA, q_lora_norm)
    kvn = rms_rows(p, EV_CKV // KV_LORA, KV_LORA, kv_lora_norm)
    wq, wkv = _mla_weights(w_qb.astype(BF16), w_kvb.astype(BF16))
    q_all = matmul(qn, wq, tn=1024, out_dtype=BF16)
    kv_all = matmul(kvn, wkv, tn=1024, out_dtype=BF16)
    perm = _deinterleave(ROPE_A)
    parts = [
        QKPart(q_all, 0, kv_all, 0, True, q_head_norm[:NOPE_A], k_head_norm[:NOPE_A], NOPE_A, None),
        QKPart(q_all, H_A, pg, 0, False, _pad_lanes(q_head_norm[NOPE_A:][perm]),
               _pad_lanes(k_head_norm[NOPE_A:][perm]), ROPE_A, rope),
    ]
    a_lat = attention_mix(parts, kv_all, H_A, H_A, H_A, NOPE_A + ROPE_A, "lat")
    def gdn_gates(ab):
        ab = ab.reshape(ab.shape[0], ab.shape[1], 4, H_B)
        g = -jnp.exp(a_log)[None, None] * jax.nn.softplus(ab[:, :, :2] + dt_bias[None, None])
        beta = jax.nn.sigmoid(ab[:, :, 2:])
        return _gate_rows([g[:, :, 0], g[:, :, 1], beta[:, :, 0], beta[:, :, 1]], GROUP)

    ab_c, ab_l = _streams(pg[:, ROPE_A:ROPE_A + 4 * H_B])
    b_ctx, b_lat = gdn_mix(p, EV_QKV, EV_Z, gdn_gates(ab_c), gdn_gates(ab_l), gdn_conv, out_norm)
    if not need_ctx:
        return [a_lat, b_lat], None
    a_ctx = attention_mix(parts, kv_all, H_A, H_A, H_A, NOPE_A + ROPE_A, "ctx")
    return [a_lat, b_lat], [a_ctx, b_ctx]


def _odd_mixers(p, pg, need_ctx, rope, gate_bias, out_norm, q_norm, k_norm):
    def mlstm_gates(g):
        g = g.reshape(g.shape[0], g.shape[1], 4, H_C) + gate_bias
        g = GATE_CAP * jnp.tanh(g / GATE_CAP)
        return _gate_rows([g[:, :, 0], g[:, :, 1], jax.nn.log_sigmoid(g[:, :, 2]), jax.nn.log_sigmoid(g[:, :, 3])],
                          CHUNK)

    mg_c, mg_l = _streams(pg[:, :4 * H_C])
    c_ctx, c_lat = mlstm_mix(p, OD_MQ, OD_MK, OD_MV, OD_MO, mlstm_gates(mg_c), mlstm_gates(mg_l), out_norm)
    perm = _deinterleave(HD_D)
    parts = [QKPart(p, OD_GQ // LANES, p, OD_GK // LANES, True, q_norm[perm], k_norm[perm], HD_D, rope)]
    d_lat = attention_mix(parts, p, OD_GV // LANES, H_D, KV_D, HD_D, "lat")
    if not need_ctx:
        return [c_lat, d_lat], None
    d_ctx = attention_mix(parts, p, OD_GV // LANES, H_D, KV_D, HD_D, "ctx")
    return [c_lat, d_lat], [c_ctx, d_ctx]


def kernel(x, c, ctx, c_ctx, mod_down, mod_up, mod_bias, norm_mix, norm_ffn, ffn_up, ffn_conv, ffn_down,
           ev_w_in, ev_w_out, mla_w_qb, mla_q_lora_norm, mla_w_kvb, mla_kv_lora_norm, mla_q_head_norm,
           mla_k_head_norm, gdn_conv, gdn_a_log, gdn_dt_bias, gdn_out_norm, od_w_in, od_w_out,
           mlstm_gate_bias, mlstm_out_norm, gqa_q_norm, gqa_k_norm):
    n_rows = SEQ // GRID_W
    rope_a = _rope_tables(n_rows, ROPE_A)
    rope_d = _rope_tables(n_rows, HD_D)
    cond = jnp.concatenate([c, c_ctx[None, :], jnp.zeros((MOD_ROWS - BATCH - 1, D_MODEL), F32)], axis=0)
    mods = adaln_all(cond, mod_down, mod_up, mod_bias)
    xs = jnp.concatenate([x.reshape(N_LAT, D_MODEL), ctx.reshape(N_CTX, D_MODEL)], axis=0)
    ffn_up16, ffn_down16 = ffn_up.astype(BF16), ffn_down.astype(BF16)
    w_out16 = (ev_w_out.astype(BF16), od_w_out.astype(BF16))
    for layer in range(DEPTH):
        need_ctx = layer < DEPTH - 1
        j = layer // 2
        mod = mods[layer].reshape(MOD_ROWS, 1, 6 * D_MODEL)
        u = norm_mod(xs, norm_mix[layer], mod, 0)
        w_in = (ev_w_in if layer % 2 == 0 else od_w_in)[j].astype(BF16)
        w_in = _split_even(w_in) if layer % 2 == 0 else _split_odd(w_in)
        p, pg = matmul(u, w_in, tn=1024, out_dtype=BF16, f32_tail=True)
        if layer % 2 == 0:
            y, y_ctx = _even_mixers(p, pg, need_ctx, rope_a, mla_w_qb[j], mla_q_lora_norm[j],
                                    mla_w_kvb[j], mla_kv_lora_norm[j], mla_q_head_norm[j], mla_k_head_norm[j],
                                    gdn_conv[j], gdn_a_log[j], gdn_dt_bias[j], gdn_out_norm[j])
        else:
            y, y_ctx = _odd_mixers(p, pg, need_ctx, rope_d, mlstm_gate_bias[j], mlstm_out_norm[j], gqa_q_norm[j],
                                   gqa_k_norm[j])
        xs = matmul(y, w_out16[layer % 2], tn=512, out_dtype=F32, res=xs, mod=mod, gate_chunk=2,
                    ctx_parts=y_ctx, w_layer=j)
        u2 = norm_mod(xs, norm_ffn[layer], mod, 3, pad=HALO)
        hmid = ffn_up_proj(u2, ffn_up16, ffn_conv, layer)
        xs = matmul(hmid, ffn_down16, tn=256, out_dtype=F32, res=xs, mod=mod, gate_chunk=5,
                    single_buffer_a=True, w_layer=layer)
    return xs.reshape(BATCH, SEQ, D_MODEL)
```

```python
import functools
import math
from typing import NamedTuple, Optional

import numpy as np
import jax
import jax.numpy as jnp
from jax import lax
from jax.experimental import pallas as pl
from jax.experimental.pallas import tpu as pltpu

F32 = jnp.float32
BF16 = jnp.bfloat16

D_MODEL = 4096
BATCH = 4
SEQ = 4096
DEPTH = 4
CTX_LEN = 256
GRID_W = 64
MOD_RANK = 256
H_A, Q_LORA, KV_LORA, NOPE_A, ROPE_A, V_A = 16, 1024, 512, 128, 64, 128
H_B, DK_B, DV_B, SHORT_CONV = 16, 128, 128, 3
H_C, DK_C, DV_C, GATE_CAP = 4, 256, 512, 15.0
H_D, KV_D, HD_D = 16, 4, 128
D_FF, FFN_CONV = 11008, 3
CHUNK = 64
ROPE_THETA = 10000.0
EPS = 1e-6

N_LAT = BATCH * SEQ
N_CTX = BATCH * CTX_LEN
MOD_ROWS = 8
LANES = 128

VMEM_LIMIT_V7X = 56 * 1024 * 1024

TM = 1024
TM_NORM = 256
TM_RMS = 512
TQ = 1024
Q_SUB_MLA = 256
Q_SUB_GQA = 128
GROUP = 256


def _cparams(sem):
    return pltpu.CompilerParams(dimension_semantics=sem, vmem_limit_bytes=VMEM_LIMIT_V7X)


def _mod_row(i, tm):
    return jnp.minimum((i * tm) // SEQ, BATCH)


def _adaln_kernel(cond_ref, down_ref, up_ref, bias_ref, o_ref):
    cond = cond_ref[...]
    a = cond * jax.nn.sigmoid(cond)
    t = jnp.dot(a, down_ref[0], preferred_element_type=F32, precision=lax.Precision.HIGHEST)
    o = jnp.dot(t, up_ref[0], preferred_element_type=F32, precision=lax.Precision.HIGHEST)
    o_ref[0] = o + bias_ref[0]


def adaln_all(cond, mod_down, mod_up, mod_bias):
    depth, d, rank = mod_down.shape
    n = mod_up.shape[-1]
    tn = d
    return pl.pallas_call(
        _adaln_kernel,
        grid=(depth, n // tn),
        in_specs=[
            pl.BlockSpec((MOD_ROWS, d), lambda l, j: (0, 0)),
            pl.BlockSpec((1, d, rank), lambda l, j: (l, 0, 0)),
            pl.BlockSpec((1, rank, tn), lambda l, j: (l, 0, j)),
            pl.BlockSpec((1, 1, tn), lambda l, j: (l, 0, j)),
        ],
        out_specs=pl.BlockSpec((1, MOD_ROWS, tn), lambda l, j: (l, 0, j)),
        out_shape=jax.ShapeDtypeStruct((depth, MOD_ROWS, n), F32),
        compiler_params=_cparams(("arbitrary", "arbitrary")),
        name="adaln",
    )(cond, mod_down, mod_up, mod_bias.reshape(depth, 1, n))


def _norm_mod_kernel(x_ref, w_ref, sh_ref, sc_ref, o_ref):
    x = x_ref[...]
    y = x * lax.rsqrt(jnp.mean(x * x, axis=-1, keepdims=True) + EPS)
    y = y * w_ref[...]
    o_ref[...] = (y * (1.0 + sc_ref[0]) + sh_ref[0]).astype(o_ref.dtype)


def norm_mod(xs, w, mod, shift_chunk, pad=0):
    m, d = xs.shape
    tm = TM_NORM
    out = pl.pallas_call(
        _norm_mod_kernel,
        grid=(m // tm,),
        in_specs=[
            pl.BlockSpec((tm, d), lambda i: (i, 0)),
            pl.BlockSpec((1, d), lambda i: (0, 0)),
            pl.BlockSpec((1, 1, d), lambda i: (_mod_row(i, tm), 0, shift_chunk)),
            pl.BlockSpec((1, 1, d), lambda i: (_mod_row(i, tm), 0, shift_chunk + 1)),
        ],
        out_specs=pl.BlockSpec((pl.Element(tm), pl.Element(d)),
                               lambda i: (pl.multiple_of(pad + i * tm, math.gcd(pad, tm) or tm), 0)),
        out_shape=jax.ShapeDtypeStruct((m + 2 * pad, d), BF16),
        compiler_params=_cparams(("parallel",)),
        name="norm_mod",
    )(xs, w.reshape(1, d), mod, mod)
    if pad:
        zeros = jnp.zeros((pad, d), BF16)
        out = lax.dynamic_update_slice(out, zeros, (0, 0))
        out = lax.dynamic_update_slice(out, zeros, (m + pad, 0))
    return out


def _rms_rows_kernel(x_ref, w_ref, o_ref):
    x = x_ref[...].astype(F32)
    y = x * lax.rsqrt(jnp.mean(x * x, axis=-1, keepdims=True) + EPS)
    o_ref[...] = (y * w_ref[...]).astype(o_ref.dtype)


def rms_rows(p, col_block, width, w):
    m = p.shape[0]
    tm = TM_RMS
    return pl.pallas_call(
        _rms_rows_kernel,
        grid=(m // tm,),
        in_specs=[
            pl.BlockSpec((tm, width), lambda i: (i, col_block)),
            pl.BlockSpec((1, width), lambda i: (0, 0)),
        ],
        out_specs=pl.BlockSpec((tm, width), lambda i: (i, 0)),
        out_shape=jax.ShapeDtypeStruct((m, width), BF16),
        compiler_params=_cparams(("parallel",)),
        name="rms_rows",
    )(p, w.reshape(1, width))


def _mm_kernel(*refs, n_parts, has_res, tail_off, n_lat_tiles):
    split = n_lat_tiles is not None
    n_a = n_parts * (2 if split else 1)
    a_refs, c_refs = refs[:n_parts], refs[n_parts:n_a]
    w_refs, rest = refs[n_a:n_a + n_parts], refs[n_a + n_parts:]

    def run(srcs):
        acc = jnp.dot(srcs[0][...], w_refs[0][...], preferred_element_type=F32)
        for a_ref, w_ref in zip(srcs[1:], w_refs[1:]):
            acc += jnp.dot(a_ref[...], w_ref[...], preferred_element_type=F32)
        if has_res:
            res_ref, gate_ref, o_ref = rest
            acc = res_ref[...] + gate_ref[0] * acc
        elif tail_off is not None:
            o_ref, tail_ref = rest

            @pl.when(pl.program_id(1) == pl.num_programs(1) - 1)
            def _():
                tail_ref[...] = acc[:, tail_off:tail_off + LANES]
        else:
            (o_ref,) = rest
        o_ref[...] = acc.astype(o_ref.dtype)

    if split:
        is_ctx = pl.program_id(0) >= n_lat_tiles
        pl.when(jnp.logical_not(is_ctx))(lambda: run(a_refs))
        pl.when(is_ctx)(lambda: run(c_refs))
    else:
        run(a_refs)


def matmul(a_parts, w, *, tn, out_dtype, res=None, mod=None, gate_chunk=None, single_buffer_a=False,
           f32_tail=False, ctx_parts=None, w_layer=None):
    if not isinstance(a_parts, (list, tuple)):
        a_parts = [a_parts]
    n_parts = len(a_parts)
    m_lat, kp = a_parts[0].shape
    n = w.shape[-1]
    tm, tn = TM, min(tn, n)
    assert m_lat % tm == 0 and w.shape[-2] == kp * n_parts and all(a.shape == (m_lat, kp) for a in a_parts)
    has_res = res is not None
    nj = pl.cdiv(n, tn)
    a_mode = dict(pipeline_mode=pl.Buffered(1)) if single_buffer_a else {}
    n_lat_tiles = None
    m = m_lat
    if ctx_parts is None:
        in_specs = [pl.BlockSpec((tm, kp), lambda i, j: (i, 0), **a_mode) for _ in a_parts]
        args = list(a_parts)
    else:
        n_lat_tiles = m_lat // tm
        m_ctx = ctx_parts[0].shape[0]
        assert m_ctx % tm == 0 and all(c.shape == (m_ctx, kp) for c in ctx_parts) and len(ctx_parts) == n_parts
        m = m_lat + m_ctx
        in_specs = [pl.BlockSpec((tm, kp), lambda i, j: (jnp.minimum(i, n_lat_tiles - 1), 0)) for _ in a_parts]
        in_specs += [pl.BlockSpec((tm, kp), lambda i, j: (jnp.maximum(i - n_lat_tiles, 0), 0),
                                  pipeline_mode=pl.Buffered(1)) for _ in ctx_parts]
        args = list(a_parts) + list(ctx_parts)
    if w_layer is None:
        in_specs += [pl.BlockSpec((kp, tn), functools.partial(lambda i, j, part: (part, j), part=part))
                     for part in range(n_parts)]
    else:
        in_specs += [pl.BlockSpec((None, kp, tn), functools.partial(lambda i, j, part: (w_layer, part, j), part=part))
                     for part in range(n_parts)]
    args += [w] * n_parts
    out_specs = pl.BlockSpec((tm, tn), lambda i, j: (i, j))
    out_shape = jax.ShapeDtypeStruct((m, n), out_dtype)
    tail_off = None
    if has_res:
        assert n % tn == 0 and mod.shape[-1] % tn == 0 and not f32_tail
        gate_off = gate_chunk * (n // tn)
        in_specs += [
            pl.BlockSpec((tm, tn), lambda i, j: (i, j)),
            pl.BlockSpec((1, 1, tn), lambda i, j: (_mod_row(i, tm), 0, gate_off + j)),
        ]
        args += [res, mod]
    elif f32_tail:
        tail_off = n - LANES - (nj - 1) * tn
        assert 0 <= tail_off and tail_off % LANES == 0
        out_specs = [out_specs, pl.BlockSpec((tm, LANES), lambda i, j: (i, 0))]
        out_shape = [out_shape, jax.ShapeDtypeStruct((m, LANES), F32)]
    return pl.pallas_call(
        functools.partial(_mm_kernel, n_parts=n_parts, has_res=has_res, tail_off=tail_off, n_lat_tiles=n_lat_tiles),
        grid=(m // tm, nj),
        in_specs=in_specs,
        out_specs=out_specs,
        out_shape=out_shape,
        compiler_params=_cparams(("parallel", "arbitrary")),
        name="matmul",
    )(*args)


HALO = 16


def _seq_pos(i, tm):
    r = i * tm + lax.broadcasted_iota(jnp.int32, (tm, 1), 0)
    pos = jnp.where(r < N_LAT, r & (SEQ - 1), (r - N_LAT) & (CTX_LEN - 1))
    last = jnp.where(r < N_LAT, SEQ - 1, CTX_LEN - 1)
    return pos == 0, pos == last


def _conv3_rows(h, h_prev_row, h_next_row, cw, first, last):
    tm = h.shape[0]
    t = lax.broadcasted_iota(jnp.int32, (tm, 1), 0)
    dn = jnp.where(t == 0, h_prev_row, pltpu.roll(h, 1, 0))
    dn = jnp.where(first, 0.0, dn)
    up = jnp.where(t == tm - 1, h_next_row, pltpu.roll(h, tm - 1, 0))
    up = jnp.where(last, 0.0, up)
    return cw[0:1, :] * dn + cw[1:2, :] * h + cw[2:3, :] * up


def _ffn_up_kernel(a_ref, wg_ref, wu_ref, cg_ref, cu_ref, o_ref, *, tm):
    i = pl.program_id(0)
    first, last = _seq_pos(i, tm)
    a = a_ref[...]
    ext = tm + 2 * HALO
    mid = slice(HALO, HALO + tm)

    def branch(w_ref, c_ref):
        h = jnp.dot(a, w_ref[...], preferred_element_type=F32)
        cw = c_ref[...]
        dn = jnp.where(first, 0.0, pltpu.roll(h, 1, 0)[mid])
        up = jnp.where(last, 0.0, pltpu.roll(h, ext - 1, 0)[mid])
        return cw[0:1, :] * dn + cw[1:2, :] * h[mid] + cw[2:3, :] * up

    g = branch(wg_ref, cg_ref)
    u = branch(wu_ref, cu_ref)
    o_ref[...] = (g * jax.nn.sigmoid(g) * u).astype(o_ref.dtype)


def ffn_up_proj(u2p, w_up, conv_w, layer, *, tn=256):
    m, d = u2p.shape[0] - 2 * HALO, u2p.shape[1]
    tm = TM
    dff = w_up.shape[-1] // 2
    nj = dff // tn
    assert dff % tn == 0 and m % tm == 0
    return pl.pallas_call(
        functools.partial(_ffn_up_kernel, tm=tm),
        grid=(m // tm, nj),
        in_specs=[
            pl.BlockSpec((pl.Element(tm + 2 * HALO), pl.Element(d)), lambda i, j: (i * tm, 0)),
            pl.BlockSpec((None, d, tn), lambda i, j: (layer, 0, j)),
            pl.BlockSpec((None, d, tn), lambda i, j: (layer, 0, j + nj)),
            pl.BlockSpec((None, FFN_CONV, tn), lambda i, j: (layer, 0, j)),
            pl.BlockSpec((None, FFN_CONV, tn), lambda i, j: (layer, 0, j + nj)),
        ],
        out_specs=pl.BlockSpec((tm, tn), lambda i, j: (i, j)),
        out_shape=jax.ShapeDtypeStruct((m, dff), BF16),
        compiler_params=_cparams(("parallel", "arbitrary")),
        name="ffn_up",
    )(u2p, w_up, w_up, conv_w, conv_w)


class QKPart(NamedTuple):
    q_arr: jax.Array
    q_col: int
    k_arr: jax.Array
    k_col: int
    k_per_head: bool
    wq: jax.Array
    wk: jax.Array
    n_valid: int
    tabs: Optional[tuple]


def _swap_halves(y, n_valid):
    half = n_valid // 2
    if n_valid == LANES:
        return pltpu.roll(y, half, 1)
    lane = lax.broadcasted_iota(jnp.int32, (1, LANES), 1)
    return jnp.where(lane < half, pltpu.roll(y, LANES - half, 1), pltpu.roll(y, half, 1))


def _norm_rope(x, w, n_valid, tabs):
    if n_valid < LANES:
        lane = lax.broadcasted_iota(jnp.int32, (1, LANES), 1)
        x = jnp.where(lane < n_valid, x, 0.0)
    y = x * lax.rsqrt(jnp.sum(x * x, axis=-1, keepdims=True) * (1.0 / n_valid) + EPS) * w
    if tabs is not None:
        y = y * tabs[0] + _swap_halves(y, n_valid) * tabs[1]
    return y


def _attn_kernel(*refs, n_parts, n_valid, roped, segs, tq, q_sub, scale):
    n_seg = len(segs)
    it = iter(refs)
    q_refs = [next(it) for _ in range(n_parts)]
    k_refs = [[next(it) for _ in range(n_parts)] for _ in range(n_seg)]
    v_refs = [next(it) for _ in range(n_seg)]
    wq_refs = [next(it) for _ in range(n_parts)]
    wk_refs = [next(it) for _ in range(n_parts)]
    tab_refs = [(next(it), next(it)) if roped[p] else None for p in range(n_parts)]
    o_ref = next(it)
    k_scr = [next(it) for _ in range(n_seg)]
    v_scr = [next(it) for _ in range(n_seg)]
    t = pl.program_id(3)

    @pl.when(jnp.logical_and(pl.program_id(2) == 0, t == 0))
    def _():
        for si, seg in enumerate(segs):
            for p in range(n_parts):
                tabs = None
                if seg == "lat" and roped[p]:
                    tabs = (tab_refs[p][0][...], tab_refs[p][1][...])
                y = _norm_rope(k_refs[si][p][...].astype(F32), wk_refs[p][...], n_valid[p], tabs)
                k_scr[si][:, p * LANES:(p + 1) * LANES] = y.astype(BF16)
            rows = v_scr[si].shape[0]
            lane = lax.broadcasted_iota(jnp.int32, (rows, LANES), 1)
            v_scr[si][:, :LANES] = v_refs[si][...]
            v_scr[si][:, LANES:] = jnp.where(lane == 0, 1.0, 0.0).astype(BF16)

    parts = []
    for p in range(n_parts):
        tabs = None
        if roped[p]:
            r = pl.ds(pl.multiple_of(t * tq, tq), tq)
            tabs = (tab_refs[p][0][r, :], tab_refs[p][1][r, :])
        parts.append(_norm_rope(q_refs[p][...].astype(F32), wq_refs[p][...], n_valid[p], tabs) * scale)
    q = jnp.concatenate(parts, axis=1).astype(BF16)
    dn_t = (((1,), (1,)), ((), ()))
    sub = min(tq, q_sub)
    for r0 in range(0, tq, sub):
        qs = q[r0:r0 + sub]
        ss = [lax.dot_general(qs, k_scr[si][...], dn_t, preferred_element_type=F32) for si in range(n_seg)]
        m = jnp.max(ss[0], axis=-1, keepdims=True)
        for s in ss[1:]:
            m = jnp.maximum(m, jnp.max(s, axis=-1, keepdims=True))
        o_ext = None
        for si, s in enumerate(ss):
            pv = jnp.dot(jnp.exp((s - m).astype(BF16)), v_scr[si][...], preferred_element_type=F32)
            o_ext = pv if o_ext is None else o_ext + pv
        o_ref[r0:r0 + sub, :] = (o_ext[:, :LANES] / o_ext[:, LANES:LANES + 1]).astype(o_ref.dtype)


def attention_mix(parts, v_arr, v_col, n_heads, n_kv, dk, q_stream, q_sub):
    g = n_heads // n_kv
    ctx_blk0 = N_LAT // CTX_LEN
    lat = q_stream == "lat"
    tq = min(TQ, SEQ) if lat else CTX_LEN
    nt = SEQ // tq if lat else 1
    segs = ("ctx", "lat") if lat else ("ctx",)
    roped = tuple(lat and part.tabs is not None for part in parts)
    seg_rows = {"ctx": CTX_LEN, "lat": SEQ}
    seg_blk0 = {"ctx": ctx_blk0, "lat": 0}
    q_blk0 = 0 if lat else ctx_blk0

    def slab(rows, blk0, col, per_head):
        return pl.BlockSpec((rows, LANES), lambda b, hk, gi, t: (blk0 + b, col + (hk if per_head else 0)))

    in_specs, args = [], []
    for part in parts:
        in_specs.append(pl.BlockSpec(
            (tq, LANES), functools.partial(lambda b, hk, gi, t, col: (q_blk0 + b * nt + t, col + hk * g + gi),
                                           col=part.q_col)))
        args.append(part.q_arr)
    for seg in segs:
        for part in parts:
            in_specs.append(slab(seg_rows[seg], seg_blk0[seg], part.k_col, part.k_per_head))
            args.append(part.k_arr)
    for seg in segs:
        in_specs.append(slab(seg_rows[seg], seg_blk0[seg], v_col, True))
        args.append(v_arr)
    const = lambda b, hk, gi, t: (0, 0)
    for w in [part.wq for part in parts] + [part.wk for part in parts]:
        in_specs.append(pl.BlockSpec((1, LANES), const))
        args.append(w.reshape(1, LANES))
    for part, r in zip(parts, roped):
        if r:
            in_specs += [pl.BlockSpec((SEQ, LANES), const)] * 2
            args += list(part.tabs)
    n_parts = len(parts)
    scratch = [pltpu.VMEM((seg_rows[seg], n_parts * LANES), BF16) for seg in segs]
    scratch += [pltpu.VMEM((seg_rows[seg], 2 * LANES), BF16) for seg in segs]
    q_rows = N_LAT if lat else N_CTX
    return pl.pallas_call(
        functools.partial(_attn_kernel, n_parts=n_parts, n_valid=tuple(part.n_valid for part in parts), roped=roped,
                          segs=segs, tq=tq, q_sub=q_sub, scale=dk ** -0.5),
        grid=(BATCH, n_kv, g, nt),
        in_specs=in_specs,
        out_specs=pl.BlockSpec((tq, LANES), lambda b, hk, gi, t: (b * nt + t, hk * g + gi)),
        out_shape=jax.ShapeDtypeStruct((q_rows, n_heads * LANES), BF16),
        scratch_shapes=scratch,
        compiler_params=_cparams(("arbitrary",) * 4),
        name="attention",
    )(*args)


def _col_from_row(row, eye):
    return jnp.sum(jnp.where(eye, row, 0.0), axis=1, keepdims=True)


def _rows(idx, size):
    return pl.ds(pl.multiple_of(idx * size, size), size)


def _seq_conv_silu(x, cw):
    zero = jnp.zeros((1, x.shape[1]), F32)
    t = lax.broadcasted_iota(jnp.int32, (x.shape[0], 1), 0)
    c = _conv3_rows(x, zero, zero, cw, t == 0, t == x.shape[0] - 1)
    return c * jax.nn.sigmoid(c)


N_LEVELS = int(math.log2(CHUNK))
MASK_SAME, MASK_EYE, MASK_BE = 0, 1, 2
MASK_LEVEL = 4
N_MASKS = MASK_LEVEL + 2 * N_LEVELS


def _gdn_init_masks(mask_ref):
    n = GROUP
    ii = lax.broadcasted_iota(jnp.int32, (n, n), 0)
    jj = lax.broadcasted_iota(jnp.int32, (n, n), 1)
    same = (ii // CHUNK) == (jj // CHUNK)

    def put(idx, m):
        mask_ref[idx] = jnp.where(m, 1.0, 0.0)

    put(MASK_SAME, same)
    put(MASK_EYE, ii == jj)
    put(MASK_BE, jnp.logical_and(same, jj <= ii))
    put(MASK_BE + 1, jnp.logical_and(same, jj >= ii))
    for level in range(N_LEVELS):
        s = 1 << level
        blk = (ii // (2 * s)) == (jj // (2 * s))
        ih = (ii // s) & 1
        jh = (jj // s) & 1
        put(MASK_LEVEL + level, jnp.logical_and(blk, jnp.logical_and(ih == 1, jh == 0)))
        put(MASK_LEVEL + N_LEVELS + level, jnp.logical_and(blk, jnp.logical_and(ih == 0, jh == 1)))


def _gdn_prep_dir(q, k, v, gates, d, mask_ref):
    def mask(idx):
        return mask_ref[idx] > 0.5

    eye = mask(MASK_EYE)
    same = mask(MASK_SAME)
    be = mask(MASK_BE + d)
    be_t = mask(MASK_BE + 1 - d)
    k16 = k.astype(BF16)
    dn_t = (((1,), (1,)), ((), ()))
    kq = lax.dot_general(jnp.concatenate([k16, q.astype(BF16)], axis=0), k16, dn_t, preferred_element_type=F32)
    kk, qk0 = kq[:k.shape[0]], kq[k.shape[0]:]
    g_row, b_row = gates[d:d + 1, :], gates[2 + d:3 + d, :]
    g_col = _col_from_row(g_row, eye)
    b_col = _col_from_row(b_row, eye)
    gc_col = jnp.sum(jnp.where(be, g_row, 0.0), axis=1, keepdims=True)
    gc_row = jnp.sum(jnp.where(be_t, g_col, 0.0), axis=0, keepdims=True)
    tot_col = jnp.sum(jnp.where(same, g_row, 0.0), axis=1, keepdims=True)
    decay = jnp.where(be, jnp.exp(jnp.where(be, gc_col - gc_row, 0.0)), 0.0)
    mm = jnp.where(jnp.logical_and(be, jnp.logical_not(eye)), b_col * kk * decay, 0.0)
    qk = qk0 * decay
    t_inv = jnp.where(eye, 1.0, 0.0) - jnp.where(mask(MASK_LEVEL + d * N_LEVELS), mm, 0.0)
    for level in range(1, N_LEVELS):
        mc = jnp.where(mask(MASK_LEVEL + d * N_LEVELS + level), mm, 0.0)
        tb = t_inv.astype(BF16)
        y = jnp.dot(tb, mc.astype(BF16), preferred_element_type=F32)
        t_inv = t_inv - jnp.dot(y.astype(BF16), tb, preferred_element_type=F32)
    egc = jnp.exp(gc_col)
    kb = k * b_col
    rhs = jnp.concatenate([v * b_col, kb * egc], axis=1).astype(BF16)
    uw = jnp.dot(t_inv.astype(BF16), rhs, preferred_element_type=F32)
    dv = v.shape[1]
    return uw[:, :dv], uw[:, dv:], q * egc, k * jnp.exp(tot_col - gc_col), qk, jnp.exp(tot_col)


def _gdn_stream(x_refs, cw_refs, gates_ref, z_ref, onorm_ref, y_ref, states, scr):
    q_ref, k_ref, v_ref = x_refs
    mask_ref, qs_ref, ks_ref, vs_ref, o_ref = scr[:5]
    slots = (scr[5:9], scr[9:13])
    t_len = q_ref.shape[0]
    n_groups = t_len // GROUP
    cpg = GROUP // CHUNK
    dk = q_ref.shape[1]

    def l2(x):
        return x * lax.rsqrt(jnp.sum(x * x, axis=-1, keepdims=True) + EPS)

    rows_all = pl.ds(0, t_len)
    qs_ref[rows_all, :] = l2(_seq_conv_silu(q_ref[...].astype(F32), cw_refs[0][...])) * dk ** -0.5
    ks_ref[rows_all, :] = l2(_seq_conv_silu(k_ref[...].astype(F32), cw_refs[1][...]))
    vs_ref[rows_all, :] = _seq_conv_silu(v_ref[...].astype(F32), cw_refs[2][...])

    def load_group(g):
        r = _rows(g, GROUP)
        return qs_ref[r, :], ks_ref[r, :], vs_ref[r, :], gates_ref[g]

    def prep(slot, d, inputs):
        u_ref, wq_ref, lt_ref, gl_ref = slot
        u, w, qg, kg, qk, glast = _gdn_prep_dir(*inputs, d, mask_ref)
        u_ref[d] = u
        kg_t = kg.T
        for c in range(cpg):
            rc = slice(c * CHUNK, (c + 1) * CHUNK)
            wq_ref[d, c] = jnp.concatenate([w[rc], qg[rc]], axis=0).astype(BF16)
            lt_ref[d, c] = jnp.concatenate([kg_t[:, rc], qk[rc, rc]], axis=0).astype(BF16)
            gl_ref[d, c] = jnp.broadcast_to(glast[c * CHUNK:c * CHUNK + 8, :], (8, LANES))

    def scan_group(slot, d, state):
        u_ref, wq_ref, lt_ref, gl_ref = slot
        u = u_ref[d]
        order = range(cpg) if d == 0 else range(cpg - 1, -1, -1)
        o_parts = [None] * cpg
        for c in order:
            rc = slice(c * CHUNK, (c + 1) * CHUNK)
            ws = jnp.dot(wq_ref[d, c], state.astype(BF16), preferred_element_type=F32)
            v_new = u[rc] - ws[:CHUNK]
            r = jnp.dot(lt_ref[d, c], v_new.astype(BF16), preferred_element_type=F32)
            state = state * gl_ref[d, c][0:1, :] + r[:dk]
            o_parts[c] = ws[CHUNK:] + r[dk:]
        return jnp.concatenate(o_parts, axis=0), state

    def step(t, carry, cur, nxt, accumulate, prefetch):
        s_f, s_b = carry
        g_b = n_groups - 1 - t
        if prefetch:
            in_f, in_b = load_group(t + 1), load_group(g_b - 1)
            prep(nxt, 0, in_f)
            prep(nxt, 1, in_b)
        o_f, s_f = scan_group(cur, 0, s_f)
        o_b, s_b = scan_group(cur, 1, s_b)
        if accumulate:
            o_ref[_rows(t, GROUP), :] += o_f
            o_ref[_rows(g_b, GROUP), :] += o_b
        else:
            o_ref[_rows(t, GROUP), :] = o_f
            o_ref[_rows(g_b, GROUP), :] = o_b
        return s_f, s_b

    in_f, in_b = load_group(0), load_group(n_groups - 1)
    prep(slots[0], 0, in_f)
    prep(slots[0], 1, in_b)
    if n_groups == 1:
        o_f, s_f = scan_group(slots[0], 0, states[0])
        o_b, s_b = scan_group(slots[0], 1, states[1])
        o_ref[pl.ds(0, GROUP), :] = o_f + o_b
        states = (s_f, s_b)
    else:
        n_pairs = n_groups // 2
        assert n_groups == 2 * n_pairs and n_pairs % 2 == 0

        def pair(p, carry, accumulate, last=False):
            carry = step(2 * p, carry, slots[0], slots[1], accumulate, True)
            return step(2 * p + 1, carry, slots[1], slots[0], accumulate, not last)

        states = lax.fori_loop(0, n_pairs // 2, lambda p, c: pair(p, c, False), states)
        states = lax.fori_loop(n_pairs // 2, n_pairs - 1, lambda p, c: pair(p, c, True), states)
        states = pair(n_pairs - 1, states, True, last=True)

    o = o_ref[rows_all, :]
    o = o * lax.rsqrt(jnp.mean(o * o, axis=-1, keepdims=True) + EPS) * onorm_ref[...]
    z = z_ref[...].astype(F32)
    y_ref[...] = (o * (z * jax.nn.sigmoid(z))).astype(y_ref.dtype)
    return states


def _gdn_kernel(qc, kc, vc, zc, gc, ql, kl, vl, zl, gl, cwq, cwk, cwv, onorm, yc_ref, yl_ref, *scr):
    @pl.when(jnp.logical_and(pl.program_id(0) == 0, pl.program_id(1) == 0))
    def _():
        _gdn_init_masks(scr[0])

    cws = (cwq, cwk, cwv)
    zero = jnp.zeros((kc.shape[1], vc.shape[1]), F32)
    states = _gdn_stream((qc, kc, vc), cws, gc.at[0, 0], zc, onorm, yc_ref, (zero, zero), scr)
    _gdn_stream((ql, kl, vl), cws, gl.at[0, 0], zl, onorm, yl_ref, states, scr)


def gdn_mix(p, col_q, col_z, gates_c, gates_l, conv_w, out_norm):
    h, dk, dv = H_B, DK_B, DV_B
    assert dk == LANES and dv == LANES and col_q % LANES == 0 and col_z % LANES == 0
    cq, cz = col_q // LANES, col_z // LANES
    ctx_blk0 = N_LAT // CTX_LEN
    cpg = GROUP // CHUNK

    def slab(t, row0, col0):
        return pl.BlockSpec((t, LANES), lambda bi, hi: (row0 + bi, col0 + hi))

    def stream_specs(t, row0, gates):
        return [slab(t, row0, cq), slab(t, row0, cq + h), slab(t, row0, cq + 2 * h), slab(t, row0, cz),
                pl.BlockSpec((1, 1) + gates.shape[2:], lambda bi, hi: (bi, hi, 0, 0, 0))]

    in_specs = stream_specs(CTX_LEN, ctx_blk0, gates_c) + stream_specs(SEQ, 0, gates_l)
    in_specs += [pl.BlockSpec((SHORT_CONV, LANES), functools.partial(lambda bi, hi, off: (0, off + hi), off=off))
                 for off in (0, h, 2 * h)]
    in_specs.append(pl.BlockSpec((1, dv), lambda bi, hi: (0, 0)))
    slot = [
        pltpu.VMEM((2, GROUP, dv), F32),
        pltpu.VMEM((2, cpg, 2 * CHUNK, dk), BF16),
        pltpu.VMEM((2, cpg, dk + CHUNK, CHUNK), BF16),
        pltpu.VMEM((2, cpg, 8, LANES), F32),
    ]
    scratch = [
        pltpu.VMEM((N_MASKS, GROUP, GROUP), F32),
        pltpu.VMEM((SEQ, dk), F32), pltpu.VMEM((SEQ, dk), F32), pltpu.VMEM((SEQ, dv), F32),
        pltpu.VMEM((SEQ, dv), F32),
    ] + slot + slot
    return pl.pallas_call(
        _gdn_kernel,
        grid=(BATCH, h),
        in_specs=in_specs,
        out_specs=[pl.BlockSpec((CTX_LEN, dv), lambda bi, hi: (bi, hi)),
                   pl.BlockSpec((SEQ, dv), lambda bi, hi: (bi, hi))],
        out_shape=[jax.ShapeDtypeStruct((N_CTX, h * dv), BF16), jax.ShapeDtypeStruct((N_LAT, h * dv), BF16)],
        scratch_shapes=scratch,
        compiler_params=_cparams(("arbitrary", "arbitrary")),
        name="gdn_mix",
    )(p, p, p, p, gates_c, p, p, p, p, gates_l, conv_w, conv_w, conv_w, out_norm.reshape(1, dv))


def _tri_masks(n, rev):
    ii = lax.broadcasted_iota(jnp.int32, (n, n), 0)
    jj = lax.broadcasted_iota(jnp.int32, (n, n), 1)
    eye = ii == jj
    if rev:
        be, be_t = jj >= ii, ii >= jj
    else:
        be, be_t = jj <= ii, ii <= jj
    return eye, be, be_t


def _mlstm_chunk(q, k, v, i_row, f_row, carry, c_ref, rev):
    n_vec, m = carry
    n = q.shape[0]
    eye, be, be_t = _tri_masks(n, rev)
    f_col = _col_from_row(f_row, eye)
    i_col = _col_from_row(i_row, eye)
    b_col = jnp.sum(jnp.where(be, f_row, 0.0), axis=1, keepdims=True)
    b_row = jnp.sum(jnp.where(be_t, f_col, 0.0), axis=0, keepdims=True)
    b_tot = jnp.sum(f_row, axis=1, keepdims=True)
    d = jnp.where(be, b_col - b_row + i_row, -jnp.inf)
    inter = b_col + m
    m_t = jnp.maximum(inter, jnp.max(d, axis=1, keepdims=True))
    a_inter = jnp.exp(inter - m_t)
    qb = q.astype(BF16)
    kbf = k.astype(BF16)
    s = lax.dot_general(qb, kbf, (((1,), (1,)), ((), ())), preferred_element_type=F32)
    p = jnp.exp(d - m_t) * s
    c_mat = c_ref[...]
    num = a_inter * jnp.dot(qb, c_mat.astype(BF16), preferred_element_type=F32) + jnp.dot(
        p.astype(BF16), v.astype(BF16), preferred_element_type=F32)
    den = a_inter * jnp.sum(q * n_vec, axis=1, keepdims=True) + jnp.sum(p, axis=1, keepdims=True)
    h = num / jnp.maximum(jnp.abs(den), jnp.exp(-m_t))
    w_end_row = b_tot - b_row + i_row
    w_end_col = b_tot - b_col + i_col
    m_new = jnp.maximum(b_tot + m, jnp.max(w_end_row, axis=1, keepdims=True))
    g_old = jnp.exp(b_tot + m - m_new)
    gs_col = jnp.exp(w_end_col - m_new)
    upd = lax.dot_general(kbf, (gs_col * v).astype(BF16), (((0,), (0,)), ((), ())), preferred_element_type=F32)
    c_ref[...] = g_old * c_mat + upd
    n_new = g_old * n_vec + jnp.sum(gs_col * k, axis=0, keepdims=True)
    return h, (n_new, m_new)


def _mlstm_stream(q_ref, k_ref, v_ref, gates_ref, opre_ref, onorm_ref, y_ref, carries, o_ref, c_refs, unroll):
    t_len = q_ref.shape[0]
    n_chunks = t_len // CHUNK
    half = n_chunks // 2
    assert n_chunks == 2 * half
    scale = q_ref.shape[1] ** -0.5

    def chunk(c, carry, rev):
        r = _rows(c, CHUNK)
        gates = gates_ref[c]
        d = 1 if rev else 0
        return _mlstm_chunk(q_ref[r, :].astype(F32) * scale, k_ref[r, :].astype(F32), v_ref[r, :],
                            gates[d:d + 1, :], gates[2 + d:3 + d, :], carry, c_refs[d], rev)

    def step(t, carry, accumulate):
        cf, cb = carry
        c_b = n_chunks - 1 - t
        o_f, cf = chunk(t, cf, False)
        o_b, cb = chunk(c_b, cb, True)
        if accumulate:
            o_ref[_rows(t, CHUNK), :] += o_f
            o_ref[_rows(c_b, CHUNK), :] += o_b
        else:
            o_ref[_rows(t, CHUNK), :] = o_f
            o_ref[_rows(c_b, CHUNK), :] = o_b
        return cf, cb

    carries = lax.fori_loop(0, half, lambda t, c: step(t, c, False), carries, unroll=unroll)
    carries = lax.fori_loop(half, n_chunks, lambda t, c: step(t, c, True), carries, unroll=unroll)
    o = o_ref[pl.ds(0, t_len), :]
    o = o * lax.rsqrt(jnp.mean(o * o, axis=-1, keepdims=True) + EPS) * onorm_ref[0]
    y_ref[...] = (o * jax.nn.sigmoid(opre_ref[...].astype(F32))).astype(y_ref.dtype)
    return carries


def _mlstm_kernel(qc, kc, vc, oc, gc, ql, kl, vl, ol, gl, onorm, yc_ref, yl_ref, o_ref, cf_ref, cb_ref, *, unroll):
    cf_ref[...] = jnp.zeros(cf_ref.shape, F32)
    cb_ref[...] = jnp.zeros(cb_ref.shape, F32)
    init = (jnp.zeros((1, kc.shape[1]), F32), jnp.zeros((1, 1), F32))
    c_refs = (cf_ref, cb_ref)
    carries = _mlstm_stream(qc, kc, vc, gc.at[0, 0], oc, onorm, yc_ref, (init, init), o_ref, c_refs, unroll)
    _mlstm_stream(ql, kl, vl, gl.at[0, 0], ol, onorm, yl_ref, carries, o_ref, c_refs, unroll)


def mlstm_mix(p, col_q, col_k, col_v, col_o, gates_c, gates_l, out_norm, *, unroll=2):
    h, dk, dv = H_C, DK_C, DV_C
    assert col_q % dk == 0 and col_k % dk == 0 and col_v % dv == 0 and col_o % dv == 0
    ctx_blk0 = N_LAT // CTX_LEN

    def slab(t, row0, width, col):
        c0 = col // width
        return pl.BlockSpec((t, width), lambda bi, hi: (row0 + bi, c0 + hi))

    def stream_specs(t, row0, gates):
        return [slab(t, row0, dk, col_q), slab(t, row0, dk, col_k), slab(t, row0, dv, col_v), slab(t, row0, dv, col_o),
                pl.BlockSpec((1, 1) + gates.shape[2:], lambda bi, hi: (bi, hi, 0, 0, 0))]

    in_specs = stream_specs(CTX_LEN, ctx_blk0, gates_c) + stream_specs(SEQ, 0, gates_l)
    in_specs.append(pl.BlockSpec((1, 1, dv), lambda bi, hi: (hi, 0, 0)))
    return pl.pallas_call(
        functools.partial(_mlstm_kernel, unroll=unroll),
        grid=(BATCH, h),
        in_specs=in_specs,
        out_specs=[pl.BlockSpec((CTX_LEN, dv), lambda bi, hi: (bi, hi)),
                   pl.BlockSpec((SEQ, dv), lambda bi, hi: (bi, hi))],
        out_shape=[jax.ShapeDtypeStruct((N_CTX, h * dv), BF16), jax.ShapeDtypeStruct((N_LAT, h * dv), BF16)],
        scratch_shapes=[pltpu.VMEM((SEQ, dv), F32), pltpu.VMEM((dk, dv), F32), pltpu.VMEM((dk, dv), F32)],
        compiler_params=_cparams(("parallel", "parallel")),
        name="mlstm_mix",
    )(p, p, p, p, gates_c, p, p, p, p, gates_l, out_norm.reshape(h, 1, dv))


def _axial_rope(n_rows, rot_dim):
    r, col = jnp.meshgrid(jnp.arange(n_rows), jnp.arange(GRID_W), indexing="ij")
    r = r.reshape(-1).astype(F32)
    col = col.reshape(-1).astype(F32)
    n_freq = rot_dim // 4
    inv = ROPE_THETA ** (-jnp.arange(n_freq, dtype=F32) / n_freq)
    ang = jnp.concatenate([r[:, None] * inv, col[:, None] * inv], axis=-1)
    return jnp.cos(ang), jnp.sin(ang)


def _deinterleave(n):
    return np.concatenate([np.arange(0, n, 2), np.arange(1, n, 2)])


def _pad_lanes(a):
    pad = LANES - a.shape[-1]
    return a if pad == 0 else jnp.concatenate([a, jnp.zeros(a.shape[:-1] + (pad,), a.dtype)], axis=-1)


def _rope_tables(n_rows, rot_dim):
    cos, sin = _axial_rope(n_rows, rot_dim)
    return _pad_lanes(jnp.concatenate([cos, cos], axis=1)), _pad_lanes(jnp.concatenate([-sin, sin], axis=1))


def _gate_rows(g, width):
    b, t, h = g[0].shape
    rows = [a.transpose(0, 2, 1).reshape(b, h, t // width, 1, width) for a in g]
    rows += [jnp.zeros_like(rows[0])] * (8 - len(rows))
    return jnp.concatenate(rows, axis=3)


def _streams(a):
    c = a.shape[-1]
    return a[N_LAT:].reshape(BATCH, CTX_LEN, c), a[:N_LAT].reshape(BATCH, SEQ, c)


EV_CQ, EV_CKV, EV_QKV, EV_Z = 0, 1024, 1536, 7680
P_EVEN_MAIN = 9728
OD_MQ, OD_MK, OD_MV, OD_MO, OD_GQ, OD_GK, OD_GV = 0, 1024, 2048, 4096, 6144, 8192, 8704
P_ODD_MAIN = 9216


def _split_even(w_in):
    cq, ckv, kpe, qkv, z, ab = jnp.split(w_in, np.cumsum([1024, 512, 64, 6144, 2048]).tolist(), axis=1)
    kpe = kpe[:, _deinterleave(ROPE_A)]
    return jnp.concatenate([cq, ckv, qkv, z, kpe, ab], axis=1)


def _split_odd(w_in):
    mq, mk, mv, mo, mg, gq, gk, gv = jnp.split(
        w_in, np.cumsum([1024, 1024, 2048, 2048, 16, 2048, 512]).tolist(), axis=1)
    perm = _deinterleave(HD_D)
    per_head = lambda w: w.reshape(w.shape[0], -1, HD_D)[:, :, perm].reshape(w.shape)
    return jnp.concatenate([mq, mk, mv, mo, per_head(gq), per_head(gk), gv, _pad_lanes(mg)], axis=1)


def _mla_weights(w_qb, w_kvb):
    wq = w_qb.reshape(Q_LORA, H_A, NOPE_A + ROPE_A)
    q_pe = _pad_lanes(wq[:, :, NOPE_A:][:, :, _deinterleave(ROPE_A)])
    wq = jnp.concatenate([wq[:, :, :NOPE_A].reshape(Q_LORA, -1), q_pe.reshape(Q_LORA, -1)], axis=1)
    wkv = w_kvb.reshape(KV_LORA, H_A, NOPE_A + V_A)
    wkv = jnp.concatenate([wkv[:, :, :NOPE_A].reshape(KV_LORA, -1), wkv[:, :, NOPE_A:].reshape(KV_LORA, -1)], axis=1)
    return wq, wkv


def _even_mixers(p, pg, need_ctx, rope, w_qb, q_lora_norm, w_kvb, kv_lora_norm, q_head_norm, k_head_norm,
                 gdn_conv, a_log, dt_bias, out_norm):
    qn = rms_rows(p, EV_CQ // Q_LORA, Q_LORA, q_lora_norm)
    kvn = rms_rows(p, EV_CKV // KV_LORA, KV_LORA, kv_lora_norm)
    wq, wkv = _mla_weights(w_qb.astype(BF16), w_kvb.astype(BF16))
    q_all = matmul(qn, wq, tn=1024, out_dtype=BF16)
    kv_all = matmul(kvn, wkv, tn=1024, out_dtype=BF16)
    perm = _deinterleave(ROPE_A)
    parts = [
        QKPart(q_all, 0, kv_all, 0, True, q_head_norm[:NOPE_A], k_head_norm[:NOPE_A], NOPE_A, None),
        QKPart(q_all, H_A, pg, 0, False, _pad_lanes(q_head_norm[NOPE_A:][perm]),
               _pad_lanes(k_head_norm[NOPE_A:][perm]), ROPE_A, rope),
    ]
    a_lat = attention_mix(parts, kv_all, H_A, H_A, H_A, NOPE_A + ROPE_A, "lat", Q_SUB_MLA)
    def gdn_gates(ab):
        ab = ab.reshape(ab.shape[0], ab.shape[1], 4, H_B)
        g = -jnp.exp(a_log)[None, None] * jax.nn.softplus(ab[:, :, :2] + dt_bias[None, None])
        beta = jax.nn.sigmoid(ab[:, :, 2:])
        return _gate_rows([g[:, :, 0], g[:, :, 1], beta[:, :, 0], beta[:, :, 1]], GROUP)

    ab_c, ab_l = _streams(pg[:, ROPE_A:ROPE_A + 4 * H_B])
    b_ctx, b_lat = gdn_mix(p, EV_QKV, EV_Z, gdn_gates(ab_c), gdn_gates(ab_l), gdn_conv, out_norm)
    if not need_ctx:
        return [a_lat, b_lat], None
    a_ctx = attention_mix(parts, kv_all, H_A, H_A, H_A, NOPE_A + ROPE_A, "ctx", Q_SUB_MLA)
    return [a_lat, b_lat], [a_ctx, b_ctx]


def _odd_mixers(p, pg, need_ctx, rope, gate_bias, out_norm, q_norm, k_norm):
    def mlstm_gates(g):
        g = g.reshape(g.shape[0], g.shape[1], 4, H_C) + gate_bias
        g = GATE_CAP * jnp.tanh(g / GATE_CAP)
        return _gate_rows([g[:, :, 0], g[:, :, 1], jax.nn.log_sigmoid(g[:, :, 2]), jax.nn.log_sigmoid(g[:, :, 3])],
                          CHUNK)

    mg_c, mg_l = _streams(pg[:, :4 * H_C])
    c_ctx, c_lat = mlstm_mix(p, OD_MQ, OD_MK, OD_MV, OD_MO, mlstm_gates(mg_c), mlstm_gates(mg_l), out_norm)
    perm = _deinterleave(HD_D)
    parts = [QKPart(p, OD_GQ // LANES, p, OD_GK // LANES, True, q_norm[perm], k_norm[perm], HD_D, rope)]
    d_lat = attention_mix(parts, p, OD_GV // LANES, H_D, KV_D, HD_D, "lat", Q_SUB_GQA)
    if not need_ctx:
        return [c_lat, d_lat], None
    d_ctx = attention_mix(parts, p, OD_GV // LANES, H_D, KV_D, HD_D, "ctx", Q_SUB_GQA)
    return [c_lat, d_lat], [c_ctx, d_ctx]


def kernel(x, c, ctx, c_ctx, mod_down, mod_up, mod_bias, norm_mix, norm_ffn, ffn_up, ffn_conv, ffn_down,
           ev_w_in, ev_w_out, mla_w_qb, mla_q_lora_norm, mla_w_kvb, mla_kv_lora_norm, mla_q_head_norm,
           mla_k_head_norm, gdn_conv, gdn_a_log, gdn_dt_bias, gdn_out_norm, od_w_in, od_w_out,
           mlstm_gate_bias, mlstm_out_norm, gqa_q_norm, gqa_k_norm):
    n_rows = SEQ // GRID_W
    rope_a = _rope_tables(n_rows, ROPE_A)
    rope_d = _rope_tables(n_rows, HD_D)
    cond = jnp.concatenate([c, c_ctx[None, :], jnp.zeros((MOD_ROWS - BATCH - 1, D_MODEL), F32)], axis=0)
    mods = adaln_all(cond, mod_down, mod_up, mod_bias)
    xs = jnp.concatenate([x.reshape(N_LAT, D_MODEL), ctx.reshape(N_CTX, D_MODEL)], axis=0)
    ffn_up16, ffn_down16 = ffn_up.astype(BF16), ffn_down.astype(BF16)
    w_out16 = (ev_w_out.astype(BF16), od_w_out.astype(BF16))
    for layer in range(DEPTH):
        need_ctx = layer < DEPTH - 1
        j = layer // 2
        mod = mods[layer].reshape(MOD_ROWS, 1, 6 * D_MODEL)
        u = norm_mod(xs, norm_mix[layer], mod, 0)
        w_in = (ev_w_in if layer % 2 == 0 else od_w_in)[j].astype(BF16)
        w_in = _split_even(w_in) if layer % 2 == 0 else _split_odd(w_in)
        p, pg = matmul(u, w_in, tn=1024, out_dtype=BF16, f32_tail=True)
        if layer % 2 == 0:
            y, y_ctx = _even_mixers(p, pg, need_ctx, rope_a, mla_w_qb[j], mla_q_lora_norm[j],
                                    mla_w_kvb[j], mla_kv_lora_norm[j], mla_q_head_norm[j], mla_k_head_norm[j],
                                    gdn_conv[j], gdn_a_log[j], gdn_dt_bias[j], gdn_out_norm[j])
        else:
            y, y_ctx = _odd_mixers(p, pg, need_ctx, rope_d, mlstm_gate_bias[j], mlstm_out_norm[j], gqa_q_norm[j],
                                   gqa_k_norm[j])
        xs = matmul(y, w_out16[layer % 2], tn=512, out_dtype=F32, res=xs, mod=mod, gate_chunk=2,
                    ctx_parts=y_ctx, w_layer=j)
        u2 = norm_mod(xs, norm_ffn[layer], mod, 3, pad=HALO)
        hmid = ffn_up_proj(u2, ffn_up16, ffn_conv, layer)
        xs = matmul(hmid, ffn_down16, tn=256, out_dtype=F32, res=xs, mod=mod, gate_chunk=5,
                    single_buffer_a=True, w_layer=layer)
    return xs.reshape(BATCH, SEQ, D_MODEL)
```

```python
import functools
import math
from typing import NamedTuple, Optional

import numpy as np
import jax
import jax.numpy as jnp
from jax import lax
from jax.experimental import pallas as pl
from jax.experimental.pallas import tpu as pltpu

F32 = jnp.float32
BF16 = jnp.bfloat16

D_MODEL = 4096
BATCH = 4
SEQ = 4096
DEPTH = 4
CTX_LEN = 256
GRID_W = 64
H_A, Q_LORA, KV_LORA, NOPE_A, ROPE_A, V_A = 16, 1024, 512, 128, 64, 128
H_B, DK_B, DV_B, SHORT_CONV = 16, 128, 128, 3
H_C, DK_C, DV_C, GATE_CAP = 4, 256, 512, 15.0
H_D, KV_D, HD_D = 16, 4, 128
D_FF, FFN_CONV = 11008, 3
CHUNK = 64
ROPE_THETA = 10000.0
EPS = 1e-6

N_LAT = BATCH * SEQ
N_CTX = BATCH * CTX_LEN
MOD_ROWS = 8
LANES = 128

VMEM_LIMIT_V7X = 56 * 1024 * 1024

TM = 1024
TN_PROJ = 1024
TN_OUT = 512
TN_DOWN = 256
TN_FFN = 256
TM_NORM = 512
TM_RMS = 512
TQ = 1024
Q_SUB_MLA = 256
Q_SUB_GQA = 128
GROUP = 256


def _cparams(sem):
    return pltpu.CompilerParams(dimension_semantics=sem, vmem_limit_bytes=VMEM_LIMIT_V7X)


def _mod_row(i, tm):
    return jnp.minimum((i * tm) // SEQ, BATCH)


def _adaln_kernel(cond_ref, down_ref, up_ref, bias_ref, o_ref):
    cond = cond_ref[...]
    a = cond * jax.nn.sigmoid(cond)
    t = jnp.dot(a, down_ref[0], preferred_element_type=F32, precision=lax.Precision.HIGHEST)
    o = jnp.dot(t, up_ref[0], preferred_element_type=F32, precision=lax.Precision.HIGHEST)
    o_ref[0] = o + bias_ref[0]


def adaln_all(cond, mod_down, mod_up, mod_bias):
    depth, d, rank = mod_down.shape
    n = mod_up.shape[-1]
    tn = d
    return pl.pallas_call(
        _adaln_kernel,
        grid=(depth, n // tn),
        in_specs=[
            pl.BlockSpec((MOD_ROWS, d), lambda l, j: (0, 0)),
            pl.BlockSpec((1, d, rank), lambda l, j: (l, 0, 0)),
            pl.BlockSpec((1, rank, tn), lambda l, j: (l, 0, j)),
            pl.BlockSpec((1, 1, tn), lambda l, j: (l, 0, j)),
        ],
        out_specs=pl.BlockSpec((1, MOD_ROWS, tn), lambda l, j: (l, 0, j)),
        out_shape=jax.ShapeDtypeStruct((depth, MOD_ROWS, n), F32),
        compiler_params=_cparams(("arbitrary", "arbitrary")),
        name="adaln",
    )(cond, mod_down, mod_up, mod_bias.reshape(depth, 1, n))


def _norm_mod_kernel(x_ref, w_ref, sh_ref, sc_ref, o_ref):
    x = x_ref[...]
    y = x * lax.rsqrt(jnp.mean(x * x, axis=-1, keepdims=True) + EPS)
    y = y * w_ref[...]
    o_ref[...] = (y * (1.0 + sc_ref[0]) + sh_ref[0]).astype(o_ref.dtype)


def norm_mod(xs, w, mod, shift_chunk, pad=0):
    m, d = xs.shape
    tm = TM_NORM
    out = pl.pallas_call(
        _norm_mod_kernel,
        grid=(m // tm,),
        in_specs=[
            pl.BlockSpec((tm, d), lambda i: (i, 0)),
            pl.BlockSpec((1, d), lambda i: (0, 0)),
            pl.BlockSpec((1, 1, d), lambda i: (_mod_row(i, tm), 0, shift_chunk)),
            pl.BlockSpec((1, 1, d), lambda i: (_mod_row(i, tm), 0, shift_chunk + 1)),
        ],
        out_specs=pl.BlockSpec((pl.Element(tm), pl.Element(d)),
                               lambda i: (pl.multiple_of(pad + i * tm, math.gcd(pad, tm) or tm), 0)),
        out_shape=jax.ShapeDtypeStruct((m + 2 * pad, d), BF16),
        compiler_params=_cparams(("parallel",)),
        name="norm_mod",
    )(xs, w.reshape(1, d), mod, mod)
    if pad:
        zeros = jnp.zeros((pad, d), BF16)
        out = lax.dynamic_update_slice(out, zeros, (0, 0))
        out = lax.dynamic_update_slice(out, zeros, (m + pad, 0))
    return out


def _rms_rows_kernel(x_ref, w_ref, o_ref):
    x = x_ref[...].astype(F32)
    y = x * lax.rsqrt(jnp.mean(x * x, axis=-1, keepdims=True) + EPS)
    o_ref[...] = (y * w_ref[...]).astype(o_ref.dtype)


def rms_rows(p, col_block, width, w):
    m = p.shape[0]
    tm = TM_RMS
    return pl.pallas_call(
        _rms_rows_kernel,
        grid=(m // tm,),
        in_specs=[
            pl.BlockSpec((tm, width), lambda i: (i, col_block)),
            pl.BlockSpec((1, width), lambda i: (0, 0)),
        ],
        out_specs=pl.BlockSpec((tm, width), lambda i: (i, 0)),
        out_shape=jax.ShapeDtypeStruct((m, width), BF16),
        compiler_params=_cparams(("parallel",)),
        name="rms_rows",
    )(p, w.reshape(1, width))


def _mm_kernel(*refs, n_parts, has_res, tail_off, n_lat_tiles):
    split = n_lat_tiles is not None
    n_a = n_parts * (2 if split else 1)
    a_refs, c_refs = refs[:n_parts], refs[n_parts:n_a]
    w_refs, rest = refs[n_a:n_a + n_parts], refs[n_a + n_parts:]

    def run(srcs):
        acc = jnp.dot(srcs[0][...], w_refs[0][...], preferred_element_type=F32)
        for a_ref, w_ref in zip(srcs[1:], w_refs[1:]):
            acc += jnp.dot(a_ref[...], w_ref[...], preferred_element_type=F32)
        if has_res:
            res_ref, gate_ref, o_ref = rest
            acc = res_ref[...] + gate_ref[0] * acc
        elif tail_off is not None:
            o_ref, tail_ref = rest

            @pl.when(pl.program_id(1) == pl.num_programs(1) - 1)
            def _():
                tail_ref[...] = acc[:, tail_off:tail_off + LANES]
        else:
            (o_ref,) = rest
        o_ref[...] = acc.astype(o_ref.dtype)

    if split:
        is_ctx = pl.program_id(0) >= n_lat_tiles
        pl.when(jnp.logical_not(is_ctx))(lambda: run(a_refs))
        pl.when(is_ctx)(lambda: run(c_refs))
    else:
        run(a_refs)


def matmul(a_parts, w, *, tn, out_dtype, res=None, mod=None, gate_chunk=None, single_buffer_a=False,
           f32_tail=False, ctx_parts=None, w_layer=None):
    if not isinstance(a_parts, (list, tuple)):
        a_parts = [a_parts]
    n_parts = len(a_parts)
    m_lat, kp = a_parts[0].shape
    n = w.shape[-1]
    tm, tn = TM, min(tn, n)
    assert m_lat % tm == 0 and w.shape[-2] == kp * n_parts and all(a.shape == (m_lat, kp) for a in a_parts)
    has_res = res is not None
    nj = pl.cdiv(n, tn)
    a_mode = dict(pipeline_mode=pl.Buffered(1)) if single_buffer_a else {}
    n_lat_tiles = None
    m = m_lat
    if ctx_parts is None:
        in_specs = [pl.BlockSpec((tm, kp), lambda i, j: (i, 0), **a_mode) for _ in a_parts]
        args = list(a_parts)
    else:
        n_lat_tiles = m_lat // tm
        m_ctx = ctx_parts[0].shape[0]
        assert m_ctx % tm == 0 and all(c.shape == (m_ctx, kp) for c in ctx_parts) and len(ctx_parts) == n_parts
        m = m_lat + m_ctx
        in_specs = [pl.BlockSpec((tm, kp), lambda i, j: (jnp.minimum(i, n_lat_tiles - 1), 0)) for _ in a_parts]
        in_specs += [pl.BlockSpec((tm, kp), lambda i, j: (jnp.maximum(i - n_lat_tiles, 0), 0),
                                  pipeline_mode=pl.Buffered(1)) for _ in ctx_parts]
        args = list(a_parts) + list(ctx_parts)
    if w_layer is None:
        in_specs += [pl.BlockSpec((kp, tn), functools.partial(lambda i, j, part: (part, j), part=part))
                     for part in range(n_parts)]
    else:
        in_specs += [pl.BlockSpec((None, kp, tn), functools.partial(lambda i, j, part: (w_layer, part, j), part=part))
                     for part in range(n_parts)]
    args += [w] * n_parts
    out_specs = pl.BlockSpec((tm, tn), lambda i, j: (i, j))
    out_shape = jax.ShapeDtypeStruct((m, n), out_dtype)
    tail_off = None
    if has_res:
        assert n % tn == 0 and mod.shape[-1] % tn == 0 and not f32_tail
        gate_off = gate_chunk * (n // tn)
        in_specs += [
            pl.BlockSpec((tm, tn), lambda i, j: (i, j)),
            pl.BlockSpec((1, 1, tn), lambda i, j: (_mod_row(i, tm), 0, gate_off + j)),
        ]
        args += [res, mod]
    elif f32_tail:
        tail_off = n - LANES - (nj - 1) * tn
        assert 0 <= tail_off and tail_off % LANES == 0
        out_specs = [out_specs, pl.BlockSpec((tm, LANES), lambda i, j: (i, 0))]
        out_shape = [out_shape, jax.ShapeDtypeStruct((m, LANES), F32)]
    return pl.pallas_call(
        functools.partial(_mm_kernel, n_parts=n_parts, has_res=has_res, tail_off=tail_off, n_lat_tiles=n_lat_tiles),
        grid=(m // tm, nj),
        in_specs=in_specs,
        out_specs=out_specs,
        out_shape=out_shape,
        compiler_params=_cparams(("parallel", "arbitrary")),
        name="matmul",
    )(*args)


HALO = 16


def _seq_pos(i, tm):
    r = i * tm + lax.broadcasted_iota(jnp.int32, (tm, 1), 0)
    pos = jnp.where(r < N_LAT, r & (SEQ - 1), (r - N_LAT) & (CTX_LEN - 1))
    last = jnp.where(r < N_LAT, SEQ - 1, CTX_LEN - 1)
    return pos == 0, pos == last


def _conv3_rows(h, h_prev_row, h_next_row, cw, first, last):
    tm = h.shape[0]
    t = lax.broadcasted_iota(jnp.int32, (tm, 1), 0)
    dn = jnp.where(t == 0, h_prev_row, pltpu.roll(h, 1, 0))
    dn = jnp.where(first, 0.0, dn)
    up = jnp.where(t == tm - 1, h_next_row, pltpu.roll(h, tm - 1, 0))
    up = jnp.where(last, 0.0, up)
    return cw[0:1, :] * dn + cw[1:2, :] * h + cw[2:3, :] * up


def _ffn_up_kernel(a_ref, wg_ref, wu_ref, cg_ref, cu_ref, o_ref, *, tm):
    i = pl.program_id(0)
    first, last = _seq_pos(i, tm)
    a = a_ref[...]
    ext = tm + 2 * HALO
    mid = slice(HALO, HALO + tm)

    def branch(w_ref, c_ref):
        h = jnp.dot(a, w_ref[...], preferred_element_type=F32)
        cw = c_ref[...]
        dn = jnp.where(first, 0.0, pltpu.roll(h, 1, 0)[mid])
        up = jnp.where(last, 0.0, pltpu.roll(h, ext - 1, 0)[mid])
        return cw[0:1, :] * dn + cw[1:2, :] * h[mid] + cw[2:3, :] * up

    g = branch(wg_ref, cg_ref)
    u = branch(wu_ref, cu_ref)
    o_ref[...] = (g * jax.nn.sigmoid(g) * u).astype(o_ref.dtype)


def ffn_up_proj(u2p, w_up, conv_w, layer, *, tn=TN_FFN):
    m, d = u2p.shape[0] - 2 * HALO, u2p.shape[1]
    tm = TM
    dff = w_up.shape[-1] // 2
    nj = dff // tn
    assert dff % tn == 0 and m % tm == 0
    return pl.pallas_call(
        functools.partial(_ffn_up_kernel, tm=tm),
        grid=(m // tm, nj),
        in_specs=[
            pl.BlockSpec((pl.Element(tm + 2 * HALO), pl.Element(d)), lambda i, j: (i * tm, 0)),
            pl.BlockSpec((None, d, tn), lambda i, j: (layer, 0, j)),
            pl.BlockSpec((None, d, tn), lambda i, j: (layer, 0, j + nj)),
            pl.BlockSpec((None, FFN_CONV, tn), lambda i, j: (layer, 0, j)),
            pl.BlockSpec((None, FFN_CONV, tn), lambda i, j: (layer, 0, j + nj)),
        ],
        out_specs=pl.BlockSpec((tm, tn), lambda i, j: (i, j)),
        out_shape=jax.ShapeDtypeStruct((m, dff), BF16),
        compiler_params=_cparams(("parallel", "arbitrary")),
        name="ffn_up",
    )(u2p, w_up, w_up, conv_w, conv_w)


class QKPart(NamedTuple):
    q_arr: jax.Array
    q_col: int
    k_arr: jax.Array
    k_col: int
    k_per_head: bool
    wq: jax.Array
    wk: jax.Array
    n_valid: int
    tabs: Optional[tuple]


def _swap_halves(y, n_valid):
    half = n_valid // 2
    if n_valid == LANES:
        return pltpu.roll(y, half, 1)
    lane = lax.broadcasted_iota(jnp.int32, (1, LANES), 1)
    return jnp.where(lane < half, pltpu.roll(y, LANES - half, 1), pltpu.roll(y, half, 1))


def _norm_rope(x, w, n_valid, tabs):
    if n_valid < LANES:
        lane = lax.broadcasted_iota(jnp.int32, (1, LANES), 1)
        x = jnp.where(lane < n_valid, x, 0.0)
    y = x * lax.rsqrt(jnp.sum(x * x, axis=-1, keepdims=True) * (1.0 / n_valid) + EPS) * w
    if tabs is not None:
        y = y * tabs[0] + _swap_halves(y, n_valid) * tabs[1]
    return y


def _attn_kernel(*refs, n_parts, n_valid, roped, segs, tq, q_sub, scale):
    n_seg = len(segs)
    it = iter(refs)
    q_refs = [next(it) for _ in range(n_parts)]
    k_refs = [[next(it) for _ in range(n_parts)] for _ in range(n_seg)]
    v_refs = [next(it) for _ in range(n_seg)]
    wq_refs = [next(it) for _ in range(n_parts)]
    wk_refs = [next(it) for _ in range(n_parts)]
    tab_refs = [(next(it), next(it)) if roped[p] else None for p in range(n_parts)]
    o_ref = next(it)
    k_scr = [next(it) for _ in range(n_seg)]
    v_scr = [next(it) for _ in range(n_seg)]
    t = pl.program_id(3)

    @pl.when(jnp.logical_and(pl.program_id(2) == 0, t == 0))
    def _():
        for si, seg in enumerate(segs):
            for p in range(n_parts):
                tabs = None
                if seg == "lat" and roped[p]:
                    tabs = (tab_refs[p][0][...], tab_refs[p][1][...])
                y = _norm_rope(k_refs[si][p][...].astype(F32), wk_refs[p][...], n_valid[p], tabs)
                k_scr[si][:, p * LANES:(p + 1) * LANES] = y.astype(BF16)
            rows = v_scr[si].shape[0]
            lane = lax.broadcasted_iota(jnp.int32, (rows, LANES), 1)
            v_scr[si][:, :LANES] = v_refs[si][...]
            v_scr[si][:, LANES:] = jnp.where(lane == 0, 1.0, 0.0).astype(BF16)

    parts = []
    for p in range(n_parts):
        tabs = None
        if roped[p]:
            r = pl.ds(pl.multiple_of(t * tq, tq), tq)
            tabs = (tab_refs[p][0][r, :], tab_refs[p][1][r, :])
        parts.append(_norm_rope(q_refs[p][...].astype(F32), wq_refs[p][...], n_valid[p], tabs) * scale)
    q = jnp.concatenate(parts, axis=1).astype(BF16)
    dn_t = (((1,), (1,)), ((), ()))
    sub = min(tq, q_sub)
    for r0 in range(0, tq, sub):
        qs = q[r0:r0 + sub]
        ss = [lax.dot_general(qs, k_scr[si][...], dn_t, preferred_element_type=F32) for si in range(n_seg)]
        m = jnp.max(ss[0], axis=-1, keepdims=True)
        for s in ss[1:]:
            m = jnp.maximum(m, jnp.max(s, axis=-1, keepdims=True))
        o_ext = None
        for si, s in enumerate(ss):
            pv = jnp.dot(jnp.exp((s - m).astype(BF16)), v_scr[si][...], preferred_element_type=F32)
            o_ext = pv if o_ext is None else o_ext + pv
        o_ref[r0:r0 + sub, :] = (o_ext[:, :LANES] / o_ext[:, LANES:LANES + 1]).astype(o_ref.dtype)


def attention_mix(parts, v_arr, v_col, n_heads, n_kv, dk, q_stream, q_sub):
    g = n_heads // n_kv
    ctx_blk0 = N_LAT // CTX_LEN
    lat = q_stream == "lat"
    tq = min(TQ, SEQ) if lat else CTX_LEN
    nt = SEQ // tq if lat else 1
    segs = ("ctx", "lat") if lat else ("ctx",)
    roped = tuple(lat and part.tabs is not None for part in parts)
    seg_rows = {"ctx": CTX_LEN, "lat": SEQ}
    seg_blk0 = {"ctx": ctx_blk0, "lat": 0}
    q_blk0 = 0 if lat else ctx_blk0

    def slab(rows, blk0, col, per_head):
        return pl.BlockSpec((rows, LANES), lambda b, hk, gi, t: (blk0 + b, col + (hk if per_head else 0)))

    in_specs, args = [], []
    for part in parts:
        in_specs.append(pl.BlockSpec(
            (tq, LANES), functools.partial(lambda b, hk, gi, t, col: (q_blk0 + b * nt + t, col + hk * g + gi),
                                           col=part.q_col)))
        args.append(part.q_arr)
    for seg in segs:
        for part in parts:
            in_specs.append(slab(seg_rows[seg], seg_blk0[seg], part.k_col, part.k_per_head))
            args.append(part.k_arr)
    for seg in segs:
        in_specs.append(slab(seg_rows[seg], seg_blk0[seg], v_col, True))
        args.append(v_arr)
    const = lambda b, hk, gi, t: (0, 0)
    for w in [part.wq for part in parts] + [part.wk for part in parts]:
        in_specs.append(pl.BlockSpec((1, LANES), const))
        args.append(w.reshape(1, LANES))
    for part, r in zip(parts, roped):
        if r:
            in_specs += [pl.BlockSpec((SEQ, LANES), const)] * 2
            args += list(part.tabs)
    n_parts = len(parts)
    scratch = [pltpu.VMEM((seg_rows[seg], n_parts * LANES), BF16) for seg in segs]
    scratch += [pltpu.VMEM((seg_rows[seg], 2 * LANES), BF16) for seg in segs]
    q_rows = N_LAT if lat else N_CTX
    return pl.pallas_call(
        functools.partial(_attn_kernel, n_parts=n_parts, n_valid=tuple(part.n_valid for part in parts), roped=roped,
                          segs=segs, tq=tq, q_sub=q_sub, scale=dk ** -0.5),
        grid=(BATCH, n_kv, g, nt),
        in_specs=in_specs,
        out_specs=pl.BlockSpec((tq, LANES), lambda b, hk, gi, t: (b * nt + t, hk * g + gi)),
        out_shape=jax.ShapeDtypeStruct((q_rows, n_heads * LANES), BF16),
        scratch_shapes=scratch,
        compiler_params=_cparams(("arbitrary",) * 4),
        name="attention",
    )(*args)


def _col_from_row(row, eye):
    return jnp.sum(jnp.where(eye, row, 0.0), axis=1, keepdims=True)


def _rows(idx, size):
    return pl.ds(pl.multiple_of(idx * size, size), size)


def _seq_conv_silu(x, cw):
    zero = jnp.zeros((1, x.shape[1]), F32)
    t = lax.broadcasted_iota(jnp.int32, (x.shape[0], 1), 0)
    c = _conv3_rows(x, zero, zero, cw, t == 0, t == x.shape[0] - 1)
    return c * jax.nn.sigmoid(c)


N_LEVELS = int(math.log2(CHUNK))
MASK_SAME, MASK_EYE, MASK_BE = 0, 1, 2
MASK_LEVEL = 4
N_MASKS = MASK_LEVEL + 2 * N_LEVELS


def _gdn_init_masks(mask_ref):
    n = GROUP
    ii = lax.broadcasted_iota(jnp.int32, (n, n), 0)
    jj = lax.broadcasted_iota(jnp.int32, (n, n), 1)
    same = (ii // CHUNK) == (jj // CHUNK)

    def put(idx, m):
        mask_ref[idx] = jnp.where(m, 1.0, 0.0)

    put(MASK_SAME, same)
    put(MASK_EYE, ii == jj)
    put(MASK_BE, jnp.logical_and(same, jj <= ii))
    put(MASK_BE + 1, jnp.logical_and(same, jj >= ii))
    for level in range(N_LEVELS):
        s = 1 << level
        blk = (ii // (2 * s)) == (jj // (2 * s))
        ih = (ii // s) & 1
        jh = (jj // s) & 1
        put(MASK_LEVEL + level, jnp.logical_and(blk, jnp.logical_and(ih == 1, jh == 0)))
        put(MASK_LEVEL + N_LEVELS + level, jnp.logical_and(blk, jnp.logical_and(ih == 0, jh == 1)))


def _gdn_prep_dir(q, k, v, gates, d, mask_ref):
    def mask(idx):
        return mask_ref[idx] > 0.5

    eye = mask(MASK_EYE)
    same = mask(MASK_SAME)
    be = mask(MASK_BE + d)
    be_t = mask(MASK_BE + 1 - d)
    k16 = k.astype(BF16)
    dn_t = (((1,), (1,)), ((), ()))
    kq = lax.dot_general(jnp.concatenate([k16, q.astype(BF16)], axis=0), k16, dn_t, preferred_element_type=F32)
    kk, qk0 = kq[:k.shape[0]], kq[k.shape[0]:]
    g_row, b_row = gates[d:d + 1, :], gates[2 + d:3 + d, :]
    g_col = _col_from_row(g_row, eye)
    b_col = _col_from_row(b_row, eye)
    gc_col = jnp.sum(jnp.where(be, g_row, 0.0), axis=1, keepdims=True)
    gc_row = jnp.sum(jnp.where(be_t, g_col, 0.0), axis=0, keepdims=True)
    tot_col = jnp.sum(jnp.where(same, g_row, 0.0), axis=1, keepdims=True)
    decay = jnp.where(be, jnp.exp(jnp.where(be, gc_col - gc_row, 0.0)), 0.0)
    mm = jnp.where(jnp.logical_and(be, jnp.logical_not(eye)), b_col * kk * decay, 0.0)
    qk = qk0 * decay
    t_inv = jnp.where(eye, 1.0, 0.0) - jnp.where(mask(MASK_LEVEL + d * N_LEVELS), mm, 0.0)
    for level in range(1, N_LEVELS):
        mc = jnp.where(mask(MASK_LEVEL + d * N_LEVELS + level), mm, 0.0)
        tb = t_inv.astype(BF16)
        y = jnp.dot(tb, mc.astype(BF16), preferred_element_type=F32)
        t_inv = t_inv - jnp.dot(y.astype(BF16), tb, preferred_element_type=F32)
    egc = jnp.exp(gc_col)
    kb = k * b_col
    rhs = jnp.concatenate([v * b_col, kb * egc], axis=1).astype(BF16)
    uw = jnp.dot(t_inv.astype(BF16), rhs, preferred_element_type=F32)
    dv = v.shape[1]
    return uw[:, :dv], uw[:, dv:], q * egc, k * jnp.exp(tot_col - gc_col), qk, jnp.exp(tot_col)


def _gdn_stream(x_refs, cw_refs, gates_ref, z_ref, onorm_ref, y_ref, states, scr):
    q_ref, k_ref, v_ref = x_refs
    mask_ref, qs_ref, ks_ref, vs_ref, o_ref = scr[:5]
    slots = (scr[5:9], scr[9:13])
    t_len = q_ref.shape[0]
    n_groups = t_len // GROUP
    cpg = GROUP // CHUNK
    dk = q_ref.shape[1]

    def l2(x):
        return x * lax.rsqrt(jnp.sum(x * x, axis=-1, keepdims=True) + EPS)

    rows_all = pl.ds(0, t_len)
    qs_ref[rows_all, :] = l2(_seq_conv_silu(q_ref[...].astype(F32), cw_refs[0][...])) * dk ** -0.5
    ks_ref[rows_all, :] = l2(_seq_conv_silu(k_ref[...].astype(F32), cw_refs[1][...]))
    vs_ref[rows_all, :] = _seq_conv_silu(v_ref[...].astype(F32), cw_refs[2][...])

    def load_group(g):
        r = _rows(g, GROUP)
        return qs_ref[r, :], ks_ref[r, :], vs_ref[r, :], gates_ref[g]

    def prep(slot, d, inputs):
        u_ref, wq_ref, lt_ref, gl_ref = slot
        u, w, qg, kg, qk, glast = _gdn_prep_dir(*inputs, d, mask_ref)
        u_ref[d] = u
        kg_t = kg.T
        for c in range(cpg):
            rc = slice(c * CHUNK, (c + 1) * CHUNK)
            wq_ref[d, c] = jnp.concatenate([w[rc], qg[rc]], axis=0).astype(BF16)
            lt_ref[d, c] = jnp.concatenate([kg_t[:, rc], qk[rc, rc]], axis=0).astype(BF16)
            gl_ref[d, c] = jnp.broadcast_to(glast[c * CHUNK:c * CHUNK + 8, :], (8, LANES))

    def scan_group(slot, d, state):
        u_ref, wq_ref, lt_ref, gl_ref = slot
        u = u_ref[d]
        order = range(cpg) if d == 0 else range(cpg - 1, -1, -1)
        o_parts = [None] * cpg
        for c in order:
            rc = slice(c * CHUNK, (c + 1) * CHUNK)
            ws = jnp.dot(wq_ref[d, c], state.astype(BF16), preferred_element_type=F32)
            v_new = u[rc] - ws[:CHUNK]
            r = jnp.dot(lt_ref[d, c], v_new.astype(BF16), preferred_element_type=F32)
            state = state * gl_ref[d, c][0:1, :] + r[:dk]
            o_parts[c] = ws[CHUNK:] + r[dk:]
        return jnp.concatenate(o_parts, axis=0), state

    def step(t, carry, cur, nxt, accumulate, prefetch):
        s_f, s_b = carry
        g_b = n_groups - 1 - t
        if prefetch:
            in_f, in_b = load_group(t + 1), load_group(g_b - 1)
            prep(nxt, 0, in_f)
            prep(nxt, 1, in_b)
        o_f, s_f = scan_group(cur, 0, s_f)
        o_b, s_b = scan_group(cur, 1, s_b)
        if accumulate:
            o_ref[_rows(t, GROUP), :] += o_f
            o_ref[_rows(g_b, GROUP), :] += o_b
        else:
            o_ref[_rows(t, GROUP), :] = o_f
            o_ref[_rows(g_b, GROUP), :] = o_b
        return s_f, s_b

    in_f, in_b = load_group(0), load_group(n_groups - 1)
    prep(slots[0], 0, in_f)
    prep(slots[0], 1, in_b)
    if n_groups == 1:
        o_f, s_f = scan_group(slots[0], 0, states[0])
        o_b, s_b = scan_group(slots[0], 1, states[1])
        o_ref[pl.ds(0, GROUP), :] = o_f + o_b
        states = (s_f, s_b)
    else:
        n_pairs = n_groups // 2
        assert n_groups == 2 * n_pairs and n_pairs % 2 == 0

        def pair(p, carry, accumulate, last=False):
            carry = step(2 * p, carry, slots[0], slots[1], accumulate, True)
            return step(2 * p + 1, carry, slots[1], slots[0], accumulate, not last)

        states = lax.fori_loop(0, n_pairs // 2, lambda p, c: pair(p, c, False), states)
        states = lax.fori_loop(n_pairs // 2, n_pairs - 1, lambda p, c: pair(p, c, True), states)
        states = pair(n_pairs - 1, states, True, last=True)

    o = o_ref[rows_all, :]
    o = o * lax.rsqrt(jnp.mean(o * o, axis=-1, keepdims=True) + EPS) * onorm_ref[...]
    z = z_ref[...].astype(F32)
    y_ref[...] = (o * (z * jax.nn.sigmoid(z))).astype(y_ref.dtype)
    return states


def _gdn_kernel(qc, kc, vc, zc, gc, ql, kl, vl, zl, gl, cwq, cwk, cwv, onorm, yc_ref, yl_ref, *scr):
    @pl.when(jnp.logical_and(pl.program_id(0) == 0, pl.program_id(1) == 0))
    def _():
        _gdn_init_masks(scr[0])

    cws = (cwq, cwk, cwv)
    zero = jnp.zeros((kc.shape[1], vc.shape[1]), F32)
    states = _gdn_stream((qc, kc, vc), cws, gc.at[0, 0], zc, onorm, yc_ref, (zero, zero), scr)
    _gdn_stream((ql, kl, vl), cws, gl.at[0, 0], zl, onorm, yl_ref, states, scr)


def gdn_mix(p, col_q, col_z, gates_c, gates_l, conv_w, out_norm):
    h, dk, dv = H_B, DK_B, DV_B
    assert dk == LANES and dv == LANES and col_q % LANES == 0 and col_z % LANES == 0
    cq, cz = col_q // LANES, col_z // LANES
    ctx_blk0 = N_LAT // CTX_LEN
    cpg = GROUP // CHUNK

    def slab(t, row0, col0):
        return pl.BlockSpec((t, LANES), lambda bi, hi: (row0 + bi, col0 + hi))

    def stream_specs(t, row0, gates):
        return [slab(t, row0, cq), slab(t, row0, cq + h), slab(t, row0, cq + 2 * h), slab(t, row0, cz),
                pl.BlockSpec((1, 1) + gates.shape[2:], lambda bi, hi: (bi, hi, 0, 0, 0))]

    in_specs = stream_specs(CTX_LEN, ctx_blk0, gates_c) + stream_specs(SEQ, 0, gates_l)
    in_specs += [pl.BlockSpec((SHORT_CONV, LANES), functools.partial(lambda bi, hi, off: (0, off + hi), off=off))
                 for off in (0, h, 2 * h)]
    in_specs.append(pl.BlockSpec((1, dv), lambda bi, hi: (0, 0)))
    slot = [
        pltpu.VMEM((2, GROUP, dv), F32),
        pltpu.VMEM((2, cpg, 2 * CHUNK, dk), BF16),
        pltpu.VMEM((2, cpg, dk + CHUNK, CHUNK), BF16),
        pltpu.VMEM((2, cpg, 8, LANES), F32),
    ]
    scratch = [
        pltpu.VMEM((N_MASKS, GROUP, GROUP), F32),
        pltpu.VMEM((SEQ, dk), F32), pltpu.VMEM((SEQ, dk), F32), pltpu.VMEM((SEQ, dv), F32),
        pltpu.VMEM((SEQ, dv), F32),
    ] + slot + slot
    return pl.pallas_call(
        _gdn_kernel,
        grid=(BATCH, h),
        in_specs=in_specs,
        out_specs=[pl.BlockSpec((CTX_LEN, dv), lambda bi, hi: (bi, hi)),
                   pl.BlockSpec((SEQ, dv), lambda bi, hi: (bi, hi))],
        out_shape=[jax.ShapeDtypeStruct((N_CTX, h * dv), BF16), jax.ShapeDtypeStruct((N_LAT, h * dv), BF16)],
        scratch_shapes=scratch,
        compiler_params=_cparams(("arbitrary", "arbitrary")),
        name="gdn_mix",
    )(p, p, p, p, gates_c, p, p, p, p, gates_l, conv_w, conv_w, conv_w, out_norm.reshape(1, dv))


def _tri_masks(n, rev):
    ii = lax.broadcasted_iota(jnp.int32, (n, n), 0)
    jj = lax.broadcasted_iota(jnp.int32, (n, n), 1)
    eye = ii == jj
    if rev:
        be, be_t = jj >= ii, ii >= jj
    else:
        be, be_t = jj <= ii, ii <= jj
    return eye, be, be_t


def _mlstm_chunk(q, k, v, i_row, f_row, carry, c_ref, rev):
    n_vec, m = carry
    n = q.shape[0]
    eye, be, be_t = _tri_masks(n, rev)
    f_col = _col_from_row(f_row, eye)
    i_col = _col_from_row(i_row, eye)
    b_col = jnp.sum(jnp.where(be, f_row, 0.0), axis=1, keepdims=True)
    b_row = jnp.sum(jnp.where(be_t, f_col, 0.0), axis=0, keepdims=True)
    b_tot = jnp.sum(f_row, axis=1, keepdims=True)
    d = jnp.where(be, b_col - b_row + i_row, -jnp.inf)
    inter = b_col + m
    m_t = jnp.maximum(inter, jnp.max(d, axis=1, keepdims=True))
    a_inter = jnp.exp(inter - m_t)
    qb = q.astype(BF16)
    kbf = k.astype(BF16)
    s = lax.dot_general(qb, kbf, (((1,), (1,)), ((), ())), preferred_element_type=F32)
    p = jnp.exp(d - m_t) * s
    c_mat = c_ref[...]
    num = a_inter * jnp.dot(qb, c_mat.astype(BF16), preferred_element_type=F32) + jnp.dot(
        p.astype(BF16), v.astype(BF16), preferred_element_type=F32)
    den = a_inter * jnp.sum(q * n_vec, axis=1, keepdims=True) + jnp.sum(p, axis=1, keepdims=True)
    h = num / jnp.maximum(jnp.abs(den), jnp.exp(-m_t))
    w_end_row = b_tot - b_row + i_row
    w_end_col = b_tot - b_col + i_col
    m_new = jnp.maximum(b_tot + m, jnp.max(w_end_row, axis=1, keepdims=True))
    g_old = jnp.exp(b_tot + m - m_new)
    gs_col = jnp.exp(w_end_col - m_new)
    upd = lax.dot_general(kbf, (gs_col * v).astype(BF16), (((0,), (0,)), ((), ())), preferred_element_type=F32)
    c_ref[...] = g_old * c_mat + upd
    n_new = g_old * n_vec + jnp.sum(gs_col * k, axis=0, keepdims=True)
    return h, (n_new, m_new)


def _mlstm_stream(q_ref, k_ref, v_ref, gates_ref, opre_ref, onorm_ref, y_ref, carries, o_ref, c_refs, unroll):
    t_len = q_ref.shape[0]
    n_chunks = t_len // CHUNK
    half = n_chunks // 2
    assert n_chunks == 2 * half
    scale = q_ref.shape[1] ** -0.5

    def chunk(c, carry, rev):
        r = _rows(c, CHUNK)
        gates = gates_ref[c]
        d = 1 if rev else 0
        return _mlstm_chunk(q_ref[r, :].astype(F32) * scale, k_ref[r, :].astype(F32), v_ref[r, :],
                            gates[d:d + 1, :], gates[2 + d:3 + d, :], carry, c_refs[d], rev)

    def step(t, carry, accumulate):
        cf, cb = carry
        c_b = n_chunks - 1 - t
        o_f, cf = chunk(t, cf, False)
        o_b, cb = chunk(c_b, cb, True)
        if accumulate:
            o_ref[_rows(t, CHUNK), :] += o_f
            o_ref[_rows(c_b, CHUNK), :] += o_b
        else:
            o_ref[_rows(t, CHUNK), :] = o_f
            o_ref[_rows(c_b, CHUNK), :] = o_b
        return cf, cb

    carries = lax.fori_loop(0, half, lambda t, c: step(t, c, False), carries, unroll=unroll)
    carries = lax.fori_loop(half, n_chunks, lambda t, c: step(t, c, True), carries, unroll=unroll)
    o = o_ref[pl.ds(0, t_len), :]
    o = o * lax.rsqrt(jnp.mean(o * o, axis=-1, keepdims=True) + EPS) * onorm_ref[0]
    y_ref[...] = (o * jax.nn.sigmoid(opre_ref[...].astype(F32))).astype(y_ref.dtype)
    return carries


def _mlstm_kernel(qc, kc, vc, oc, gc, ql, kl, vl, ol, gl, onorm, yc_ref, yl_ref, o_ref, cf_ref, cb_ref, *, unroll):
    cf_ref[...] = jnp.zeros(cf_ref.shape, F32)
    cb_ref[...] = jnp.zeros(cb_ref.shape, F32)
    init = (jnp.zeros((1, kc.shape[1]), F32), jnp.zeros((1, 1), F32))
    c_refs = (cf_ref, cb_ref)
    carries = _mlstm_stream(qc, kc, vc, gc.at[0, 0], oc, onorm, yc_ref, (init, init), o_ref, c_refs, unroll)
    _mlstm_stream(ql, kl, vl, gl.at[0, 0], ol, onorm, yl_ref, carries, o_ref, c_refs, unroll)


def mlstm_mix(p, col_q, col_k, col_v, col_o, gates_c, gates_l, out_norm, *, unroll=2):
    h, dk, dv = H_C, DK_C, DV_C
    assert col_q % dk == 0 and col_k % dk == 0 and col_v % dv == 0 and col_o % dv == 0
    ctx_blk0 = N_LAT // CTX_LEN

    def slab(t, row0, width, col):
        c0 = col // width
        return pl.BlockSpec((t, width), lambda bi, hi: (row0 + bi, c0 + hi))

    def stream_specs(t, row0, gates):
        return [slab(t, row0, dk, col_q), slab(t, row0, dk, col_k), slab(t, row0, dv, col_v), slab(t, row0, dv, col_o),
                pl.BlockSpec((1, 1) + gates.shape[2:], lambda bi, hi: (bi, hi, 0, 0, 0))]

    in_specs = stream_specs(CTX_LEN, ctx_blk0, gates_c) + stream_specs(SEQ, 0, gates_l)
    in_specs.append(pl.BlockSpec((1, 1, dv), lambda bi, hi: (hi, 0, 0)))
    return pl.pallas_call(
        functools.partial(_mlstm_kernel, unroll=unroll),
        grid=(BATCH, h),
        in_specs=in_specs,
        out_specs=[pl.BlockSpec((CTX_LEN, dv), lambda bi, hi: (bi, hi)),
                   pl.BlockSpec((SEQ, dv), lambda bi, hi: (bi, hi))],
        out_shape=[jax.ShapeDtypeStruct((N_CTX, h * dv), BF16), jax.ShapeDtypeStruct((N_LAT, h * dv), BF16)],
        scratch_shapes=[pltpu.VMEM((SEQ, dv), F32), pltpu.VMEM((dk, dv), F32), pltpu.VMEM((dk, dv), F32)],
        compiler_params=_cparams(("parallel", "parallel")),
        name="mlstm_mix",
    )(p, p, p, p, gates_c, p, p, p, p, gates_l, out_norm.reshape(h, 1, dv))


def _axial_rope(n_rows, rot_dim):
    r, col = jnp.meshgrid(jnp.arange(n_rows), jnp.arange(GRID_W), indexing="ij")
    r = r.reshape(-1).astype(F32)
    col = col.reshape(-1).astype(F32)
    n_freq = rot_dim // 4
    inv = ROPE_THETA ** (-jnp.arange(n_freq, dtype=F32) / n_freq)
    ang = jnp.concatenate([r[:, None] * inv, col[:, None] * inv], axis=-1)
    return jnp.cos(ang), jnp.sin(ang)


def _deinterleave(n):
    return np.concatenate([np.arange(0, n, 2), np.arange(1, n, 2)])


def _pad_lanes(a):
    pad = LANES - a.shape[-1]
    return a if pad == 0 else jnp.concatenate([a, jnp.zeros(a.shape[:-1] + (pad,), a.dtype)], axis=-1)


def _rope_tables(n_rows, rot_dim):
    cos, sin = _axial_rope(n_rows, rot_dim)
    return _pad_lanes(jnp.concatenate([cos, cos], axis=1)), _pad_lanes(jnp.concatenate([-sin, sin], axis=1))


def _gate_rows(g, width):
    b, t, h = g[0].shape
    rows = [a.transpose(0, 2, 1).reshape(b, h, t // width, 1, width) for a in g]
    rows += [jnp.zeros_like(rows[0])] * (8 - len(rows))
    return jnp.concatenate(rows, axis=3)


def _streams(a):
    c = a.shape[-1]
    return a[N_LAT:].reshape(BATCH, CTX_LEN, c), a[:N_LAT].reshape(BATCH, SEQ, c)


EV_CQ, EV_CKV, EV_QKV, EV_Z = 0, 1024, 1536, 7680
P_EVEN_MAIN = 9728
OD_MQ, OD_MK, OD_MV, OD_MO, OD_GQ, OD_GK, OD_GV = 0, 1024, 2048, 4096, 6144, 8192, 8704
P_ODD_MAIN = 9216


def _split_even(w_in):
    cq, ckv, kpe, qkv, z, ab = jnp.split(w_in, np.cumsum([1024, 512, 64, 6144, 2048]).tolist(), axis=1)
    kpe = kpe[:, _deinterleave(ROPE_A)]
    return jnp.concatenate([cq, ckv, qkv, z, kpe, ab], axis=1)


def _split_odd(w_in):
    mq, mk, mv, mo, mg, gq, gk, gv = jnp.split(
        w_in, np.cumsum([1024, 1024, 2048, 2048, 16, 2048, 512]).tolist(), axis=1)
    perm = _deinterleave(HD_D)
    per_head = lambda w: w.reshape(w.shape[0], -1, HD_D)[:, :, perm].reshape(w.shape)
    return jnp.concatenate([mq, mk, mv, mo, per_head(gq), per_head(gk), gv, _pad_lanes(mg)], axis=1)


def _mla_weights(w_qb, w_kvb):
    wq = w_qb.reshape(Q_LORA, H_A, NOPE_A + ROPE_A)
    q_pe = _pad_lanes(wq[:, :, NOPE_A:][:, :, _deinterleave(ROPE_A)])
    wq = jnp.concatenate([wq[:, :, :NOPE_A].reshape(Q_LORA, -1), q_pe.reshape(Q_LORA, -1)], axis=1)
    wkv = w_kvb.reshape(KV_LORA, H_A, NOPE_A + V_A)
    wkv = jnp.concatenate([wkv[:, :, :NOPE_A].reshape(KV_LORA, -1), wkv[:, :, NOPE_A:].reshape(KV_LORA, -1)], axis=1)
    return wq, wkv


def _even_mixers(p, pg, need_ctx, rope, w_qb, q_lora_norm, w_kvb, kv_lora_norm, q_head_norm, k_head_norm,
                 gdn_conv, a_log, dt_bias, out_norm):
    qn = rms_rows(p, EV_CQ // Q_LORA, Q_LORA, q_lora_norm)
    kvn = rms_rows(p, EV_CKV // KV_LORA, KV_LORA, kv_lora_norm)
    wq, wkv = _mla_weights(w_qb.astype(BF16), w_kvb.astype(BF16))
    q_all = matmul(qn, wq, tn=TN_PROJ, out_dtype=BF16)
    kv_all = matmul(kvn, wkv, tn=TN_PROJ, out_dtype=BF16)
    perm = _deinterleave(ROPE_A)
    parts = [
        QKPart(q_all, 0, kv_all, 0, True, q_head_norm[:NOPE_A], k_head_norm[:NOPE_A], NOPE_A, None),
        QKPart(q_all, H_A, pg, 0, False, _pad_lanes(q_head_norm[NOPE_A:][perm]),
               _pad_lanes(k_head_norm[NOPE_A:][perm]), ROPE_A, rope),
    ]
    a_lat = attention_mix(parts, kv_all, H_A, H_A, H_A, NOPE_A + ROPE_A, "lat", Q_SUB_MLA)
    def gdn_gates(ab):
        ab = ab.reshape(ab.shape[0], ab.shape[1], 4, H_B)
        g = -jnp.exp(a_log)[None, None] * jax.nn.softplus(ab[:, :, :2] + dt_bias[None, None])
        beta = jax.nn.sigmoid(ab[:, :, 2:])
        return _gate_rows([g[:, :, 0], g[:, :, 1], beta[:, :, 0], beta[:, :, 1]], GROUP)

    ab_c, ab_l = _streams(pg[:, ROPE_A:ROPE_A + 4 * H_B])
    b_ctx, b_lat = gdn_mix(p, EV_QKV, EV_Z, gdn_gates(ab_c), gdn_gates(ab_l), gdn_conv, out_norm)
    if not need_ctx:
        return [a_lat, b_lat], None
    a_ctx = attention_mix(parts, kv_all, H_A, H_A, H_A, NOPE_A + ROPE_A, "ctx", Q_SUB_MLA)
    return [a_lat, b_lat], [a_ctx, b_ctx]


def _odd_mixers(p, pg, need_ctx, rope, gate_bias, out_norm, q_norm, k_norm):
    def mlstm_gates(g):
        g = g.reshape(g.shape[0], g.shape[1], 4, H_C) + gate_bias
        g = GATE_CAP * jnp.tanh(g / GATE_CAP)
        return _gate_rows([g[:, :, 0], g[:, :, 1], jax.nn.log_sigmoid(g[:, :, 2]), jax.nn.log_sigmoid(g[:, :, 3])],
                          CHUNK)

    mg_c, mg_l = _streams(pg[:, :4 * H_C])
    c_ctx, c_lat = mlstm_mix(p, OD_MQ, OD_MK, OD_MV, OD_MO, mlstm_gates(mg_c), mlstm_gates(mg_l), out_norm)
    perm = _deinterleave(HD_D)
    parts = [QKPart(p, OD_GQ // LANES, p, OD_GK // LANES, True, q_norm[perm], k_norm[perm], HD_D, rope)]
    d_lat = attention_mix(parts, p, OD_GV // LANES, H_D, KV_D, HD_D, "lat", Q_SUB_GQA)
    if not need_ctx:
        return [c_lat, d_lat], None
    d_ctx = attention_mix(parts, p, OD_GV // LANES, H_D, KV_D, HD_D, "ctx", Q_SUB_GQA)
    return [c_lat, d_lat], [c_ctx, d_ctx]


def kernel(x, c, ctx, c_ctx, mod_down, mod_up, mod_bias, norm_mix, norm_ffn, ffn_up, ffn_conv, ffn_down,
           ev_w_in, ev_w_out, mla_w_qb, mla_q_lora_norm, mla_w_kvb, mla_kv_lora_norm, mla_q_head_norm,
           mla_k_head_norm, gdn_conv, gdn_a_log, gdn_dt_bias, gdn_out_norm, od_w_in, od_w_out,
           mlstm_gate_bias, mlstm_out_norm, gqa_q_norm, gqa_k_norm):
    n_rows = SEQ // GRID_W
    rope_a = _rope_tables(n_rows, ROPE_A)
    rope_d = _rope_tables(n_rows, HD_D)
    cond = jnp.concatenate([c, c_ctx[None, :], jnp.zeros((MOD_ROWS - BATCH - 1, D_MODEL), F32)], axis=0)
    mods = adaln_all(cond, mod_down, mod_up, mod_bias)
    xs = jnp.concatenate([x.reshape(N_LAT, D_MODEL), ctx.reshape(N_CTX, D_MODEL)], axis=0)
    ffn_up16, ffn_down16 = ffn_up.astype(BF16), ffn_down.astype(BF16)
    w_out16 = (ev_w_out.astype(BF16), od_w_out.astype(BF16))
    for layer in range(DEPTH):
        need_ctx = layer < DEPTH - 1
        j = layer // 2
        mod = mods[layer].reshape(MOD_ROWS, 1, 6 * D_MODEL)
        u = norm_mod(xs, norm_mix[layer], mod, 0)
        w_in = (ev_w_in if layer % 2 == 0 else od_w_in)[j].astype(BF16)
        w_in = _split_even(w_in) if layer % 2 == 0 else _split_odd(w_in)
        p, pg = matmul(u, w_in, tn=TN_PROJ, out_dtype=BF16, f32_tail=True)
        if layer % 2 == 0:
            y, y_ctx = _even_mixers(p, pg, need_ctx, rope_a, mla_w_qb[j], mla_q_lora_norm[j],
                                    mla_w_kvb[j], mla_kv_lora_norm[j], mla_q_head_norm[j], mla_k_head_norm[j],
                                    gdn_conv[j], gdn_a_log[j], gdn_dt_bias[j], gdn_out_norm[j])
        else:
            y, y_ctx = _odd_mixers(p, pg, need_ctx, rope_d, mlstm_gate_bias[j], mlstm_out_norm[j], gqa_q_norm[j],
                                   gqa_k_norm[j])
        xs = matmul(y, w_out16[layer % 2], tn=TN_OUT, out_dtype=F32, res=xs, mod=mod, gate_chunk=2,
                    ctx_parts=y_ctx, w_layer=j)
        u2 = norm_mod(xs, norm_ffn[layer], mod, 3, pad=HALO)
        hmid = ffn_up_proj(u2, ffn_up16, ffn_conv, layer)
        xs = matmul(hmid, ffn_down16, tn=TN_DOWN, out_dtype=F32, res=xs, mod=mod, gate_chunk=5,
                    single_buffer_a=True, w_layer=layer)
    return xs.reshape(BATCH, SEQ, D_MODEL)
```
